```python
import jax
import jax.numpy as jnp
from jax import lax
import numpy as np

D_MODEL = 1024
BATCH = 4
SEQ = 4096
DEPTH = 1

ATTN_HEADS = 8
ATTN_HEAD_DIM = 64
ATTN_WIDTH = ATTN_HEADS * ATTN_HEAD_DIM
DILATED_PATTERNS = ((128, 1), (512, 4), (2048, 16))
ATTN_BLOCK = 128
DN_HEADS = 4
DN_HEAD_DIM = 128
DN_WIDTH = DN_HEADS * DN_HEAD_DIM
DN_CHUNK = 64
CONV_WIDTH = 4
MIX_WIDTH = ATTN_WIDTH + DN_WIDTH
IN_SECTIONS = (ATTN_WIDTH, ATTN_WIDTH, ATTN_WIDTH, DN_WIDTH, DN_WIDTH, DN_WIDTH, DN_HEADS, DN_HEADS, DN_WIDTH)
IN_COLS = sum(IN_SECTIONS)
D_FF = 2816
NORM_EPS = 1e-6
L2_EPS = 1e-6
INIT_NOISE = 0.02

kernel_name = "hybrid_dilated_attn_gated_deltanet_macaron"


def rms_norm(x, gain):
    xf = x.astype(jnp.float32)
    y = xf * lax.rsqrt(jnp.mean(xf * xf, axis=-1, keepdims=True) + NORM_EPS)
    return (y * gain.astype(jnp.float32)).astype(x.dtype)


def swiglu(h, w_gate, w_up, w_down):
    return (jax.nn.silu(h @ w_gate) * (h @ w_up)) @ w_down


def alibi_slopes(n_heads):
    return jnp.asarray(np.array([2.0 ** (-8.0 * (i + 1) / n_heads) for i in range(n_heads)], dtype=np.float32))


def dilated_window_attention(q, k, v, slopes, window, dilation):
    B, S, H, Dh = q.shape
    L = S // dilation
    W = window // dilation
    nb = -(-L // ATTN_BLOCK)
    Lp = nb * ATTN_BLOCK

    def residues(t):
        return t.reshape(B, L, dilation, H, Dh).transpose(0, 2, 3, 1, 4).reshape(B * dilation, H, L, Dh)

    qr = jnp.pad(residues(q), ((0, 0), (0, 0), (0, Lp - L), (0, 0))).reshape(B * dilation, H, nb, ATTN_BLOCK, Dh)

    def band(t):
        t = jnp.pad(residues(t), ((0, 0), (0, 0), (ATTN_BLOCK, Lp - L), (0, 0)))
        t = t.reshape(B * dilation, H, nb + 1, ATTN_BLOCK, Dh)
        return jnp.concatenate([t[:, :, :-1], t[:, :, 1:]], axis=3)

    kb, vb = band(k), band(v)
    s = jnp.einsum('zhnqd,zhnkd->zhnqk', qr, kb) * (Dh ** -0.5)
    qi = jnp.arange(ATTN_BLOCK)[:, None]
    kj = jnp.arange(2 * ATTN_BLOCK)[None, :]
    steps = qi + ATTN_BLOCK - kj
    key_idx = jnp.arange(nb)[:, None, None] * ATTN_BLOCK + kj - ATTN_BLOCK
    valid = (steps >= 0) & (steps <= W) & (key_idx >= 0)
    bias = -slopes[:, None, None, None] * (steps * dilation).astype(jnp.float32)
    s = jnp.where(valid, s + bias, -jnp.inf)
    m = jnp.max(s, axis=-1)
    p = jnp.exp(s - m[..., None])
    l = jnp.sum(p, axis=-1)
    o = jnp.einsum('zhnqk,zhnkd->zhnqd', p, vb) / l[..., None]

    def back(t):
        extra = t.shape[4:]
        t = t.reshape(B, dilation, H, Lp, *extra)[:, :, :, :L]
        perm = (0, 3, 1, 2) + tuple(range(4, 4 + len(extra)))
        return t.transpose(perm).reshape(B, S, H, *extra)

    return back(o), back(m), back(l)


def dilated_attention(q, k, v):
    slopes = alibi_slopes(q.shape[2])
    outs, maxes, denoms = [], [], []
    for window, dilation in DILATED_PATTERNS:
        o, m, l = dilated_window_attention(q, k, v, slopes, window, dilation)
        outs.append(o)
        maxes.append(m)
        denoms.append(l)
    m_all = jnp.stack(maxes)
    wts = jnp.stack(denoms) * jnp.exp(m_all - jnp.max(m_all, axis=0, keepdims=True))
    return jnp.einsum('pbsh,pbshd->bshd', wts, jnp.stack(outs)) / jnp.sum(wts, axis=0)[..., None]


def gated_delta_rule(q, k, v, beta, g):
    B, S, H, Dk = q.shape
    Dv = v.shape[-1]
    C = DN_CHUNK
    N = S // C

    def chunks(t):
        return jnp.moveaxis(t.reshape(B, N, C, H, *t.shape[3:]), 3, 1)

    q = chunks(q) * (Dk ** -0.5)
    k, v, beta = chunks(k), chunks(v), chunks(beta)
    g = jnp.cumsum(chunks(g), axis=-1)
    causal = jnp.tril(jnp.ones((C, C), dtype=bool))
    strict = jnp.tril(jnp.ones((C, C), dtype=bool), k=-1)
    decay = jnp.exp(jnp.where(causal, g[..., :, None] - g[..., None, :], -jnp.inf))
    k_beta = k * beta[..., None]
    a_mat = jnp.where(strict, jnp.einsum('bhnck,bhnjk->bhncj', k_beta, k) * decay, 0.0)
    eye = jnp.eye(C, dtype=q.dtype)
    rhs = jnp.concatenate([v * beta[..., None], k_beta * jnp.exp(g)[..., None]], axis=-1)
    sol = lax.linalg.triangular_solve(eye + a_mat, rhs, left_side=True, lower=True)
    u, w = sol[..., :Dv], sol[..., Dv:]
    attn_intra = jnp.where(causal, jnp.einsum('bhnck,bhnjk->bhncj', q, k) * decay, 0.0)

    def step(state, xs):
        q_n, k_n, u_n, w_n, g_n, a_n = xs
        v_new = u_n - jnp.einsum('bhck,bhkv->bhcv', w_n, state)
        o_n = (jnp.einsum('bhck,bhkv->bhcv', q_n * jnp.exp(g_n)[..., None], state)
               + jnp.einsum('bhcj,bhjv->bhcv', a_n, v_new))
        g_last = g_n[..., -1]
        k_dec = k_n * jnp.exp(g_last[..., None] - g_n)[..., None]
        state = state * jnp.exp(g_last)[..., None, None] + jnp.einsum('bhck,bhcv->bhkv', k_dec, v_new)
        return state, o_n

    xs = tuple(jnp.moveaxis(t, 2, 0) for t in (q, k, u, w, g, attn_intra))
    state0 = jnp.zeros((B, H, Dk, Dv), q.dtype)
    _, o = lax.scan(step, state0, xs)
    return o.transpose(1, 0, 3, 2, 4).reshape(B, S, H, Dv)


def hybrid_mixer(h, w_in, conv_w, a_log, dt_bias, dn_norm, w_out):
    B, S, _ = h.shape
    f32 = jnp.float32
    split_points = np.cumsum(IN_SECTIONS)[:-1].tolist()
    aq, ak, av, dq, dk, dv, beta_raw, decay_raw, gate = jnp.split(h @ w_in, split_points, axis=-1)

    def attn_heads(t):
        return t.reshape(B, S, ATTN_HEADS, ATTN_HEAD_DIM).astype(f32)

    attn = dilated_attention(attn_heads(aq), attn_heads(ak), attn_heads(av))
    attn = attn.reshape(B, S, ATTN_WIDTH).astype(h.dtype)

    qkv = jnp.concatenate([dq, dk, dv], axis=-1)
    qkv_pad = jnp.pad(qkv, ((0, 0), (CONV_WIDTH - 1, 0), (0, 0)))
    conv = qkv_pad[:, 0:S] * conv_w[0]
    for j in range(1, CONV_WIDTH):
        conv = conv + qkv_pad[:, j:j + S] * conv_w[j]
    dq, dk, dv = jnp.split(jax.nn.silu(conv).astype(f32), 3, axis=-1)

    def dn_heads(t):
        return t.reshape(B, S, DN_HEADS, DN_HEAD_DIM)

    def l2n(t):
        return t * lax.rsqrt(jnp.sum(t * t, axis=-1, keepdims=True) + L2_EPS)

    beta = jax.nn.sigmoid(beta_raw.astype(f32))
    g = -jnp.exp(a_log.astype(f32)) * jax.nn.softplus(decay_raw.astype(f32) + dt_bias.astype(f32))
    o = gated_delta_rule(l2n(dn_heads(dq)), l2n(dn_heads(dk)), dn_heads(dv), beta, g)
    o = (o * lax.rsqrt(jnp.mean(o * o, axis=-1, keepdims=True) + NORM_EPS) * dn_norm.astype(f32)
         * jax.nn.silu(dn_heads(gate.astype(f32))))
    dn = o.reshape(B, S, DN_WIDTH).astype(h.dtype)

    return jnp.concatenate([attn, dn], axis=-1) @ w_out


def setup_inputs(seed: int = 0) -> dict:
    key = jax.random.key(seed)
    ks = jax.random.split(key, 17)
    f32 = jnp.float32
    L = DEPTH

    def normal(k, shape, scale):
        return jax.random.normal(k, shape, f32) * scale

    def gain(k, shape):
        return 1.0 + INIT_NOISE * jax.random.normal(k, shape, f32)

    dt = jnp.exp(jax.random.uniform(ks[9], (L, DN_HEADS), f32, float(np.log(1e-3)), float(np.log(1e-1))))
    return {
        "x": normal(ks[0], (BATCH, SEQ, D_MODEL), 1.0),
        "norm_ffn1": gain(ks[1], (L, D_MODEL)),
        "ffn1_gate": normal(ks[2], (L, D_MODEL, D_FF), D_MODEL ** -0.5),
        "ffn1_up": normal(ks[3], (L, D_MODEL, D_FF), D_MODEL ** -0.5),
        "ffn1_down": normal(ks[4], (L, D_FF, D_MODEL), D_FF ** -0.5),
        "norm_mix": gain(ks[5], (L, D_MODEL)),
        "w_in": normal(ks[6], (L, D_MODEL, IN_COLS), D_MODEL ** -0.5),
        "conv_w": normal(ks[7], (L, CONV_WIDTH, 3 * DN_WIDTH), CONV_WIDTH ** -0.5),
        "a_log": jnp.log(jax.random.uniform(ks[8], (L, DN_HEADS), f32, 1.0, 16.0)),
        "dt_bias": dt + jnp.log(-jnp.expm1(-dt)),
        "dn_norm": gain(ks[10], (L, DN_HEAD_DIM)),
        "w_out": normal(ks[11], (L, MIX_WIDTH, D_MODEL), MIX_WIDTH ** -0.5),
        "norm_ffn2": gain(ks[12], (L, D_MODEL)),
        "ffn2_gate": normal(ks[13], (L, D_MODEL, D_FF), D_MODEL ** -0.5),
        "ffn2_up": normal(ks[14], (L, D_MODEL, D_FF), D_MODEL ** -0.5),
        "ffn2_down": normal(ks[15], (L, D_FF, D_MODEL), D_FF ** -0.5),
        "norm_final": gain(ks[16], (D_MODEL,)),
    }


def reference(x, norm_ffn1, ffn1_gate, ffn1_up, ffn1_down, norm_mix, w_in, conv_w, a_log, dt_bias,
              dn_norm, w_out, norm_ffn2, ffn2_gate, ffn2_up, ffn2_down, norm_final):
    for i in range(DEPTH):
        x = x + 0.5 * swiglu(rms_norm(x, norm_ffn1[i]), ffn1_gate[i], ffn1_up[i], ffn1_down[i])
        x = x + hybrid_mixer(rms_norm(x, norm_mix[i]), w_in[i], conv_w[i], a_log[i], dt_bias[i],
                             dn_norm[i], w_out[i])
        x = x + 0.5 * swiglu(rms_norm(x, norm_ffn2[i]), ffn2_gate[i], ffn2_up[i], ffn2_down[i])
    return rms_norm(x, norm_final)
```

```python
import functools

import numpy as np
import jax
import jax.numpy as jnp
from jax import lax
from jax.experimental import pallas as pl
from jax.experimental.pallas import tpu as pltpu

F32 = jnp.float32
BF16 = jnp.bfloat16

ATTN_HEADS = 8
ATTN_HEAD_DIM = 64
ATTN_WIDTH = ATTN_HEADS * ATTN_HEAD_DIM
DILATED_PATTERNS = ((128, 1), (512, 4), (2048, 16))
ATTN_BLOCK = 128
DN_HEADS = 4
DN_HEAD_DIM = 128
DN_WIDTH = DN_HEADS * DN_HEAD_DIM
DN_CHUNK = 64
CONV_WIDTH = 4
NORM_EPS = 1e-6
L2_EPS = 1e-6

LANES = 128
SUBLANES = 8
VMEM_LIMIT_BYTES = 56 * 1024 * 1024

MASK_VALUE = -1e30

TOKEN_TILE = 512
FF_TILE = 256
DN_TOKENS = 256


def _rms(x):
    return x * lax.rsqrt(jnp.mean(x * x, axis=-1, keepdims=True) + NORM_EPS)


def _silu(x):
    return x * jax.nn.sigmoid(x)


def _mm(a, b):
    return jnp.dot(a.astype(BF16), b.astype(BF16), preferred_element_type=F32)


def _resident(shape):
    nd = len(shape)
    return pl.BlockSpec(shape, lambda *_: (0,) * nd, pipeline_mode=pl.Buffered(1))


def _params(*sem):
    return pltpu.CompilerParams(dimension_semantics=sem, vmem_limit_bytes=VMEM_LIMIT_BYTES)


def _ffn_kernel(x_ref, gain_ref, wg_ref, wu_ref, wd_ref, gfin_ref, o_ref, a_ref, *, final):
    x = x_ref[...]
    h = (_rms(x) * gain_ref[...]).astype(BF16)
    d_ff = wg_ref.shape[1]
    for f0 in range(0, d_ff, FF_TILE):
        g = jnp.dot(h, wg_ref[:, f0:f0 + FF_TILE], preferred_element_type=F32)
        u = jnp.dot(h, wu_ref[:, f0:f0 + FF_TILE], preferred_element_type=F32)
        a_ref[:, f0:f0 + FF_TILE] = (_silu(g) * u).astype(BF16)
    y = jnp.dot(a_ref[...], wd_ref[...], preferred_element_type=F32)
    out = x + 0.5 * y
    if final:
        out = _rms(out) * gfin_ref[...]
    o_ref[...] = out


def _ffn(x, gain, wg, wu, wd, gfin, *, final):
    t, d = x.shape
    d_ff = wg.shape[1]
    tm = TOKEN_TILE
    return pl.pallas_call(
        functools.partial(_ffn_kernel, final=final),
        grid=(t // tm,),
        in_specs=[
            pl.BlockSpec((tm, d), lambda i: (i, 0)),
            _resident((1, d)),
            _resident((d, d_ff)),
            _resident((d, d_ff)),
            _resident((d_ff, d)),
            _resident((1, d)),
        ],
        out_specs=pl.BlockSpec((tm, d), lambda i: (i, 0)),
        out_shape=jax.ShapeDtypeStruct((t, d), F32),
        scratch_shapes=[pltpu.VMEM((tm, d_ff), BF16)],
        compiler_params=_params("arbitrary"),
        name="ffn_final" if final else "ffn",
    )(x, gain, wg, wu, wd, gfin)


def _inproj_kernel(x_ref, gain_ref, wm_ref, ws_ref, cw_ref,
                   aq_ref, ak_ref, av_ref, dq_ref, dk_ref, dv_ref, gate_ref, small_ref,
                   xc_ref, *, tiles_per_seq):
    i = pl.program_id(0)
    tm = x_ref.shape[0]
    halo = SUBLANES
    h = (_rms(x_ref[...]) * gain_ref[...]).astype(BF16)

    def section(j, width=ATTN_WIDTH):
        return jnp.dot(h, wm_ref[:, j * width:(j + 1) * width], preferred_element_type=F32)

    aq_ref[...] = (section(0) * (ATTN_HEAD_DIM ** -0.5)).astype(BF16)
    ak_ref[...] = section(1).astype(BF16)
    av_ref[...] = section(2).astype(BF16)

    @pl.when(i % tiles_per_seq == 0)
    def _():
        xc_ref[0:halo, :] = jnp.zeros((halo, xc_ref.shape[1]), F32)

    @pl.when(i % tiles_per_seq != 0)
    def _():
        xc_ref[0:halo, :] = xc_ref[tm:tm + halo, :]

    for j in range(3):
        xc_ref[halo:halo + tm, j * DN_WIDTH:(j + 1) * DN_WIDTH] = section(3 + j)

    outs = (dq_ref, dk_ref, dv_ref)
    for c in range(3 * DN_HEADS):
        sl = slice(c * DN_HEAD_DIM, (c + 1) * DN_HEAD_DIM)
        conv = xc_ref[halo - 3:halo - 3 + tm, sl] * cw_ref[0:1, sl]
        for j in range(1, CONV_WIDTH):
            conv = conv + xc_ref[halo - 3 + j:halo - 3 + j + tm, sl] * cw_ref[j:j + 1, sl]
        a = _silu(conv)
        if c < 2 * DN_HEADS:
            a = a * lax.rsqrt(jnp.sum(a * a, axis=-1, keepdims=True) + L2_EPS)
        if c < DN_HEADS:
            a = a * (DN_HEAD_DIM ** -0.5)
        hs = slice((c % DN_HEADS) * DN_HEAD_DIM, (c % DN_HEADS + 1) * DN_HEAD_DIM)
        outs[c // DN_HEADS][:, hs] = a

    gate_ref[...] = _silu(section(6))
    small_ref[...] = jnp.dot(h, ws_ref[...], preferred_element_type=F32)


def _inproj(x, gain, w_main, w_small, conv_w, *, seq):
    t, d = x.shape
    tm = TOKEN_TILE
    row = lambda i: (i, 0)
    wide = lambda dt: jax.ShapeDtypeStruct((t, ATTN_WIDTH), dt)
    return pl.pallas_call(
        functools.partial(_inproj_kernel, tiles_per_seq=seq // tm),
        grid=(t // tm,),
        in_specs=[
            pl.BlockSpec((tm, d), row),
            _resident((1, d)),
            _resident(w_main.shape),
            _resident(w_small.shape),
            _resident(conv_w.shape),
        ],
        out_specs=[pl.BlockSpec((tm, ATTN_WIDTH), row)] * 7 + [pl.BlockSpec((tm, LANES), row)],
        out_shape=[wide(BF16)] * 3 + [wide(F32)] * 4 + [jax.ShapeDtypeStruct((t, LANES), F32)],
        scratch_shapes=[pltpu.VMEM((tm + 2 * SUBLANES, 3 * DN_WIDTH), F32)],
        compiler_params=_params("arbitrary"),
        name="inproj",
    )(x, gain, w_main, w_small, conv_w)


def _attn_kernel(q_ref, kp_ref, kc_ref, vp_ref, vc_ref, bias_ref, acc_ref, st_ref):
    q = q_ref[...]
    kk = jnp.concatenate([kp_ref[...], kc_ref[...]], axis=0)
    vv = jnp.concatenate([vp_ref[...], vc_ref[...]], axis=0)
    lane = lax.broadcasted_iota(jnp.int32, (ATTN_BLOCK, LANES), 1)
    st = jnp.zeros((ATTN_BLOCK, LANES), F32)
    outs = []
    for hd in range(ATTN_HEADS):
        sl = slice(hd * ATTN_HEAD_DIM, (hd + 1) * ATTN_HEAD_DIM)
        s = lax.dot_general(q[:, sl], kk[:, sl], (((1,), (1,)), ((), ())),
                            preferred_element_type=F32)
        s = s + bias_ref[hd]
        m = jnp.max(s, axis=-1, keepdims=True)
        p = jnp.exp(s - m)
        l = jnp.sum(p, axis=-1, keepdims=True)
        outs.append(jnp.dot(p.astype(BF16), vv[:, sl], preferred_element_type=F32))
        st = jnp.where(lane == hd, m, st)
        st = jnp.where(lane == ATTN_HEADS + hd, l, st)
    acc_ref[...] = jnp.concatenate(outs, axis=1)
    st_ref[...] = st


def _attn_bias(dilation):
    slopes = np.array([2.0 ** (-8.0 * (i + 1) / ATTN_HEADS) for i in range(ATTN_HEADS)],
                      dtype=np.float32)
    window = DILATED_PATTERNS[[d for _, d in DILATED_PATTERNS].index(dilation)][0] // dilation
    qi = np.arange(ATTN_BLOCK)[:, None]
    kj = np.arange(2 * ATTN_BLOCK)[None, :]
    steps = qi + ATTN_BLOCK - kj
    valid = (steps >= 0) & (steps <= window)
    bias = -slopes[:, None, None] * (steps * dilation).astype(np.float32)
    general = np.where(valid[None], bias, np.float32(MASK_VALUE))
    first = np.where((valid & (kj >= ATTN_BLOCK))[None], bias, np.float32(MASK_VALUE))
    return jnp.asarray(np.stack([first, general]).astype(np.float32))


def _attn_pattern(q, k, v, dilation):
    b, s, w = q.shape
    l = s // dilation
    nb = l // ATTN_BLOCK
    view = lambda t: t.reshape(b, l, dilation * w)
    cur = lambda n, bi, r: (bi, n, r)
    prev = lambda n, bi, r: (bi, jnp.maximum(n - 1, 0), r)
    blk = pl.BlockSpec((None, ATTN_BLOCK, w), cur)
    blk_prev = pl.BlockSpec((None, ATTN_BLOCK, w), prev)
    acc, st = pl.pallas_call(
        _attn_kernel,
        grid=(nb, b, dilation),
        in_specs=[
            blk, blk_prev, blk, blk_prev, blk,
            pl.BlockSpec((None, ATTN_HEADS, ATTN_BLOCK, 2 * ATTN_BLOCK),
                         lambda n, bi, r: (jnp.minimum(n, 1), 0, 0, 0)),
        ],
        out_specs=[blk, pl.BlockSpec((None, ATTN_BLOCK, LANES), cur)],
        out_shape=[jax.ShapeDtypeStruct((b, l, dilation * w), F32),
                   jax.ShapeDtypeStruct((b, l, dilation * LANES), F32)],
        compiler_params=_params("arbitrary", "arbitrary", "arbitrary"),
        name=f"attn_d{dilation}",
    )(view(q), view(k), view(k), view(v), view(v), _attn_bias(dilation))
    return acc.reshape(b * s, w), st.reshape(b * s, LANES)


def _bmm(a, b):
    return lax.dot_general(a.astype(BF16), b.astype(BF16), (((2,), (1,)), ((0,), (0,))),
                           preferred_element_type=F32)


def _bmm_nt(a, b):
    return lax.dot_general(a.astype(BF16), b.astype(BF16), (((2,), (2,)), ((0,), (0,))),
                           preferred_element_type=F32)


def _unit_lower_inverse(a_strict, row, col):
    c = a_strict.shape[-1]
    eye = (row == col).astype(F32)
    t = None
    size = 1
    while size < c:
        shift = size.bit_length() - 1
        join = ((row >> (shift + 1)) == (col >> (shift + 1))) & (((row >> shift) & 1) == 1) \
            & (((col >> shift) & 1) == 0)
        lk = jnp.where(join, a_strict, 0.0)
        if t is None:
            t = eye - lk
        else:
            t = t - _bmm(t, _bmm(lk, t))
        size *= 2
    return t


def _dn_kernel(q_ref, k_ref, v_ref, gate_ref, small_ref, arow_ref, dtrow_ref, nrm_ref,
               o_ref, s_ref):
    c = DN_CHUNK
    tc = q_ref.shape[0]
    g_chunks = tc // c

    @pl.when(pl.program_id(1) == 0)
    def _():
        s_ref[...] = jnp.zeros(s_ref.shape, F32)

    sm = small_ref[...]
    beta_all = jax.nn.sigmoid(sm)
    z = sm + dtrow_ref[...]
    softplus = jnp.maximum(z, 0.0) + jnp.log1p(jnp.exp(-jnp.abs(z)))
    g_all = -jnp.exp(arow_ref[...]) * softplus

    row = lax.broadcasted_iota(jnp.int32, (c, c), 0)
    col = lax.broadcasted_iota(jnp.int32, (c, c), 1)
    causal = row >= col
    strict = row > col

    for hd in range(DN_HEADS):
        sl = slice(hd * DN_HEAD_DIM, (hd + 1) * DN_HEAD_DIM)
        q = q_ref[:, sl].reshape(g_chunks, c, DN_HEAD_DIM)
        k = k_ref[:, sl].reshape(g_chunks, c, DN_HEAD_DIM)
        v = v_ref[:, sl].reshape(g_chunks, c, DN_HEAD_DIM)
        beta = beta_all[:, hd:hd + 1].reshape(g_chunks, c, 1)
        g = g_all[:, DN_HEADS + hd:DN_HEADS + hd + 1].reshape(g_chunks, c, 1)

        g_mat = jnp.broadcast_to(g, (g_chunks, c, c))
        g_row = jnp.sum(jnp.where(row == col, g_mat, 0.0), axis=1, keepdims=True)
        gc_col = jnp.sum(jnp.where(causal, g_row, 0.0), axis=2, keepdims=True)
        gc_row = jnp.sum(jnp.where(row <= col, g_mat, 0.0), axis=1, keepdims=True)
        decay = jnp.where(causal, jnp.exp(jnp.where(causal, gc_col - gc_row, 0.0)), 0.0)

        k_beta = k * beta
        a_mat = jnp.where(strict, _bmm_nt(k_beta, k) * decay, 0.0)
        attn = jnp.where(causal, _bmm_nt(q, k) * decay, 0.0)
        e_gc = jnp.exp(gc_col)
        rhs = jnp.concatenate([v * beta, k_beta * e_gc], axis=-1)
        sol = _bmm(_unit_lower_inverse(a_mat, row, col), rhs)
        u = sol[..., :DN_HEAD_DIM]
        w = sol[..., DN_HEAD_DIM:]
        g_last = gc_col[:, c - 1:c, :]
        q_g = q * e_gc
        k_dec = k * jnp.exp(g_last - gc_col)
        e_last = jnp.exp(g_last)

        state = s_ref[hd]
        outs = []
        for n in range(g_chunks):
            v_new = u[n] - _mm(w[n], state)
            outs.append(_mm(q_g[n], state) + _mm(attn[n], v_new))
            state = state * e_last[n] + lax.dot_general(
                k_dec[n].astype(BF16), v_new.astype(BF16), (((0,), (0,)), ((), ())),
                preferred_element_type=F32)
        s_ref[hd] = state

        o = jnp.concatenate(outs, axis=0)
        o = o * lax.rsqrt(jnp.mean(o * o, axis=-1, keepdims=True) + NORM_EPS)
        o_ref[:, sl] = (o * nrm_ref[...] * gate_ref[:, sl]).astype(o_ref.dtype)


def _deltanet(dq, dk, dv, gate, small, arow, dtrow, nrm, *, batch, seq):
    tc = DN_TOKENS
    per_seq = seq // tc
    row = lambda bi, j: (bi * per_seq + j, 0)
    wide = pl.BlockSpec((tc, DN_WIDTH), row)
    return pl.pallas_call(
        _dn_kernel,
        grid=(batch, per_seq),
        in_specs=[wide, wide, wide, wide, pl.BlockSpec((tc, LANES), row),
                  _resident((1, LANES)), _resident((1, LANES)), _resident((1, DN_HEAD_DIM))],
        out_specs=wide,
        out_shape=jax.ShapeDtypeStruct((batch * seq, DN_WIDTH), BF16),
        scratch_shapes=[pltpu.VMEM((DN_HEADS, DN_HEAD_DIM, DN_HEAD_DIM), F32)],
        compiler_params=_params("arbitrary", "arbitrary"),
        name="deltanet",
    )(dq, dk, dv, gate, small, arow, dtrow, nrm)


def _mixout_kernel(x_ref, a0_ref, a1_ref, a2_ref, s0_ref, s1_ref, s2_ref, dn_ref,
                   woa_ref, wod_ref, o_ref):
    stats = [s0_ref[...], s1_ref[...], s2_ref[...]]
    accs = [a0_ref, a1_ref, a2_ref]
    lane = lax.broadcasted_iota(jnp.int32, stats[0].shape, 1)
    m_all = jnp.maximum(jnp.maximum(stats[0], stats[1]), stats[2])
    scale = [jnp.exp(s - m_all) for s in stats]
    denom = [pltpu.roll(s, LANES - ATTN_HEADS, 1) for s in stats]
    total = scale[0] * denom[0] + scale[1] * denom[1] + scale[2] * denom[2]
    wts = [jnp.where(lane < ATTN_HEADS, e / total, 0.0) for e in scale]
    heads = []
    for hd in range(ATTN_HEADS):
        sl = slice(hd * ATTN_HEAD_DIM, (hd + 1) * ATTN_HEAD_DIM)
        heads.append(sum(accs[p][:, sl] * wts[p][:, hd:hd + 1] for p in range(3)))
    attn = jnp.concatenate(heads, axis=1).astype(BF16)
    y = jnp.dot(attn, woa_ref[...], preferred_element_type=F32)
    y = y + jnp.dot(dn_ref[...], wod_ref[...], preferred_element_type=F32)
    o_ref[...] = x_ref[...] + y


def _mixout(x, accs, stats, dn, wo_attn, wo_dn):
    t, d = x.shape
    tm = TOKEN_TILE
    row = lambda i: (i, 0)
    wide = pl.BlockSpec((tm, ATTN_WIDTH), row)
    st = pl.BlockSpec((tm, LANES), row)
    return pl.pallas_call(
        _mixout_kernel,
        grid=(t // tm,),
        in_specs=[pl.BlockSpec((tm, d), row), wide, wide, wide, st, st, st, wide,
                  _resident(wo_attn.shape), _resident(wo_dn.shape)],
        out_specs=pl.BlockSpec((tm, d), row),
        out_shape=jax.ShapeDtypeStruct((t, d), F32),
        compiler_params=_params("arbitrary"),
        name="mixout",
    )(x, *accs, *stats, dn, wo_attn, wo_dn)


def _lane_row(values, offset):
    return jnp.zeros((1, LANES), F32).at[0, offset:offset + values.shape[0]].set(values.astype(F32))


def kernel(x, norm_ffn1, ffn1_gate, ffn1_up, ffn1_down, norm_mix, w_in, conv_w, a_log, dt_bias,
           dn_norm, w_out, norm_ffn2, ffn2_gate, ffn2_up, ffn2_down, norm_final):
    batch, seq, d = x.shape
    depth = norm_ffn1.shape[0]
    assert seq % (max(dl for _, dl in DILATED_PATTERNS) * ATTN_BLOCK) == 0
    assert seq % TOKEN_TILE == 0 and seq % DN_TOKENS == 0
    xt = x.reshape(batch * seq, d)
    gfin = norm_final.reshape(1, d).astype(F32)
    small_lo = 6 * ATTN_WIDTH
    small_hi = small_lo + 2 * DN_HEADS

    for i in range(depth):
        row = lambda g: g.reshape(1, -1).astype(F32)
        xt = _ffn(xt, row(norm_ffn1[i]), ffn1_gate[i].astype(BF16), ffn1_up[i].astype(BF16),
                  ffn1_down[i].astype(BF16), gfin, final=False)

        wi = w_in[i]
        w_main = jnp.concatenate([wi[:, :small_lo], wi[:, small_hi:]], axis=1).astype(BF16)
        w_small = jnp.pad(wi[:, small_lo:small_hi], ((0, 0), (0, LANES - 2 * DN_HEADS))).astype(BF16)
        aq, ak, av, dq, dk, dv, gate, small = _inproj(
            xt, row(norm_mix[i]), w_main, w_small, conv_w[i].astype(F32), seq=seq)

        shape3 = (batch, seq, ATTN_WIDTH)
        accs, stats = [], []
        for _, dilation in DILATED_PATTERNS:
            acc, st = _attn_pattern(aq.reshape(shape3), ak.reshape(shape3), av.reshape(shape3),
                                    dilation)
            accs.append(acc)
            stats.append(st)

        dn = _deltanet(dq, dk, dv, gate, small, _lane_row(a_log[i], DN_HEADS),
                       _lane_row(dt_bias[i], DN_HEADS), row(dn_norm[i]), batch=batch, seq=seq)

        wo = w_out[i].astype(BF16)
        xt = _mixout(xt, accs, stats, dn, wo[:ATTN_WIDTH], wo[ATTN_WIDTH:])

        xt = _ffn(xt, row(norm_ffn2[i]), ffn2_gate[i].astype(BF16), ffn2_up[i].astype(BF16),
                  ffn2_down[i].astype(BF16), gfin, final=(i == depth - 1))

    return xt.reshape(batch, seq, d)
```

```python
import functools

import numpy as np
import jax
import jax.numpy as jnp
from jax import lax
from jax.experimental import pallas as pl
from jax.experimental.pallas import tpu as pltpu

F32 = jnp.float32
BF16 = jnp.bfloat16

ATTN_HEADS = 8
ATTN_HEAD_DIM = 64
ATTN_WIDTH = ATTN_HEADS * ATTN_HEAD_DIM
DILATED_PATTERNS = ((128, 1), (512, 4), (2048, 16))
ATTN_BLOCK = 128
DN_HEADS = 4
DN_HEAD_DIM = 128
DN_WIDTH = DN_HEADS * DN_HEAD_DIM
DN_CHUNK = 64
CONV_WIDTH = 4
NORM_EPS = 1e-6
L2_EPS = 1e-6

LANES = 128
SUBLANES = 8
VMEM_LIMIT_BYTES = 56 * 1024 * 1024

MASK_VALUE = -1e30

TOKEN_TILE = 512
FF_TILE = 256
DN_TOKENS = 256
HEADS_PER_SLAB = LANES // ATTN_HEAD_DIM
OUT_ROWS = 512


def _rms(x):
    return x * lax.rsqrt(jnp.mean(x * x, axis=-1, keepdims=True) + NORM_EPS)


def _silu(x):
    return x * jax.nn.sigmoid(x)


def _mm(a, b):
    return jnp.dot(a.astype(BF16), b.astype(BF16), preferred_element_type=F32)


def _resident(shape):
    nd = len(shape)
    return pl.BlockSpec(shape, lambda *_: (0,) * nd, pipeline_mode=pl.Buffered(1))


def _params(*sem):
    return pltpu.CompilerParams(dimension_semantics=sem, vmem_limit_bytes=VMEM_LIMIT_BYTES)


def _ffn_kernel(*refs, mix, final):
    if mix:
        x_ref, attn_ref, dn_ref, woa_ref, wod_ref = refs[:5]
        refs = refs[5:]
        x = x_ref[...] + jnp.dot(attn_ref[...], woa_ref[...], preferred_element_type=F32)
        x = x + jnp.dot(dn_ref[...], wod_ref[...], preferred_element_type=F32)
    else:
        x_ref = refs[0]
        refs = refs[1:]
        x = x_ref[...]
    gain_ref, wg_ref, wu_ref, wd_ref, gfin_ref, o_ref, a_ref = refs
    h = (_rms(x) * gain_ref[...]).astype(BF16)
    d_ff = wg_ref.shape[1]
    for f0 in range(0, d_ff, FF_TILE):
        g = jnp.dot(h, wg_ref[:, f0:f0 + FF_TILE], preferred_element_type=F32)
        u = jnp.dot(h, wu_ref[:, f0:f0 + FF_TILE], preferred_element_type=F32)
        a_ref[:, f0:f0 + FF_TILE] = (_silu(g) * u).astype(BF16)
    y = jnp.dot(a_ref[...], wd_ref[...], preferred_element_type=F32)
    out = x + 0.5 * y
    if final:
        out = _rms(out) * gfin_ref[...]
    o_ref[...] = out


def _ffn(x, gain, wg, wu, wd, gfin, *, final, mix=None):
    t, d = x.shape
    d_ff = wg.shape[1]
    tm = TOKEN_TILE
    row = lambda i: (i, 0)
    operands = [x]
    in_specs = [pl.BlockSpec((tm, d), row)]
    if mix is not None:
        attn, dn, wo_attn, wo_dn = mix
        operands += [attn, dn, wo_attn, wo_dn]
        in_specs += [pl.BlockSpec((tm, attn.shape[1]), row), pl.BlockSpec((tm, dn.shape[1]), row),
                     _resident(wo_attn.shape), _resident(wo_dn.shape)]
    operands += [gain, wg, wu, wd, gfin]
    in_specs += [_resident((1, d)), _resident((d, d_ff)), _resident((d, d_ff)),
                 _resident((d_ff, d)), _resident((1, d))]
    return pl.pallas_call(
        functools.partial(_ffn_kernel, mix=mix is not None, final=final),
        grid=(t // tm,),
        in_specs=in_specs,
        out_specs=pl.BlockSpec((tm, d), row),
        out_shape=jax.ShapeDtypeStruct((t, d), F32),
        scratch_shapes=[pltpu.VMEM((tm, d_ff), BF16)],
        compiler_params=_params("arbitrary"),
        name="ffn_mix" if mix is not None else "ffn",
    )(*operands)


def _inproj_kernel(x_ref, gain_ref, wm_ref, ws_ref, cw_ref,
                   aq_ref, ak_ref, av_ref, dq_ref, dk_ref, dv_ref, gate_ref, small_ref,
                   xc_ref, *, tiles_per_seq):
    i = pl.program_id(0)
    tm = x_ref.shape[0]
    halo = SUBLANES
    h = (_rms(x_ref[...]) * gain_ref[...]).astype(BF16)

    def section(j, width=ATTN_WIDTH):
        return jnp.dot(h, wm_ref[:, j * width:(j + 1) * width], preferred_element_type=F32)

    aq_ref[...] = section(0) * (ATTN_HEAD_DIM ** -0.5)
    ak_ref[...] = section(1)
    av_ref[...] = section(2)

    @pl.when(i % tiles_per_seq == 0)
    def _():
        xc_ref[0:halo, :] = jnp.zeros((halo, xc_ref.shape[1]), F32)

    @pl.when(i % tiles_per_seq != 0)
    def _():
        xc_ref[0:halo, :] = xc_ref[tm:tm + halo, :]

    for j in range(3):
        xc_ref[halo:halo + tm, j * DN_WIDTH:(j + 1) * DN_WIDTH] = section(3 + j)

    outs = (dq_ref, dk_ref, dv_ref)
    for c in range(3 * DN_HEADS):
        sl = slice(c * DN_HEAD_DIM, (c + 1) * DN_HEAD_DIM)
        conv = xc_ref[halo - 3:halo - 3 + tm, sl] * cw_ref[0:1, sl]
        for j in range(1, CONV_WIDTH):
            conv = conv + xc_ref[halo - 3 + j:halo - 3 + j + tm, sl] * cw_ref[j:j + 1, sl]
        a = _silu(conv)
        if c < 2 * DN_HEADS:
            a = a * lax.rsqrt(jnp.sum(a * a, axis=-1, keepdims=True) + L2_EPS)
        if c < DN_HEADS:
            a = a * (DN_HEAD_DIM ** -0.5)
        hs = slice((c % DN_HEADS) * DN_HEAD_DIM, (c % DN_HEADS + 1) * DN_HEAD_DIM)
        outs[c // DN_HEADS][:, hs] = a

    gate_ref[...] = _silu(section(6))
    small_ref[...] = jnp.dot(h, ws_ref[...], preferred_element_type=F32)


def _inproj(x, gain, w_main, w_small, conv_w, *, seq):
    t, d = x.shape
    tm = TOKEN_TILE
    row = lambda i: (i, 0)
    wide = jax.ShapeDtypeStruct((t, ATTN_WIDTH), F32)
    return pl.pallas_call(
        functools.partial(_inproj_kernel, tiles_per_seq=seq // tm),
        grid=(t // tm,),
        in_specs=[
            pl.BlockSpec((tm, d), row),
            _resident((1, d)),
            _resident(w_main.shape),
            _resident(w_small.shape),
            _resident(conv_w.shape),
        ],
        out_specs=[pl.BlockSpec((tm, ATTN_WIDTH), row)] * 7 + [pl.BlockSpec((tm, LANES), row)],
        out_shape=[wide] * 7 + [jax.ShapeDtypeStruct((t, LANES), F32)],
        scratch_shapes=[pltpu.VMEM((tm + 2 * SUBLANES, 3 * DN_WIDTH), F32)],
        compiler_params=_params("arbitrary"),
        name="inproj",
    )(x, gain, w_main, w_small, conv_w)


def _rows(ref, start, size, stride):
    if stride == 1:
        return ref[pl.ds(start, size), :]
    return ref[pl.ds(start, size, stride=stride), :]


def _put_rows(ref, start, size, stride, value):
    if stride == 1:
        ref[pl.ds(start, size), :] = value
    else:
        ref[pl.ds(start, size, stride=stride), :] = value


def _attn_blocks(refs, bias_ref, dilation, blocks, *, first, init):
    q_ref, k_ref, v_ref, acc_ref, m_refs, l_refs = refs
    d = dilation
    blk = ATTN_BLOCK
    lane = lax.broadcasted_iota(jnp.int32, (blk, LANES), 1)
    head_lanes = [(lane >= h * ATTN_HEAD_DIM) & (lane < (h + 1) * ATTN_HEAD_DIM)
                  for h in range(HEADS_PER_SLAB)]
    nkeys = blk if first else 2 * blk
    pending = []
    for r, n in blocks:
        q_start = r + d * blk * n
        k_start = q_start if first else q_start - d * blk
        if d == 1:
            q_start = pl.multiple_of(q_start, blk)
            k_start = pl.multiple_of(k_start, blk)
        q = _rows(q_ref, q_start, blk, d)
        kb = _rows(k_ref, k_start, nkeys, d).astype(BF16)
        vb = _rows(v_ref, k_start, nkeys, d).astype(BF16)
        acc_prev = None if init else _rows(acc_ref, q_start, blk, d)
        acc_new = None
        for h in range(HEADS_PER_SLAB):
            qh = jnp.where(head_lanes[h], q, 0.0).astype(BF16)
            s = lax.dot_general(qh, kb, (((1,), (1,)), ((), ())), preferred_element_type=F32)
            if first:
                s = s + bias_ref[0, h][:, blk:]
            else:
                s = s + bias_ref[1, h]
            m_cur = jnp.broadcast_to(jnp.max(s, axis=-1, keepdims=True), (blk, LANES))
            if init:
                m_new = m_cur
            else:
                m_prev = _rows(m_refs[h], q_start, blk, d)
                l_prev = _rows(l_refs[h], q_start, blk, d)
                m_new = jnp.maximum(m_prev, m_cur)
                alpha = jnp.exp(m_prev - m_new)
            p = jnp.exp(s - (m_new if first else jnp.concatenate([m_new, m_new], axis=1)))
            l_new = jnp.broadcast_to(jnp.sum(p, axis=-1, keepdims=True), (blk, LANES))
            pv = jnp.dot(p.astype(BF16), vb, preferred_element_type=F32)
            if not init:
                l_new = l_prev * alpha + l_new
                pv = acc_prev * alpha + pv
            acc_new = pv if acc_new is None else jnp.where(head_lanes[h], pv, acc_new)
            pending.append((m_refs[h], q_start, m_new))
            pending.append((l_refs[h], q_start, l_new))
        pending.append((acc_ref, q_start, acc_new))
    for ref, start, value in pending:
        _put_rows(ref, start, blk, d, value)


def _attn_kernel(q_ref, k_ref, v_ref, b1_ref, b4_ref, b16_ref, o_ref,
                 acc_ref, m0_ref, m1_ref, l0_ref, l1_ref):
    seq = q_ref.shape[0]
    refs = (q_ref, k_ref, v_ref, acc_ref, (m0_ref, m1_ref), (l0_ref, l1_ref))
    blk = ATTN_BLOCK
    for idx, ((_, d), bias_ref) in enumerate(zip(DILATED_PATTERNS, (b1_ref, b4_ref, b16_ref))):
        nb = seq // (d * blk)
        run = functools.partial(_attn_blocks, refs, bias_ref, d, init=(idx == 0))
        if d == 1:
            run([(0, 0)], first=True)
            pairs = (nb - 1) // 2

            def pair_body(i, carry, run=run):
                run([(0, 1 + 2 * i), (0, 2 + 2 * i)], first=False)
                return carry

            lax.fori_loop(0, pairs, pair_body, 0)
            if (nb - 1) % 2:
                run([(0, nb - 1)], first=False)
        else:
            def res_body(i, carry, run=run, nb=nb):
                r0 = 2 * i
                run([(r0, 0), (r0 + 1, 0)], first=True)

                def blk_body(n, c2):
                    run([(r0, n), (r0 + 1, n)], first=False)
                    return c2

                lax.fori_loop(1, nb, blk_body, 0)
                return carry

            lax.fori_loop(0, d // 2, res_body, 0)

    lane = lax.broadcasted_iota(jnp.int32, (OUT_ROWS, LANES), 1)

    def out_body(i, carry):
        rows = pl.ds(pl.multiple_of(i * OUT_ROWS, OUT_ROWS), OUT_ROWS)
        denom = jnp.where(lane < ATTN_HEAD_DIM, l0_ref[rows, :], l1_ref[rows, :])
        o_ref[rows, :] = (acc_ref[rows, :] / denom).astype(o_ref.dtype)
        return carry

    lax.fori_loop(0, seq // OUT_ROWS, out_body, 0)


def _attn_bias(window, dilation):
    slopes = np.array([2.0 ** (-8.0 * (i + 1) / ATTN_HEADS) for i in range(ATTN_HEADS)],
                      dtype=np.float32)
    qi = np.arange(ATTN_BLOCK)[:, None]
    kj = np.arange(2 * ATTN_BLOCK)[None, :]
    steps = qi + ATTN_BLOCK - kj
    valid = (steps >= 0) & (steps <= window // dilation)
    bias = -slopes[:, None, None] * (steps * dilation).astype(np.float32)
    general = np.where(valid[None], bias, np.float32(MASK_VALUE))
    first = np.where((valid & (kj >= ATTN_BLOCK))[None], bias, np.float32(MASK_VALUE))
    table = np.stack([first, general]).astype(np.float32)
    table = table.reshape(2, ATTN_HEADS // HEADS_PER_SLAB, HEADS_PER_SLAB, *table.shape[2:])
    return jnp.asarray(table.transpose(1, 0, 2, 3, 4))


def _attention(q, k, v, *, batch, seq):
    slabs = ATTN_WIDTH // LANES
    per_seq = pl.BlockSpec((seq, LANES), lambda bi, j: (bi, j))
    bias = [_attn_bias(w, d) for w, d in DILATED_PATTERNS]
    bias_spec = pl.BlockSpec((None,) + bias[0].shape[1:], lambda bi, j: (j, 0, 0, 0, 0))
    state = pltpu.VMEM((seq, LANES), F32)
    return pl.pallas_call(
        _attn_kernel,
        grid=(batch, slabs),
        in_specs=[per_seq, per_seq, per_seq, bias_spec, bias_spec, bias_spec],
        out_specs=per_seq,
        out_shape=jax.ShapeDtypeStruct((batch * seq, ATTN_WIDTH), BF16),
        scratch_shapes=[state] * (1 + 2 * HEADS_PER_SLAB),
        compiler_params=_params("arbitrary", "arbitrary"),
        name="attention",
    )(q, k, v, *bias)


def _bmm(a, b):
    return lax.dot_general(a.astype(BF16), b.astype(BF16), (((2,), (1,)), ((0,), (0,))),
                           preferred_element_type=F32)


def _bmm_nt(a, b):
    return lax.dot_general(a.astype(BF16), b.astype(BF16), (((2,), (2,)), ((0,), (0,))),
                           preferred_element_type=F32)


def _unit_lower_inverse(a_strict, row, col):
    c = a_strict.shape[-1]
    eye = (row == col).astype(F32)
    t = None
    size = 1
    while size < c:
        shift = size.bit_length() - 1
        join = ((row >> (shift + 1)) == (col >> (shift + 1))) & (((row >> shift) & 1) == 1) \
            & (((col >> shift) & 1) == 0)
        lk = jnp.where(join, a_strict, 0.0)
        if t is None:
            t = eye - lk
        else:
            t = t - _bmm(t, _bmm(lk, t))
        size *= 2
    return t


def _dn_kernel(q_ref, k_ref, v_ref, gate_ref, small_ref, arow_ref, dtrow_ref, nrm_ref,
               o_ref, s_ref):
    c = DN_CHUNK
    tc = q_ref.shape[0]
    g_chunks = tc // c

    @pl.when(pl.program_id(1) == 0)
    def _():
        s_ref[...] = jnp.zeros(s_ref.shape, F32)

    sm = small_ref[...]
    beta_all = jax.nn.sigmoid(sm)
    z = sm + dtrow_ref[...]
    softplus = jnp.maximum(z, 0.0) + jnp.log1p(jnp.exp(-jnp.abs(z)))
    g_all = -jnp.exp(arow_ref[...]) * softplus

    row = lax.broadcasted_iota(jnp.int32, (c, c), 0)
    col = lax.broadcasted_iota(jnp.int32, (c, c), 1)
    causal = row >= col
    strict = row > col

    for hd in range(DN_HEADS):
        sl = slice(hd * DN_HEAD_DIM, (hd + 1) * DN_HEAD_DIM)
        q = q_ref[:, sl].reshape(g_chunks, c, DN_HEAD_DIM)
        k = k_ref[:, sl].reshape(g_chunks, c, DN_HEAD_DIM)
        v = v_ref[:, sl].reshape(g_chunks, c, DN_HEAD_DIM)
        beta = beta_all[:, hd:hd + 1].reshape(g_chunks, c, 1)
        g = g_all[:, DN_HEADS + hd:DN_HEADS + hd + 1].reshape(g_chunks, c, 1)

        g_mat = jnp.broadcast_to(g, (g_chunks, c, c))
        g_row = jnp.sum(jnp.where(row == col, g_mat, 0.0), axis=1, keepdims=True)
        gc_col = jnp.sum(jnp.where(causal, g_row, 0.0), axis=2, keepdims=True)
        gc_row = jnp.sum(jnp.where(row <= col, g_mat, 0.0), axis=1, keepdims=True)
        decay = jnp.where(causal, jnp.exp(jnp.where(causal, gc_col - gc_row, 0.0)), 0.0)

        k_beta = k * beta
        a_mat = jnp.where(strict, _bmm_nt(k_beta, k) * decay, 0.0)
        attn = jnp.where(causal, _bmm_nt(q, k) * decay, 0.0)
        e_gc = jnp.exp(gc_col)
        rhs = jnp.concatenate([v * beta, k_beta * e_gc], axis=-1)
        sol = _bmm(_unit_lower_inverse(a_mat, row, col), rhs)
        u = sol[..., :DN_HEAD_DIM]
        w = sol[..., DN_HEAD_DIM:]
        g_last = gc_col[:, c - 1:c, :]
        q_g = q * e_gc
        k_dec = k * jnp.exp(g_last - gc_col)
        e_last = jnp.exp(g_last)

        state = s_ref[hd]
        outs = []
        for n in range(g_chunks):
            v_new = u[n] - _mm(w[n], state)
            outs.append(_mm(q_g[n], state) + _mm(attn[n], v_new))
            state = state * e_last[n] + lax.dot_general(
                k_dec[n].astype(BF16), v_new.astype(BF16), (((0,), (0,)), ((), ())),
                preferred_element_type=F32)
        s_ref[hd] = state

        o = jnp.concatenate(outs, axis=0)
        o = o * lax.rsqrt(jnp.mean(o * o, axis=-1, keepdims=True) + NORM_EPS)
        o_ref[:, sl] = (o * nrm_ref[...] * gate_ref[:, sl]).astype(o_ref.dtype)


def _deltanet(dq, dk, dv, gate, small, arow, dtrow, nrm, *, batch, seq):
    tc = DN_TOKENS
    per_seq = seq // tc
    row = lambda bi, j: (bi * per_seq + j, 0)
    wide = pl.BlockSpec((tc, DN_WIDTH), row)
    return pl.pallas_call(
        _dn_kernel,
        grid=(batch, per_seq),
        in_specs=[wide, wide, wide, wide, pl.BlockSpec((tc, LANES), row),
                  _resident((1, LANES)), _resident((1, LANES)), _resident((1, DN_HEAD_DIM))],
        out_specs=wide,
        out_shape=jax.ShapeDtypeStruct((batch * seq, DN_WIDTH), BF16),
        scratch_shapes=[pltpu.VMEM((DN_HEADS, DN_HEAD_DIM, DN_HEAD_DIM), F32)],
        compiler_params=_params("arbitrary", "arbitrary"),
        name="deltanet",
    )(dq, dk, dv, gate, small, arow, dtrow, nrm)


def _lane_row(values, offset):
    return jnp.zeros((1, LANES), F32).at[0, offset:offset + values.shape[0]].set(values.astype(F32))


def kernel(x, norm_ffn1, ffn1_gate, ffn1_up, ffn1_down, norm_mix, w_in, conv_w, a_log, dt_bias,
           dn_norm, w_out, norm_ffn2, ffn2_gate, ffn2_up, ffn2_down, norm_final):
    batch, seq, d = x.shape
    depth = norm_ffn1.shape[0]
    assert seq % (max(dl for _, dl in DILATED_PATTERNS) * ATTN_BLOCK) == 0
    assert seq % TOKEN_TILE == 0 and seq % DN_TOKENS == 0 and seq % OUT_ROWS == 0
    xt = x.reshape(batch * seq, d)
    gfin = norm_final.reshape(1, d).astype(F32)
    small_lo = 6 * ATTN_WIDTH
    small_hi = small_lo + 2 * DN_HEADS

    for i in range(depth):
        row = lambda g: g.reshape(1, -1).astype(F32)
        xt = _ffn(xt, row(norm_ffn1[i]), ffn1_gate[i].astype(BF16), ffn1_up[i].astype(BF16),
                  ffn1_down[i].astype(BF16), gfin, final=False)

        wi = w_in[i]
        w_main = jnp.concatenate([wi[:, :small_lo], wi[:, small_hi:]], axis=1).astype(BF16)
        w_small = jnp.pad(wi[:, small_lo:small_hi], ((0, 0), (0, LANES - 2 * DN_HEADS))).astype(BF16)
        aq, ak, av, dq, dk, dv, gate, small = _inproj(
            xt, row(norm_mix[i]), w_main, w_small, conv_w[i].astype(F32), seq=seq)

        attn = _attention(aq, ak, av, batch=batch, seq=seq)
        dn = _deltanet(dq, dk, dv, gate, small, _lane_row(a_log[i], DN_HEADS),
                       _lane_row(dt_bias[i], DN_HEADS), row(dn_norm[i]), batch=batch, seq=seq)

        wo = w_out[i].astype(BF16)
        xt = _ffn(xt, row(norm_ffn2[i]), ffn2_gate[i].astype(BF16), ffn2_up[i].astype(BF16),
                  ffn2_down[i].astype(BF16), gfin, final=(i == depth - 1),
                  mix=(attn, dn, wo[:ATTN_WIDTH], wo[ATTN_WIDTH:]))

    return xt.reshape(batch, seq, d)
```

```python
import functools

import numpy as np
import jax
import jax.numpy as jnp
from jax import lax
from jax.experimental import pallas as pl
from jax.experimental.pallas import tpu as pltpu

F32 = jnp.float32
BF16 = jnp.bfloat16

ATTN_HEADS = 8
ATTN_HEAD_DIM = 64
ATTN_WIDTH = ATTN_HEADS * ATTN_HEAD_DIM
DILATED_PATTERNS = ((128, 1), (512, 4), (2048, 16))
ATTN_BLOCK = 128
DN_HEADS = 4
DN_HEAD_DIM = 128
DN_WIDTH = DN_HEADS * DN_HEAD_DIM
DN_CHUNK = 64
CONV_WIDTH = 4
NORM_EPS = 1e-6
L2_EPS = 1e-6

LANES = 128
SUBLANES = 8
VMEM_LIMIT_BYTES = 56 * 1024 * 1024

MASK_VALUE = -1e30

TOKEN_TILE = 512
FF_TILE = 256
DN_TOKENS = 512
HEADS_PER_SLAB = LANES // ATTN_HEAD_DIM
OUT_ROWS = 512
ATTN_GROUP = 4


def _rms(x):
    return x * lax.rsqrt(jnp.mean(x * x, axis=-1, keepdims=True) + NORM_EPS)


def _silu(x):
    return x * jax.nn.sigmoid(x)


def _mm(a, b):
    return jnp.dot(a.astype(BF16), b.astype(BF16), preferred_element_type=F32)


def _resident(shape):
    nd = len(shape)
    return pl.BlockSpec(shape, lambda *_: (0,) * nd, pipeline_mode=pl.Buffered(1))


def _params(*sem):
    return pltpu.CompilerParams(dimension_semantics=sem, vmem_limit_bytes=VMEM_LIMIT_BYTES)


def _ffn_kernel(*refs, mix, final):
    if mix:
        x_ref, attn_ref, dn_ref, woa_ref, wod_ref = refs[:5]
        refs = refs[5:]
        x = x_ref[...] + jnp.dot(attn_ref[...], woa_ref[...], preferred_element_type=F32)
        x = x + jnp.dot(dn_ref[...], wod_ref[...], preferred_element_type=F32)
    else:
        x_ref = refs[0]
        refs = refs[1:]
        x = x_ref[...]
    gain_ref, wg_ref, wu_ref, wd_ref, gfin_ref, o_ref, a_ref = refs
    h = (_rms(x) * gain_ref[...]).astype(BF16)
    d_ff = wg_ref.shape[1]
    for f0 in range(0, d_ff, FF_TILE):
        g = jnp.dot(h, wg_ref[:, f0:f0 + FF_TILE], preferred_element_type=F32)
        u = jnp.dot(h, wu_ref[:, f0:f0 + FF_TILE], preferred_element_type=F32)
        a_ref[:, f0:f0 + FF_TILE] = (_silu(g) * u).astype(BF16)
    y = jnp.dot(a_ref[...], wd_ref[...], preferred_element_type=F32)
    out = x + 0.5 * y
    if final:
        out = _rms(out) * gfin_ref[...]
    o_ref[...] = out


def _ffn(x, gain, wg, wu, wd, gfin, *, final, mix=None):
    t, d = x.shape
    d_ff = wg.shape[1]
    tm = TOKEN_TILE
    row = lambda i: (i, 0)
    operands = [x]
    in_specs = [pl.BlockSpec((tm, d), row)]
    if mix is not None:
        attn, dn, wo_attn, wo_dn = mix
        operands += [attn, dn, wo_attn, wo_dn]
        in_specs += [pl.BlockSpec((tm, attn.shape[1]), row), pl.BlockSpec((tm, dn.shape[1]), row),
                     _resident(wo_attn.shape), _resident(wo_dn.shape)]
    operands += [gain, wg, wu, wd, gfin]
    in_specs += [_resident((1, d)), _resident((d, d_ff)), _resident((d, d_ff)),
                 _resident((d_ff, d)), _resident((1, d))]
    return pl.pallas_call(
        functools.partial(_ffn_kernel, mix=mix is not None, final=final),
        grid=(t // tm,),
        in_specs=in_specs,
        out_specs=pl.BlockSpec((tm, d), row),
        out_shape=jax.ShapeDtypeStruct((t, d), F32),
        scratch_shapes=[pltpu.VMEM((tm, d_ff), BF16)],
        compiler_params=_params("arbitrary"),
        name="ffn_mix" if mix is not None else "ffn",
    )(*operands)


def _inproj_kernel(x_ref, gain_ref, wm_ref, ws_ref, cw_ref,
                   aq_ref, ak_ref, av_ref, dq_ref, dk_ref, dv_ref, gate_ref, small_ref,
                   xc_ref, *, tiles_per_seq):
    i = pl.program_id(0)
    tm = x_ref.shape[0]
    halo = SUBLANES
    h = (_rms(x_ref[...]) * gain_ref[...]).astype(BF16)

    def section(j, width=ATTN_WIDTH):
        return jnp.dot(h, wm_ref[:, j * width:(j + 1) * width], preferred_element_type=F32)

    aq_ref[...] = section(0) * (ATTN_HEAD_DIM ** -0.5)
    ak_ref[...] = section(1)
    av_ref[...] = section(2)

    @pl.when(i % tiles_per_seq == 0)
    def _():
        xc_ref[0:halo, :] = jnp.zeros((halo, xc_ref.shape[1]), F32)

    @pl.when(i % tiles_per_seq != 0)
    def _():
        xc_ref[0:halo, :] = xc_ref[tm:tm + halo, :]

    for j in range(3):
        xc_ref[halo:halo + tm, j * DN_WIDTH:(j + 1) * DN_WIDTH] = section(3 + j)

    outs = (dq_ref, dk_ref, dv_ref)
    for c in range(3 * DN_HEADS):
        sl = slice(c * DN_HEAD_DIM, (c + 1) * DN_HEAD_DIM)
        conv = xc_ref[halo - 3:halo - 3 + tm, sl] * cw_ref[0:1, sl]
        for j in range(1, CONV_WIDTH):
            conv = conv + xc_ref[halo - 3 + j:halo - 3 + j + tm, sl] * cw_ref[j:j + 1, sl]
        a = _silu(conv)
        if c < 2 * DN_HEADS:
            a = a * lax.rsqrt(jnp.sum(a * a, axis=-1, keepdims=True) + L2_EPS)
        if c < DN_HEADS:
            a = a * (DN_HEAD_DIM ** -0.5)
        hs = slice((c % DN_HEADS) * DN_HEAD_DIM, (c % DN_HEADS + 1) * DN_HEAD_DIM)
        outs[c // DN_HEADS][:, hs] = a

    gate_ref[...] = _silu(section(6))
    small_ref[...] = jnp.dot(h, ws_ref[...], preferred_element_type=F32)


def _inproj(x, gain, w_main, w_small, conv_w, *, seq):
    t, d = x.shape
    tm = TOKEN_TILE
    row = lambda i: (i, 0)
    wide = jax.ShapeDtypeStruct((t, ATTN_WIDTH), F32)
    return pl.pallas_call(
        functools.partial(_inproj_kernel, tiles_per_seq=seq // tm),
        grid=(t // tm,),
        in_specs=[
            pl.BlockSpec((tm, d), row),
            _resident((1, d)),
            _resident(w_main.shape),
            _resident(w_small.shape),
            _resident(conv_w.shape),
        ],
        out_specs=[pl.BlockSpec((tm, ATTN_WIDTH), row)] * 7 + [pl.BlockSpec((tm, LANES), row)],
        out_shape=[wide] * 7 + [jax.ShapeDtypeStruct((t, LANES), F32)],
        scratch_shapes=[pltpu.VMEM((tm + 2 * SUBLANES, 3 * DN_WIDTH), F32)],
        compiler_params=_params("arbitrary"),
        name="inproj",
    )(x, gain, w_main, w_small, conv_w)


def _rows(ref, start, size, stride):
    if stride == 1:
        return ref[pl.ds(start, size), :]
    return ref[pl.ds(start, size, stride=stride), :]


def _put_rows(ref, start, size, stride, value):
    if stride == 1:
        ref[pl.ds(start, size), :] = value
    else:
        ref[pl.ds(start, size, stride=stride), :] = value


def _attn_group(refs, stage, bias_ref, dilation, blocks, *, init):
    q_ref, k_ref, v_ref, acc_ref, m_refs, l_refs = refs
    s_ref, p_ref, alpha_ref = stage
    d = dilation
    blk = ATTN_BLOCK
    lane = lax.broadcasted_iota(jnp.int32, (blk, LANES), 1)
    head_lanes = [(lane >= h * ATTN_HEAD_DIM) & (lane < (h + 1) * ATTN_HEAD_DIM)
                  for h in range(HEADS_PER_SLAB)]
    starts = []
    for r, n in blocks:
        q_start = r + d * blk * n
        k_start = jnp.maximum(q_start - d * blk, r)
        if d == 1:
            q_start = pl.multiple_of(q_start, blk)
            k_start = pl.multiple_of(k_start, blk)
        starts.append((q_start, k_start, jnp.minimum(n, 1)))

    for i, (q_start, k_start, _) in enumerate(starts):
        q = _rows(q_ref, q_start, blk, d)
        kb = _rows(k_ref, k_start, 2 * blk, d).astype(BF16)
        for h in range(HEADS_PER_SLAB):
            qh = jnp.where(head_lanes[h], q, 0.0).astype(BF16)
            s_ref[i * HEADS_PER_SLAB + h] = lax.dot_general(
                qh, kb, (((1,), (1,)), ((), ())), preferred_element_type=F32)

    pending = []
    for i, (q_start, _, table) in enumerate(starts):
        for h in range(HEADS_PER_SLAB):
            t = i * HEADS_PER_SLAB + h
            s = s_ref[t] + bias_ref[table, h]
            m_new = jnp.broadcast_to(jnp.max(s, axis=-1, keepdims=True), (blk, LANES))
            if not init:
                m_prev = _rows(m_refs[h], q_start, blk, d)
                m_new = jnp.maximum(m_prev, m_new)
                alpha = jnp.exp(m_prev - m_new)
                alpha_ref[t] = alpha
            p = jnp.exp(s - jnp.concatenate([m_new, m_new], axis=1))
            l_new = jnp.broadcast_to(jnp.sum(p, axis=-1, keepdims=True), (blk, LANES))
            if not init:
                l_new = _rows(l_refs[h], q_start, blk, d) * alpha + l_new
            p_ref[t] = p.astype(BF16)
            pending.append((m_refs[h], q_start, m_new))
            pending.append((l_refs[h], q_start, l_new))

    for i, (q_start, k_start, _) in enumerate(starts):
        vb = _rows(v_ref, k_start, 2 * blk, d).astype(BF16)
        acc_prev = None if init else _rows(acc_ref, q_start, blk, d)
        acc_new = None
        for h in range(HEADS_PER_SLAB):
            t = i * HEADS_PER_SLAB + h
            pv = jnp.dot(p_ref[t], vb, preferred_element_type=F32)
            if not init:
                pv = acc_prev * alpha_ref[t] + pv
            acc_new = pv if acc_new is None else jnp.where(head_lanes[h], pv, acc_new)
        pending.append((acc_ref, q_start, acc_new))

    for ref, start, value in pending:
        _put_rows(ref, start, blk, d, value)


def _attn_kernel(q_ref, k_ref, v_ref, b1_ref, b4_ref, b16_ref, o_ref,
                 acc_ref, m0_ref, m1_ref, l0_ref, l1_ref, s_ref, p_ref, alpha_ref):
    seq = q_ref.shape[0]
    refs = (q_ref, k_ref, v_ref, acc_ref, (m0_ref, m1_ref), (l0_ref, l1_ref))
    stage = (s_ref, p_ref, alpha_ref)
    blk = ATTN_BLOCK
    grp = ATTN_GROUP
    patterns = sorted(zip(DILATED_PATTERNS, (b1_ref, b4_ref, b16_ref)), key=lambda e: -e[0][1])
    for idx, ((_, d), bias_ref) in enumerate(patterns):
        nb = seq // (d * blk)
        run = functools.partial(_attn_group, refs, stage, bias_ref, d, init=(idx == 0))
        if d < grp:
            per_res = nb // grp

            def body(i, carry, run=run, per_res=per_res):
                r = i // per_res
                n0 = (i % per_res) * grp
                run([(r, n0 + j) for j in range(grp)])
                return carry

            lax.fori_loop(0, d * per_res, body, 0)
        else:
            per_blk = d // grp

            def body(i, carry, run=run, per_blk=per_blk):
                n = i // per_blk
                r0 = (i % per_blk) * grp
                run([(r0 + j, n) for j in range(grp)])
                return carry

            lax.fori_loop(0, nb * per_blk, body, 0)

    lane = lax.broadcasted_iota(jnp.int32, (OUT_ROWS, LANES), 1)

    def out_body(i, carry):
        rows = pl.ds(pl.multiple_of(i * OUT_ROWS, OUT_ROWS), OUT_ROWS)
        denom = jnp.where(lane < ATTN_HEAD_DIM, l0_ref[rows, :], l1_ref[rows, :])
        o_ref[rows, :] = (acc_ref[rows, :] / denom).astype(o_ref.dtype)
        return carry

    lax.fori_loop(0, seq // OUT_ROWS, out_body, 0)


def _attn_bias(window, dilation):
    slopes = np.array([2.0 ** (-8.0 * (i + 1) / ATTN_HEADS) for i in range(ATTN_HEADS)],
                      dtype=np.float32)
    qi = np.arange(ATTN_BLOCK)[:, None]
    kj = np.arange(2 * ATTN_BLOCK)[None, :]
    tables = []
    for steps in (qi - kj, qi + ATTN_BLOCK - kj):
        valid = (steps >= 0) & (steps <= window // dilation)
        bias = -slopes[:, None, None] * (steps * dilation).astype(np.float32)
        tables.append(np.where(valid[None], bias, np.float32(MASK_VALUE)))
    table = np.stack(tables).astype(np.float32)
    table = table.reshape(2, ATTN_HEADS // HEADS_PER_SLAB, HEADS_PER_SLAB, *table.shape[2:])
    return jnp.asarray(table.transpose(1, 0, 2, 3, 4))


def _attention(q, k, v, *, batch, seq):
    slabs = ATTN_WIDTH // LANES
    per_seq = pl.BlockSpec((seq, LANES), lambda bi, j: (bi, j))
    bias = [_attn_bias(w, d) for w, d in DILATED_PATTERNS]
    bias_spec = pl.BlockSpec((None,) + bias[0].shape[1:], lambda bi, j: (j, 0, 0, 0, 0))
    state = pltpu.VMEM((seq, LANES), F32)
    tiles = ATTN_GROUP * HEADS_PER_SLAB
    return pl.pallas_call(
        _attn_kernel,
        grid=(batch, slabs),
        in_specs=[per_seq, per_seq, per_seq, bias_spec, bias_spec, bias_spec],
        out_specs=per_seq,
        out_shape=jax.ShapeDtypeStruct((batch * seq, ATTN_WIDTH), BF16),
        scratch_shapes=[state] * (1 + 2 * HEADS_PER_SLAB) + [
            pltpu.VMEM((tiles, ATTN_BLOCK, 2 * ATTN_BLOCK), F32),
            pltpu.VMEM((tiles, ATTN_BLOCK, 2 * ATTN_BLOCK), BF16),
            pltpu.VMEM((tiles, ATTN_BLOCK, LANES), F32),
        ],
        compiler_params=_params("arbitrary", "arbitrary"),
        name="attention",
    )(q, k, v, *bias)


def _bmm(a, b):
    return lax.dot_general(a.astype(BF16), b.astype(BF16), (((2,), (1,)), ((0,), (0,))),
                           preferred_element_type=F32)


def _bmm_nt(a, b):
    return lax.dot_general(a.astype(BF16), b.astype(BF16), (((2,), (2,)), ((0,), (0,))),
                           preferred_element_type=F32)


def _unit_lower_inverse(a_strict, row, col):
    c = a_strict.shape[-1]
    eye = (row == col).astype(F32)
    t = None
    size = 1
    while size < c:
        shift = size.bit_length() - 1
        join = ((row >> (shift + 1)) == (col >> (shift + 1))) & (((row >> shift) & 1) == 1) \
            & (((col >> shift) & 1) == 0)
        lk = jnp.where(join, a_strict, 0.0)
        if t is None:
            t = eye - lk
        else:
            t = t - _bmm(t, _bmm(lk, t))
        size *= 2
    return t


def _dn_kernel(q_ref, k_ref, v_ref, gate_ref, small_ref, arow_ref, dtrow_ref, nrm_ref,
               o_ref, s_ref, wq_ref, u_ref, at_ref, kd_ref):
    c = DN_CHUNK
    dh = DN_HEAD_DIM
    tc = q_ref.shape[0]
    g_chunks = tc // c
    nb = g_chunks * DN_HEADS

    @pl.when(pl.program_id(1) == 0)
    def _():
        s_ref[...] = jnp.zeros(s_ref.shape, F32)

    sm = small_ref[...]
    beta_all = jax.nn.sigmoid(sm)
    z = sm + dtrow_ref[...]
    softplus = jnp.maximum(z, 0.0) + jnp.log1p(jnp.exp(-jnp.abs(z)))
    g_all = -jnp.exp(arow_ref[...]) * softplus

    def per_head(ref):
        parts = [ref[:, hd * dh:(hd + 1) * dh].reshape(g_chunks, 1, c, dh) for hd in range(DN_HEADS)]
        return jnp.concatenate(parts, axis=1).reshape(nb, c, dh)

    def per_head_col(x, offset):
        parts = [x[:, offset + hd:offset + hd + 1].reshape(g_chunks, 1, c, 1)
                 for hd in range(DN_HEADS)]
        return jnp.concatenate(parts, axis=1).reshape(nb, c, 1)

    q = per_head(q_ref)
    k = per_head(k_ref)
    v = per_head(v_ref)
    beta = per_head_col(beta_all, 0)
    g = per_head_col(g_all, DN_HEADS)

    row = lax.broadcasted_iota(jnp.int32, (c, c), 0)
    col = lax.broadcasted_iota(jnp.int32, (c, c), 1)
    causal = row >= col
    strict = row > col

    g_mat = jnp.broadcast_to(g, (nb, c, c))
    g_row = jnp.sum(jnp.where(row == col, g_mat, 0.0), axis=1, keepdims=True)
    gc_col = jnp.sum(jnp.where(causal, g_row, 0.0), axis=2, keepdims=True)
    gc_row = jnp.sum(jnp.where(row <= col, g_mat, 0.0), axis=1, keepdims=True)
    decay = jnp.where(causal, jnp.exp(jnp.where(causal, gc_col - gc_row, 0.0)), 0.0)

    k_beta = k * beta
    e_gc = jnp.exp(gc_col)
    g_last = gc_col[:, c - 1:c, :]
    both = _bmm_nt(jnp.concatenate([k_beta, q], axis=1), k)
    a_mat = jnp.where(strict, both[:, :c] * decay, 0.0)
    at_ref[...] = jnp.where(causal, both[:, c:] * decay, 0.0).astype(BF16)
    rhs = jnp.concatenate([v * beta, k_beta * e_gc], axis=-1)
    sol = _bmm(_unit_lower_inverse(a_mat, row, col), rhs)
    u_ref[...] = sol[..., :dh]
    wq_ref[:, :c, :] = sol[..., dh:].astype(BF16)
    wq_ref[:, c:, :] = (q * e_gc).astype(BF16)
    kd_ref[...] = (k * jnp.exp(g_last - gc_col)).astype(BF16)
    e_last = jnp.exp(g_last)

    state = s_ref[...]
    outs = []
    for n in range(g_chunks):
        sl = slice(n * DN_HEADS, (n + 1) * DN_HEADS)
        ws = _bmm(wq_ref[sl], state)
        v_new = u_ref[sl] - ws[:, :c]
        outs.append(ws[:, c:] + _bmm(at_ref[sl], v_new))
        state = state * e_last[sl] + lax.dot_general(
            kd_ref[sl], v_new.astype(BF16), (((1,), (1,)), ((0,), (0,))),
            preferred_element_type=F32)
    s_ref[...] = state

    for hd in range(DN_HEADS):
        sl = slice(hd * dh, (hd + 1) * dh)
        o = jnp.concatenate([outs[n][hd] for n in range(g_chunks)], axis=0)
        o = o * lax.rsqrt(jnp.mean(o * o, axis=-1, keepdims=True) + NORM_EPS)
        o_ref[:, sl] = (o * nrm_ref[...] * gate_ref[:, sl]).astype(o_ref.dtype)


def _deltanet(dq, dk, dv, gate, small, arow, dtrow, nrm, *, batch, seq):
    tc = DN_TOKENS
    per_seq = seq // tc
    nb = (tc // DN_CHUNK) * DN_HEADS
    row = lambda bi, j: (bi * per_seq + j, 0)
    wide = pl.BlockSpec((tc, DN_WIDTH), row)
    return pl.pallas_call(
        _dn_kernel,
        grid=(batch, per_seq),
        in_specs=[wide, wide, wide, wide, pl.BlockSpec((tc, LANES), row),
                  _resident((1, LANES)), _resident((1, LANES)), _resident((1, DN_HEAD_DIM))],
        out_specs=wide,
        out_shape=jax.ShapeDtypeStruct((batch * seq, DN_WIDTH), BF16),
        scratch_shapes=[
            pltpu.VMEM((DN_HEADS, DN_HEAD_DIM, DN_HEAD_DIM), F32),
            pltpu.VMEM((nb, 2 * DN_CHUNK, DN_HEAD_DIM), BF16),
            pltpu.VMEM((nb, DN_CHUNK, DN_HEAD_DIM), F32),
            pltpu.VMEM((nb, DN_CHUNK, DN_CHUNK), BF16),
            pltpu.VMEM((nb, DN_CHUNK, DN_HEAD_DIM), BF16),
        ],
        compiler_params=_params("arbitrary", "arbitrary"),
        name="deltanet",
    )(dq, dk, dv, gate, small, arow, dtrow, nrm)


def _lane_row(values, offset):
    return jnp.zeros((1, LANES), F32).at[0, offset:offset + values.shape[0]].set(values.astype(F32))


def kernel(x, norm_ffn1, ffn1_gate, ffn1_up, ffn1_down, norm_mix, w_in, conv_w, a_log, dt_bias,
           dn_norm, w_out, norm_ffn2, ffn2_gate, ffn2_up, ffn2_down, norm_final):
    batch, seq, d = x.shape
    depth = norm_ffn1.shape[0]
    assert seq % (max(dl for _, dl in DILATED_PATTERNS) * ATTN_BLOCK) == 0
    assert seq % TOKEN_TILE == 0 and seq % DN_TOKENS == 0 and seq % OUT_ROWS == 0
    xt = x.reshape(batch * seq, d)
    gfin = norm_final.reshape(1, d).astype(F32)
    small_lo = 6 * ATTN_WIDTH
    small_hi = small_lo + 2 * DN_HEADS

    for i in range(depth):
        row = lambda g: g.reshape(1, -1).astype(F32)
        xt = _ffn(xt, row(norm_ffn1[i]), ffn1_gate[i].astype(BF16), ffn1_up[i].astype(BF16),
                  ffn1_down[i].astype(BF16), gfin, final=False)

        wi = w_in[i]
        w_main = jnp.concatenate([wi[:, :small_lo], wi[:, small_hi:]], axis=1).astype(BF16)
        w_small = jnp.pad(wi[:, small_lo:small_hi], ((0, 0), (0, LANES - 2 * DN_HEADS))).astype(BF16)
        aq, ak, av, dq, dk, dv, gate, small = _inproj(
            xt, row(norm_mix[i]), w_main, w_small, conv_w[i].astype(F32), seq=seq)

        attn = _attention(aq, ak, av, batch=batch, seq=seq)
        dn = _deltanet(dq, dk, dv, gate, small, _lane_row(a_log[i], DN_HEADS),
                       _lane_row(dt_bias[i], DN_HEADS), row(dn_norm[i]), batch=batch, seq=seq)

        wo = w_out[i].astype(BF16)
        xt = _ffn(xt, row(norm_ffn2[i]), ffn2_gate[i].astype(BF16), ffn2_up[i].astype(BF16),
                  ffn2_down[i].astype(BF16), gfin, final=(i == depth - 1),
                  mix=(attn, dn, wo[:ATTN_WIDTH], wo[ATTN_WIDTH:]))

    return xt.reshape(batch, seq, d)
```

```python
import functools

import numpy as np
import jax
import jax.numpy as jnp
from jax import lax
from jax.experimental import pallas as pl
from jax.experimental.pallas import tpu as pltpu

F32 = jnp.float32
BF16 = jnp.bfloat16

ATTN_HEADS = 8
ATTN_HEAD_DIM = 64
ATTN_WIDTH = ATTN_HEADS * ATTN_HEAD_DIM
DILATED_PATTERNS = ((128, 1), (512, 4), (2048, 16))
ATTN_BLOCK = 128
DN_HEADS = 4
DN_HEAD_DIM = 128
DN_WIDTH = DN_HEADS * DN_HEAD_DIM
DN_CHUNK = 64
CONV_WIDTH = 4
NORM_EPS = 1e-6
L2_EPS = 1e-6

LANES = 128
SUBLANES = 8
VMEM_LIMIT_BYTES = 56 * 1024 * 1024

MASK_VALUE = -1e30

TOKEN_TILE = 512
FFN_TILE = 1024
FF_TILE = 256
DN_TOKENS = 512
HEADS_PER_SLAB = LANES // ATTN_HEAD_DIM
OUT_ROWS = 512
ATTN_GROUP = 4


def _rms(x):
    return x * lax.rsqrt(jnp.mean(x * x, axis=-1, keepdims=True) + NORM_EPS)


def _silu(x):
    half = 0.5 * x
    return half + half * jnp.tanh(half)


def _mm(a, b):
    return jnp.dot(a.astype(BF16), b.astype(BF16), preferred_element_type=F32)


def _resident(shape):
    nd = len(shape)
    return pl.BlockSpec(shape, lambda *_: (0,) * nd, pipeline_mode=pl.Buffered(1))


def _params(*sem):
    return pltpu.CompilerParams(dimension_semantics=sem, vmem_limit_bytes=VMEM_LIMIT_BYTES)


def _ffn_kernel(*refs, mix, final):
    if mix:
        x_ref, attn_ref, dn_ref, woa_ref, wod_ref = refs[:5]
        refs = refs[5:]
        x = x_ref[...] + jnp.dot(attn_ref[...], woa_ref[...], preferred_element_type=F32)
        x = x + jnp.dot(dn_ref[...], wod_ref[...], preferred_element_type=F32)
    else:
        x_ref = refs[0]
        refs = refs[1:]
        x = x_ref[...]
    gain_ref, wg_ref, wu_ref, wd_ref, gfin_ref, o_ref, a_ref = refs
    h = (_rms(x) * gain_ref[...]).astype(BF16)
    d_ff = wg_ref.shape[1]
    for f0 in range(0, d_ff, FF_TILE):
        g = jnp.dot(h, wg_ref[:, f0:f0 + FF_TILE], preferred_element_type=F32)
        u = jnp.dot(h, wu_ref[:, f0:f0 + FF_TILE], preferred_element_type=F32)
        a_ref[:, f0:f0 + FF_TILE] = (_silu(g) * u).astype(BF16)
    y = jnp.dot(a_ref[...], wd_ref[...], preferred_element_type=F32)
    out = x + 0.5 * y
    if final:
        out = _rms(out) * gfin_ref[...]
    o_ref[...] = out


def _ffn(x, gain, wg, wu, wd, gfin, *, final, mix=None):
    t, d = x.shape
    d_ff = wg.shape[1]
    tm = FFN_TILE
    row = lambda i: (i, 0)
    operands = [x]
    in_specs = [pl.BlockSpec((tm, d), row)]
    if mix is not None:
        attn, dn, wo_attn, wo_dn = mix
        operands += [attn, dn, wo_attn, wo_dn]
        in_specs += [pl.BlockSpec((tm, attn.shape[1]), row), pl.BlockSpec((tm, dn.shape[1]), row),
                     _resident(wo_attn.shape), _resident(wo_dn.shape)]
    operands += [gain, wg, wu, wd, gfin]
    in_specs += [_resident((1, d)), _resident((d, d_ff)), _resident((d, d_ff)),
                 _resident((d_ff, d)), _resident((1, d))]
    return pl.pallas_call(
        functools.partial(_ffn_kernel, mix=mix is not None, final=final),
        grid=(t // tm,),
        in_specs=in_specs,
        out_specs=pl.BlockSpec((tm, d), row),
        out_shape=jax.ShapeDtypeStruct((t, d), F32),
        scratch_shapes=[pltpu.VMEM((tm, d_ff), BF16)],
        compiler_params=_params("arbitrary"),
        name="ffn_mix" if mix is not None else "ffn",
    )(*operands)


def _inproj_kernel(x_ref, gain_ref, wm_ref, ws_ref, cw_ref,
                   aq_ref, ak_ref, av_ref, dq_ref, dk_ref, dv_ref, gate_ref, small_ref,
                   xc_ref, *, tiles_per_seq):
    i = pl.program_id(0)
    tm = x_ref.shape[0]
    halo = SUBLANES

    @pl.when(i % tiles_per_seq == 0)
    def _():
        xc_ref[0:halo, :] = jnp.zeros((halo, xc_ref.shape[1]), F32)

    @pl.when(i % tiles_per_seq != 0)
    def _():
        xc_ref[0:halo, :] = xc_ref[tm:tm + halo, :]

    h = (_rms(x_ref[...]) * gain_ref[...]).astype(BF16)

    def section(j, width=ATTN_WIDTH):
        return jnp.dot(h, wm_ref[:, j * width:(j + 1) * width], preferred_element_type=F32)

    for j in range(3):
        xc_ref[halo:halo + tm, j * DN_WIDTH:(j + 1) * DN_WIDTH] = section(3 + j)

    outs = (dq_ref, dk_ref, dv_ref)
    sub = lax.broadcasted_iota(jnp.int32, (halo, DN_HEAD_DIM), 0)
    for c in range(3 * DN_HEADS):
        sl = slice(c * DN_HEAD_DIM, (c + 1) * DN_HEAD_DIM)
        cur = xc_ref[halo:halo + tm, sl]
        tail = xc_ref[0:halo, sl]
        conv = None
        for j in range(CONV_WIDTH):
            back = CONV_WIDTH - 1 - j
            if back:
                shifted = pltpu.roll(cur, back, 0)
                head = jnp.where(sub < back, pltpu.roll(tail, back, 0), shifted[0:halo])
                shifted = jnp.concatenate([head, shifted[halo:]], axis=0)
            else:
                shifted = cur
            term = shifted * cw_ref[j:j + 1, sl]
            conv = term if conv is None else conv + term
        a = _silu(conv)
        if c < 2 * DN_HEADS:
            a = a * lax.rsqrt(jnp.sum(a * a, axis=-1, keepdims=True) + L2_EPS)
        if c < DN_HEADS:
            a = a * (DN_HEAD_DIM ** -0.5)
        hs = slice((c % DN_HEADS) * DN_HEAD_DIM, (c % DN_HEADS + 1) * DN_HEAD_DIM)
        outs[c // DN_HEADS][:, hs] = a

    aq_ref[...] = section(0) * (ATTN_HEAD_DIM ** -0.5)
    ak_ref[...] = section(1)
    av_ref[...] = section(2)
    gate_ref[...] = _silu(section(6))
    small_ref[...] = jnp.dot(h, ws_ref[...], preferred_element_type=F32)


def _inproj(x, gain, w_main, w_small, conv_w, *, seq):
    t, d = x.shape
    tm = TOKEN_TILE
    row = lambda i: (i, 0)
    wide = jax.ShapeDtypeStruct((t, ATTN_WIDTH), F32)
    return pl.pallas_call(
        functools.partial(_inproj_kernel, tiles_per_seq=seq // tm),
        grid=(t // tm,),
        in_specs=[
            pl.BlockSpec((tm, d), row),
            _resident((1, d)),
            _resident(w_main.shape),
            _resident(w_small.shape),
            _resident(conv_w.shape),
        ],
        out_specs=[pl.BlockSpec((tm, ATTN_WIDTH), row)] * 7 + [pl.BlockSpec((tm, LANES), row)],
        out_shape=[wide] * 7 + [jax.ShapeDtypeStruct((t, LANES), F32)],
        scratch_shapes=[pltpu.VMEM((tm + 2 * SUBLANES, 3 * DN_WIDTH), F32)],
        compiler_params=_params("arbitrary"),
        name="inproj",
    )(x, gain, w_main, w_small, conv_w)


def _rows(ref, start, size, stride):
    if stride == 1:
        return ref[pl.ds(start, size), :]
    return ref[pl.ds(start, size, stride=stride), :]


def _put_rows(ref, start, size, stride, value):
    if stride == 1:
        ref[pl.ds(start, size), :] = value
    else:
        ref[pl.ds(start, size, stride=stride), :] = value


def _attn_group(refs, stage, bias_ref, dilation, blocks, *, init):
    q_ref, k_ref, v_ref, acc_ref, m_refs, l_refs = refs
    s_ref, p_ref, alpha_ref = stage
    d = dilation
    blk = ATTN_BLOCK
    lane = lax.broadcasted_iota(jnp.int32, (blk, LANES), 1)
    head_lanes = [(lane >= h * ATTN_HEAD_DIM) & (lane < (h + 1) * ATTN_HEAD_DIM)
                  for h in range(HEADS_PER_SLAB)]
    starts = []
    for r, n in blocks:
        q_start = r + d * blk * n
        k_start = jnp.maximum(q_start - d * blk, r)
        if d == 1:
            q_start = pl.multiple_of(q_start, blk)
            k_start = pl.multiple_of(k_start, blk)
        starts.append((q_start, k_start, jnp.minimum(n, 1)))

    for i, (q_start, k_start, _) in enumerate(starts):
        q = _rows(q_ref, q_start, blk, d)
        kb = _rows(k_ref, k_start, 2 * blk, d).astype(BF16)
        for h in range(HEADS_PER_SLAB):
            qh = jnp.where(head_lanes[h], q, 0.0).astype(BF16)
            s_ref[i * HEADS_PER_SLAB + h] = lax.dot_general(
                qh, kb, (((1,), (1,)), ((), ())), preferred_element_type=F32)

    pending = []
    for i, (q_start, _, table) in enumerate(starts):
        for h in range(HEADS_PER_SLAB):
            t = i * HEADS_PER_SLAB + h
            s = s_ref[t] + bias_ref[table, h]
            m_new = jnp.broadcast_to(jnp.max(s, axis=-1, keepdims=True), (blk, LANES))
            if not init:
                m_prev = _rows(m_refs[h], q_start, blk, d)
                m_new = jnp.maximum(m_prev, m_new)
                alpha = jnp.exp(m_prev - m_new)
                alpha_ref[t] = alpha
            p = jnp.exp(s - jnp.concatenate([m_new, m_new], axis=1))
            l_new = jnp.broadcast_to(jnp.sum(p, axis=-1, keepdims=True), (blk, LANES))
            if not init:
                l_new = _rows(l_refs[h], q_start, blk, d) * alpha + l_new
            p_ref[t] = p.astype(BF16)
            pending.append((m_refs[h], q_start, m_new))
            pending.append((l_refs[h], q_start, l_new))

    for i, (q_start, k_start, _) in enumerate(starts):
        vb = _rows(v_ref, k_start, 2 * blk, d).astype(BF16)
        acc_prev = None if init else _rows(acc_ref, q_start, blk, d)
        acc_new = None
        for h in range(HEADS_PER_SLAB):
            t = i * HEADS_PER_SLAB + h
            pv = jnp.dot(p_ref[t], vb, preferred_element_type=F32)
            if not init:
                pv = acc_prev * alpha_ref[t] + pv
            acc_new = pv if acc_new is None else jnp.where(head_lanes[h], pv, acc_new)
        pending.append((acc_ref, q_start, acc_new))

    for ref, start, value in pending:
        _put_rows(ref, start, blk, d, value)


def _attn_kernel(q_ref, k_ref, v_ref, b1_ref, b4_ref, b16_ref, o_ref,
                 acc_ref, m0_ref, m1_ref, l0_ref, l1_ref, s_ref, p_ref, alpha_ref):
    seq = q_ref.shape[0]
    refs = (q_ref, k_ref, v_ref, acc_ref, (m0_ref, m1_ref), (l0_ref, l1_ref))
    stage = (s_ref, p_ref, alpha_ref)
    blk = ATTN_BLOCK
    grp = ATTN_GROUP
    patterns = sorted(zip(DILATED_PATTERNS, (b1_ref, b4_ref, b16_ref)), key=lambda e: -e[0][1])
    for idx, ((_, d), bias_ref) in enumerate(patterns):
        nb = seq // (d * blk)
        run = functools.partial(_attn_group, refs, stage, bias_ref, d, init=(idx == 0))
        if d < grp:
            per_res = nb // grp

            def body(i, carry, run=run, per_res=per_res):
                r = i // per_res
                n0 = (i % per_res) * grp
                run([(r, n0 + j) for j in range(grp)])
                return carry

            lax.fori_loop(0, d * per_res, body, 0)
        else:
            per_blk = d // grp

            def body(i, carry, run=run, per_blk=per_blk):
                n = i // per_blk
                r0 = (i % per_blk) * grp
                run([(r0 + j, n) for j in range(grp)])
                return carry

            lax.fori_loop(0, nb * per_blk, body, 0)

    lane = lax.broadcasted_iota(jnp.int32, (OUT_ROWS, LANES), 1)

    def out_body(i, carry):
        rows = pl.ds(pl.multiple_of(i * OUT_ROWS, OUT_ROWS), OUT_ROWS)
        denom = jnp.where(lane < ATTN_HEAD_DIM, l0_ref[rows, :], l1_ref[rows, :])
        o_ref[rows, :] = (acc_ref[rows, :] / denom).astype(o_ref.dtype)
        return carry

    lax.fori_loop(0, seq // OUT_ROWS, out_body, 0)


def _attn_bias(window, dilation):
    slopes = np.array([2.0 ** (-8.0 * (i + 1) / ATTN_HEADS) for i in range(ATTN_HEADS)],
                      dtype=np.float32)
    qi = np.arange(ATTN_BLOCK)[:, None]
    kj = np.arange(2 * ATTN_BLOCK)[None, :]
    tables = []
    for steps in (qi - kj, qi + ATTN_BLOCK - kj):
        valid = (steps >= 0) & (steps <= window // dilation)
        bias = -slopes[:, None, None] * (steps * dilation).astype(np.float32)
        tables.append(np.where(valid[None], bias, np.float32(MASK_VALUE)))
    table = np.stack(tables).astype(np.float32)
    table = table.reshape(2, ATTN_HEADS // HEADS_PER_SLAB, HEADS_PER_SLAB, *table.shape[2:])
    return jnp.asarray(table.transpose(1, 0, 2, 3, 4))


def _attention(q, k, v, *, batch, seq):
    slabs = ATTN_WIDTH // LANES
    per_seq = pl.BlockSpec((seq, LANES), lambda bi, j: (bi, j))
    bias = [_attn_bias(w, d) for w, d in DILATED_PATTERNS]
    bias_spec = pl.BlockSpec((None,) + bias[0].shape[1:], lambda bi, j: (j, 0, 0, 0, 0))
    state = pltpu.VMEM((seq, LANES), F32)
    tiles = ATTN_GROUP * HEADS_PER_SLAB
    return pl.pallas_call(
        _attn_kernel,
        grid=(batch, slabs),
        in_specs=[per_seq, per_seq, per_seq, bias_spec, bias_spec, bias_spec],
        out_specs=per_seq,
        out_shape=jax.ShapeDtypeStruct((batch * seq, ATTN_WIDTH), BF16),
        scratch_shapes=[state] * (1 + 2 * HEADS_PER_SLAB) + [
            pltpu.VMEM((tiles, ATTN_BLOCK, 2 * ATTN_BLOCK), F32),
            pltpu.VMEM((tiles, ATTN_BLOCK, 2 * ATTN_BLOCK), BF16),
            pltpu.VMEM((tiles, ATTN_BLOCK, LANES), F32),
        ],
        compiler_params=_params("arbitrary", "arbitrary"),
        name="attention",
    )(q, k, v, *bias)


def _bmm(a, b):
    return lax.dot_general(a.astype(BF16), b.astype(BF16), (((2,), (1,)), ((0,), (0,))),
                           preferred_element_type=F32)


def _bmm_nt(a, b):
    return lax.dot_general(a.astype(BF16), b.astype(BF16), (((2,), (2,)), ((0,), (0,))),
                           preferred_element_type=F32)


def _unit_lower_inverse(a_strict, row, col):
    c = a_strict.shape[-1]
    eye = (row == col).astype(F32)
    t = None
    size = 1
    while size < c:
        shift = size.bit_length() - 1
        join = ((row >> (shift + 1)) == (col >> (shift + 1))) & (((row >> shift) & 1) == 1) \
            & (((col >> shift) & 1) == 0)
        lk = jnp.where(join, a_strict, 0.0)
        if t is None:
            t = eye - lk
        else:
            t = t - _bmm(t, _bmm(lk, t))
        size *= 2
    return t


def _dn_kernel(q_ref, k_ref, v_ref, gate_ref, small_ref, arow_ref, dtrow_ref, nrm_ref,
               o_ref, s_ref, wq_ref, u_ref, at_ref, kd_ref):
    c = DN_CHUNK
    dh = DN_HEAD_DIM
    tc = q_ref.shape[0]
    g_chunks = tc // c
    nb = g_chunks * DN_HEADS

    @pl.when(pl.program_id(1) == 0)
    def _():
        s_ref[...] = jnp.zeros(s_ref.shape, F32)

    sm = small_ref[...]
    beta_all = jax.nn.sigmoid(sm)
    z = sm + dtrow_ref[...]
    softplus = jnp.maximum(z, 0.0) + jnp.log1p(jnp.exp(-jnp.abs(z)))
    g_all = -jnp.exp(arow_ref[...]) * softplus

    def per_head(ref):
        parts = [ref[:, hd * dh:(hd + 1) * dh].reshape(g_chunks, 1, c, dh) for hd in range(DN_HEADS)]
        return jnp.concatenate(parts, axis=1).reshape(nb, c, dh)

    def per_head_col(x, offset):
        parts = [x[:, offset + hd:offset + hd + 1].reshape(g_chunks, 1, c, 1)
                 for hd in range(DN_HEADS)]
        return jnp.concatenate(parts, axis=1).reshape(nb, c, 1)

    q = per_head(q_ref)
    k = per_head(k_ref)
    v = per_head(v_ref)
    beta = per_head_col(beta_all, 0)
    g = per_head_col(g_all, DN_HEADS)

    row = lax.broadcasted_iota(jnp.int32, (c, c), 0)
    col = lax.broadcasted_iota(jnp.int32, (c, c), 1)
    causal = row >= col
    strict = row > col

    g_mat = jnp.broadcast_to(g, (nb, c, c))
    g_row = jnp.sum(jnp.where(row == col, g_mat, 0.0), axis=1, keepdims=True)
    gc_col = jnp.sum(jnp.where(causal, g_row, 0.0), axis=2, keepdims=True)
    gc_row = jnp.sum(jnp.where(row <= col, g_mat, 0.0), axis=1, keepdims=True)
    decay = jnp.where(causal, jnp.exp(jnp.where(causal, gc_col - gc_row, 0.0)), 0.0)

    k_beta = k * beta
    e_gc = jnp.exp(gc_col)
    g_last = gc_col[:, c - 1:c, :]
    both = _bmm_nt(jnp.concatenate([k_beta, q], axis=1), k)
    a_mat = jnp.where(strict, both[:, :c] * decay, 0.0)
    at_ref[...] = jnp.where(causal, both[:, c:] * decay, 0.0).astype(BF16)
    rhs = jnp.concatenate([v * beta, k_beta * e_gc], axis=-1)
    sol = _bmm(_unit_lower_inverse(a_mat, row, col), rhs)
    u_ref[...] = sol[..., :dh]
    wq_ref[:, :c, :] = sol[..., dh:].astype(BF16)
    wq_ref[:, c:, :] = (q * e_gc).astype(BF16)
    kd_ref[...] = (k * jnp.exp(g_last - gc_col)).astype(BF16)
    e_last = jnp.exp(g_last)

    state = s_ref[...]
    outs = []
    for n in range(g_chunks):
        sl = slice(n * DN_HEADS, (n + 1) * DN_HEADS)
        ws = _bmm(wq_ref[sl], state)
        v_new = u_ref[sl] - ws[:, :c]
        outs.append(ws[:, c:] + _bmm(at_ref[sl], v_new))
        state = state * e_last[sl] + lax.dot_general(
            kd_ref[sl], v_new.astype(BF16), (((1,), (1,)), ((0,), (0,))),
            preferred_element_type=F32)
    s_ref[...] = state

    for hd in range(DN_HEADS):
        sl = slice(hd * dh, (hd + 1) * dh)
        o = jnp.concatenate([outs[n][hd] for n in range(g_chunks)], axis=0)
        o = o * lax.rsqrt(jnp.mean(o * o, axis=-1, keepdims=True) + NORM_EPS)
        o_ref[:, sl] = (o * nrm_ref[...] * gate_ref[:, sl]).astype(o_ref.dtype)


def _deltanet(dq, dk, dv, gate, small, arow, dtrow, nrm, *, batch, seq):
    tc = DN_TOKENS
    per_seq = seq // tc
    nb = (tc // DN_CHUNK) * DN_HEADS
    row = lambda bi, j: (bi * per_seq + j, 0)
    wide = pl.BlockSpec((tc, DN_WIDTH), row)
    return pl.pallas_call(
        _dn_kernel,
        grid=(batch, per_seq),
        in_specs=[wide, wide, wide, wide, pl.BlockSpec((tc, LANES), row),
                  _resident((1, LANES)), _resident((1, LANES)), _resident((1, DN_HEAD_DIM))],
        out_specs=wide,
        out_shape=jax.ShapeDtypeStruct((batch * seq, DN_WIDTH), BF16),
        scratch_shapes=[
            pltpu.VMEM((DN_HEADS, DN_HEAD_DIM, DN_HEAD_DIM), F32),
            pltpu.VMEM((nb, 2 * DN_CHUNK, DN_HEAD_DIM), BF16),
            pltpu.VMEM((nb, DN_CHUNK, DN_HEAD_DIM), F32),
            pltpu.VMEM((nb, DN_CHUNK, DN_CHUNK), BF16),
            pltpu.VMEM((nb, DN_CHUNK, DN_HEAD_DIM), BF16),
        ],
        compiler_params=_params("arbitrary", "arbitrary"),
        name="deltanet",
    )(dq, dk, dv, gate, small, arow, dtrow, nrm)


def _lane_row(values, offset):
    return jnp.zeros((1, LANES), F32).at[0, offset:offset + values.shape[0]].set(values.astype(F32))


def kernel(x, norm_ffn1, ffn1_gate, ffn1_up, ffn1_down, norm_mix, w_in, conv_w, a_log, dt_bias,
           dn_norm, w_out, norm_ffn2, ffn2_gate, ffn2_up, ffn2_down, norm_final):
    batch, seq, d = x.shape
    depth = norm_ffn1.shape[0]
    assert seq % (max(dl for _, dl in DILATED_PATTERNS) * ATTN_BLOCK) == 0
    assert (batch * seq) % FFN_TILE == 0
    assert seq % TOKEN_TILE == 0 and seq % DN_TOKENS == 0 and seq % OUT_ROWS == 0
    xt = x.reshape(batch * seq, d)
    gfin = norm_final.reshape(1, d).astype(F32)
    small_lo = 6 * ATTN_WIDTH
    small_hi = small_lo + 2 * DN_HEADS

    for i in range(depth):
        row = lambda g: g.reshape(1, -1).astype(F32)
        xt = _ffn(xt, row(norm_ffn1[i]), ffn1_gate[i].astype(BF16), ffn1_up[i].astype(BF16),
                  ffn1_down[i].astype(BF16), gfin, final=False)

        wi = w_in[i]
        w_main = jnp.concatenate([wi[:, :small_lo], wi[:, small_hi:]], axis=1).astype(BF16)
        w_small = jnp.pad(wi[:, small_lo:small_hi], ((0, 0), (0, LANES - 2 * DN_HEADS))).astype(BF16)
        aq, ak, av, dq, dk, dv, gate, small = _inproj(
            xt, row(norm_mix[i]), w_main, w_small, conv_w[i].astype(F32), seq=seq)

        attn = _attention(aq, ak, av, batch=batch, seq=seq)
        dn = _deltanet(dq, dk, dv, gate, small, _lane_row(a_log[i], DN_HEADS),
                       _lane_row(dt_bias[i], DN_HEADS), row(dn_norm[i]), batch=batch, seq=seq)

        wo = w_out[i].astype(BF16)
        xt = _ffn(xt, row(norm_ffn2[i]), ffn2_gate[i].astype(BF16), ffn2_up[i].astype(BF16),
                  ffn2_down[i].astype(BF16), gfin, final=(i == depth - 1),
                  mix=(attn, dn, wo[:ATTN_WIDTH], wo[ATTN_WIDTH:]))

    return xt.reshape(batch, seq, d)
```

```python
import functools

import numpy as np
import jax
import jax.numpy as jnp
from jax import lax
from jax.experimental import pallas as pl
from jax.experimental.pallas import tpu as pltpu

F32 = jnp.float32
BF16 = jnp.bfloat16

ATTN_HEADS = 8
ATTN_HEAD_DIM = 64
ATTN_WIDTH = ATTN_HEADS * ATTN_HEAD_DIM
DILATED_PATTERNS = ((128, 1), (512, 4), (2048, 16))
ATTN_BLOCK = 128
DN_HEADS = 4
DN_HEAD_DIM = 128
DN_WIDTH = DN_HEADS * DN_HEAD_DIM
DN_CHUNK = 64
CONV_WIDTH = 4
NORM_EPS = 1e-6
L2_EPS = 1e-6

LANES = 128
SUBLANES = 8
VMEM_LIMIT_BYTES = 56 * 1024 * 1024

MASK_VALUE = -1e30
LOG2_E = 1.4426950408889634

TOKEN_TILE = 512
FFN_TILE = 1024
FF_TILE = 256
DN_TOKENS = 512
HEADS_PER_SLAB = LANES // ATTN_HEAD_DIM
OUT_ROWS = 512
ATTN_GROUP = 8
DEINTERLEAVE_ROWS = 256


def _rms(x):
    return x * lax.rsqrt(jnp.mean(x * x, axis=-1, keepdims=True) + NORM_EPS)


def _silu(x):
    half = 0.5 * x
    return half + half * jnp.tanh(half)


def _mm(a, b):
    return jnp.dot(a.astype(BF16), b.astype(BF16), preferred_element_type=F32)


def _resident(shape):
    nd = len(shape)
    return pl.BlockSpec(shape, lambda *_: (0,) * nd, pipeline_mode=pl.Buffered(1))


def _params(*sem):
    return pltpu.CompilerParams(dimension_semantics=sem, vmem_limit_bytes=VMEM_LIMIT_BYTES)


def _ffn_kernel(*refs, mix, final):
    if mix:
        x_ref, attn_ref, dn_ref, woa_ref, wod_ref = refs[:5]
        refs = refs[5:]
        x = x_ref[...] + jnp.dot(attn_ref[...], woa_ref[...], preferred_element_type=F32)
        x = x + jnp.dot(dn_ref[...], wod_ref[...], preferred_element_type=F32)
    else:
        x_ref = refs[0]
        refs = refs[1:]
        x = x_ref[...]
    gain_ref, wg_ref, wu_ref, wd_ref, gfin_ref, o_ref, a_ref = refs
    h = (_rms(x) * gain_ref[...]).astype(BF16)
    d_ff = wg_ref.shape[1]
    for f0 in range(0, d_ff, FF_TILE):
        g = jnp.dot(h, wg_ref[:, f0:f0 + FF_TILE], preferred_element_type=F32)
        u = jnp.dot(h, wu_ref[:, f0:f0 + FF_TILE], preferred_element_type=F32)
        a_ref[:, f0:f0 + FF_TILE] = (_silu(g) * u).astype(BF16)
    y = jnp.dot(a_ref[...], wd_ref[...], preferred_element_type=F32)
    out = x + 0.5 * y
    if final:
        out = _rms(out) * gfin_ref[...]
    o_ref[...] = out


def _ffn(x, gain, wg, wu, wd, gfin, *, final, mix=None):
    t, d = x.shape
    d_ff = wg.shape[1]
    tm = FFN_TILE
    row = lambda i: (i, 0)
    operands = [x]
    in_specs = [pl.BlockSpec((tm, d), row)]
    if mix is not None:
        attn, dn, wo_attn, wo_dn = mix
        operands += [attn, dn, wo_attn, wo_dn]
        in_specs += [pl.BlockSpec((tm, attn.shape[1]), row), pl.BlockSpec((tm, dn.shape[1]), row),
                     _resident(wo_attn.shape), _resident(wo_dn.shape)]
    operands += [gain, wg, wu, wd, gfin]
    in_specs += [_resident((1, d)), _resident((d, d_ff)), _resident((d, d_ff)),
                 _resident((d_ff, d)), _resident((1, d))]
    return pl.pallas_call(
        functools.partial(_ffn_kernel, mix=mix is not None, final=final),
        grid=(t // tm,),
        in_specs=in_specs,
        out_specs=pl.BlockSpec((tm, d), row),
        out_shape=jax.ShapeDtypeStruct((t, d), F32),
        scratch_shapes=[pltpu.VMEM((tm, d_ff), BF16)],
        compiler_params=_params("arbitrary"),
        name="ffn_mix" if mix is not None else "ffn",
    )(*operands)


def _inproj_kernel(x_ref, gain_ref, wm_ref, ws_ref, cw_ref,
                   aq_ref, ak_ref, av_ref, dq_ref, dk_ref, dv_ref, gate_ref, small_ref,
                   xc_ref, *, tiles_per_seq):
    i = pl.program_id(0)
    tm = x_ref.shape[0]
    halo = SUBLANES

    @pl.when(i % tiles_per_seq == 0)
    def _():
        xc_ref[0:halo, :] = jnp.zeros((halo, xc_ref.shape[1]), F32)

    @pl.when(i % tiles_per_seq != 0)
    def _():
        xc_ref[0:halo, :] = xc_ref[tm:tm + halo, :]

    h = (_rms(x_ref[...]) * gain_ref[...]).astype(BF16)

    def section(j, width=ATTN_WIDTH):
        return jnp.dot(h, wm_ref[:, j * width:(j + 1) * width], preferred_element_type=F32)

    for j in range(3):
        xc_ref[halo:halo + tm, j * DN_WIDTH:(j + 1) * DN_WIDTH] = section(3 + j)

    outs = (dq_ref, dk_ref, dv_ref)
    sub = lax.broadcasted_iota(jnp.int32, (halo, DN_HEAD_DIM), 0)
    for c in range(3 * DN_HEADS):
        sl = slice(c * DN_HEAD_DIM, (c + 1) * DN_HEAD_DIM)
        cur = xc_ref[halo:halo + tm, sl]
        tail = xc_ref[0:halo, sl]
        conv = None
        for j in range(CONV_WIDTH):
            back = CONV_WIDTH - 1 - j
            if back:
                shifted = pltpu.roll(cur, back, 0)
                head = jnp.where(sub < back, pltpu.roll(tail, back, 0), shifted[0:halo])
                shifted = jnp.concatenate([head, shifted[halo:]], axis=0)
            else:
                shifted = cur
            term = shifted * cw_ref[j:j + 1, sl]
            conv = term if conv is None else conv + term
        a = _silu(conv)
        if c < 2 * DN_HEADS:
            a = a * lax.rsqrt(jnp.sum(a * a, axis=-1, keepdims=True) + L2_EPS)
        if c < DN_HEADS:
            a = a * (DN_HEAD_DIM ** -0.5)
        hs = slice((c % DN_HEADS) * DN_HEAD_DIM, (c % DN_HEADS + 1) * DN_HEAD_DIM)
        outs[c // DN_HEADS][:, hs] = a

    aq_ref[...] = section(0) * (ATTN_HEAD_DIM ** -0.5 * LOG2_E)
    ak_ref[...] = section(1)
    av_ref[...] = section(2)
    gate_ref[...] = _silu(section(6))
    small_ref[...] = jnp.dot(h, ws_ref[...], preferred_element_type=F32)


def _inproj(x, gain, w_main, w_small, conv_w, *, seq):
    t, d = x.shape
    tm = TOKEN_TILE
    row = lambda i: (i, 0)
    wide = jax.ShapeDtypeStruct((t, ATTN_WIDTH), F32)
    return pl.pallas_call(
        functools.partial(_inproj_kernel, tiles_per_seq=seq // tm),
        grid=(t // tm,),
        in_specs=[
            pl.BlockSpec((tm, d), row),
            _resident((1, d)),
            _resident(w_main.shape),
            _resident(w_small.shape),
            _resident(conv_w.shape),
        ],
        out_specs=[pl.BlockSpec((tm, ATTN_WIDTH), row)] * 7 + [pl.BlockSpec((tm, LANES), row)],
        out_shape=[wide] * 7 + [jax.ShapeDtypeStruct((t, LANES), F32)],
        scratch_shapes=[pltpu.VMEM((tm + 2 * SUBLANES, 3 * DN_WIDTH), F32)],
        compiler_params=_params("arbitrary"),
        name="inproj",
    )(x, gain, w_main, w_small, conv_w)


FINE = 8
COARSE = 16


def _chunks(dilation):
    return FINE // dilation


def _tile_order(dilation, size):
    c = _chunks(dilation) if dilation < FINE else 1
    per = size // c
    x = np.arange(size)
    return c * (x % per) + x // per


def _attn_group(refs, stage, bias_ref, dilation, blocks, *, init):
    fine, coarse, acc_ref, m_ref, l_ref = refs
    s_ref, p_ref, alpha_ref = stage
    d = dilation
    blk = ATTN_BLOCK
    seq = acc_ref.shape[0]
    lane = lax.broadcasted_iota(jnp.int32, (blk, LANES), 1)
    head_lanes = [(lane >= h * ATTN_HEAD_DIM) & (lane < (h + 1) * ATTN_HEAD_DIM)
                  for h in range(HEADS_PER_SLAB)]

    def gather(ref, starts, rows):
        parts = [ref[pl.ds(pl.multiple_of(st, 16), rows), :] for st in starts]
        return parts[0] if len(parts) == 1 else jnp.concatenate(parts, axis=0)

    plans = []
    for r, n in blocks:
        n_key = jnp.maximum(n - 1, 0)
        if d < FINE:
            c = _chunks(d)
            per = seq // FINE
            bases = [(r + d * b) * per for b in range(c)]
            q_starts = [base + (blk // c) * n for base in bases]
            k_starts = [base + (blk // c) * n_key for base in bases]
            q_refs, k_ref, v_ref = fine
            state = [(st, blk // c, 1) for st in q_starts]
        else:
            c = 1
            per = seq // COARSE
            q_starts = [r * per + blk * n]
            k_starts = [r * per + blk * n_key]
            q_refs, k_ref, v_ref = coarse
            ratio = d // FINE
            state = [((r % FINE) * (seq // FINE) + ratio * blk * n + r // FINE, blk, ratio)]
        plans.append((q_refs, k_ref, v_ref, q_starts, k_starts, c, state, jnp.minimum(n, 1)))

    def state_rows(ref, state):
        parts = []
        for start, rows, stride in state:
            if stride == 1:
                parts.append(ref[pl.ds(pl.multiple_of(start, 16), rows), :])
            else:
                parts.append(ref[pl.ds(start, rows, stride=stride), :])
        return parts[0] if len(parts) == 1 else jnp.concatenate(parts, axis=0)

    for i, (q_refs, k_ref, _, q_starts, k_starts, c, _, _) in enumerate(plans):
        kb = gather(k_ref, k_starts, 2 * blk // c)
        for h in range(HEADS_PER_SLAB):
            qh = gather(q_refs[h], q_starts, blk // c)
            s_ref[i * HEADS_PER_SLAB + h] = lax.dot_general(
                qh, kb, (((1,), (1,)), ((), ())), preferred_element_type=F32)

    pending = []
    for i, plan in enumerate(plans):
        state, table = plan[6], plan[7]
        m_prev = None if init else state_rows(m_ref, state)
        m_heads, l_heads = [], []
        for h in range(HEADS_PER_SLAB):
            t = i * HEADS_PER_SLAB + h
            s = s_ref[t] + bias_ref[table, h]
            top = jnp.maximum(s[:, :LANES], s[:, LANES:])
            if not init:
                top = jnp.maximum(top, jnp.where(head_lanes[h], m_prev, MASK_VALUE))
            m_new = jnp.broadcast_to(jnp.max(top, axis=-1, keepdims=True), (blk, LANES))
            p = jnp.exp2(s - jnp.concatenate([m_new, m_new], axis=1))
            l_heads.append(jnp.broadcast_to(jnp.sum(p, axis=-1, keepdims=True), (blk, LANES)))
            m_heads.append(m_new)
            p_ref[t] = p.astype(BF16)
        m_new = jnp.where(head_lanes[0], m_heads[0], m_heads[1])
        l_new = jnp.where(head_lanes[0], l_heads[0], l_heads[1])
        if not init:
            alpha = jnp.exp2(m_prev - m_new)
            alpha_ref[i] = alpha
            l_new = state_rows(l_ref, state) * alpha + l_new
        pending.append((m_ref, state, m_new))
        pending.append((l_ref, state, l_new))

    for i, plan in enumerate(plans):
        v_ref, k_starts, c, state = plan[2], plan[4], plan[5], plan[6]
        vb = gather(v_ref, k_starts, 2 * blk // c)
        pv = [jnp.dot(p_ref[i * HEADS_PER_SLAB + h], vb, preferred_element_type=F32)
              for h in range(HEADS_PER_SLAB)]
        acc_new = jnp.where(head_lanes[0], pv[0], pv[1])
        if not init:
            acc_new = state_rows(acc_ref, state) * alpha_ref[i] + acc_new
        pending.append((acc_ref, state, acc_new))

    for ref, state, value in pending:
        offset = 0
        for start, rows, stride in state:
            piece = value[offset:offset + rows]
            if stride == 1:
                ref[pl.ds(pl.multiple_of(start, 16), rows), :] = piece
            else:
                ref[pl.ds(start, rows, stride=stride), :] = piece
            offset += rows


def _deinterleave(src_ref, tmp_ref, dst_fine, dst_coarse, *, masks):
    seq = src_ref.shape[0]
    quarter = seq // 4
    piece = DEINTERLEAVE_ROWS

    for r4 in range(4):
        for off in range(0, quarter, piece):
            tmp_ref[r4 * quarter + off:r4 * quarter + off + piece, :] = \
                src_ref[pl.ds(r4 + 4 * off, piece, stride=4), :]

    for layout, dsts in ((FINE, dst_fine), (COARSE, dst_coarse)):
        per = seq // layout
        sub = layout // 4
        for res in range(layout):
            for off in range(0, per, piece):
                value = tmp_ref[pl.ds((res % 4) * quarter + res // 4 + sub * off, piece,
                                      stride=sub), :]
                rows = slice(res * per + off, res * per + off + piece)
                if masks is None:
                    dsts[0][rows, :] = value.astype(BF16)
                else:
                    for dst, mask in zip(dsts, masks):
                        dst[rows, :] = jnp.where(mask, value, 0.0).astype(BF16)


def _attn_kernel(q_ref, k_ref, v_ref, b1_ref, b4_ref, b16_ref, o_ref,
                 qf0, qf1, kf, vf, qc0, qc1, kc, vc, tmp_ref, acc_ref, m_ref, l_ref,
                 s_ref, p_ref, alpha_ref):
    seq = q_ref.shape[0]
    blk = ATTN_BLOCK
    grp = ATTN_GROUP
    lane = lax.broadcasted_iota(jnp.int32, (DEINTERLEAVE_ROWS, LANES), 1)
    masks = [(lane >= h * ATTN_HEAD_DIM) & (lane < (h + 1) * ATTN_HEAD_DIM)
             for h in range(HEADS_PER_SLAB)]
    _deinterleave(q_ref, tmp_ref, (qf0, qf1), (qc0, qc1), masks=masks)
    _deinterleave(k_ref, tmp_ref, (kf,), (kc,), masks=None)
    _deinterleave(v_ref, tmp_ref, (vf,), (vc,), masks=None)

    refs = (((qf0, qf1), kf, vf), ((qc0, qc1), kc, vc), acc_ref, m_ref, l_ref)
    stage = (s_ref, p_ref, alpha_ref)
    patterns = sorted(zip(DILATED_PATTERNS, (b1_ref, b4_ref, b16_ref)), key=lambda e: -e[0][1])
    for idx, ((_, d), bias_ref) in enumerate(patterns):
        nb = seq // (d * blk)
        run = functools.partial(_attn_group, refs, stage, bias_ref, d, init=(idx == 0))
        if d < grp:
            per_res = nb // grp

            def body(i, carry, run=run, per_res=per_res):
                r = i // per_res
                n0 = (i % per_res) * grp
                run([(r, n0 + j) for j in range(grp)])
                return carry

            lax.fori_loop(0, d * per_res, body, 0)
        else:
            per_blk = d // grp

            def body(i, carry, run=run, per_blk=per_blk):
                n = i // per_blk
                r0 = (i % per_blk) * grp
                run([(r0 + j, n) for j in range(grp)])
                return carry

            lax.fori_loop(0, nb * per_blk, body, 0)

    per = seq // FINE

    def out_body(i, carry):
        res = i // (per // OUT_ROWS)
        off = (i % (per // OUT_ROWS)) * OUT_ROWS
        rows = pl.ds(pl.multiple_of(res * per + off, OUT_ROWS), OUT_ROWS)
        tmp_ref[pl.ds(res + FINE * off, OUT_ROWS, stride=FINE), :] = acc_ref[rows, :] / l_ref[rows, :]
        return carry

    lax.fori_loop(0, FINE * (per // OUT_ROWS), out_body, 0)

    def cast_body(i, carry):
        rows = pl.ds(pl.multiple_of(i * OUT_ROWS, OUT_ROWS), OUT_ROWS)
        o_ref[rows, :] = tmp_ref[rows, :].astype(o_ref.dtype)
        return carry

    lax.fori_loop(0, seq // OUT_ROWS, cast_body, 0)


def _attn_bias(window, dilation):
    slopes = np.array([2.0 ** (-8.0 * (i + 1) / ATTN_HEADS) for i in range(ATTN_HEADS)],
                      dtype=np.float32)
    qi = _tile_order(dilation, ATTN_BLOCK)[:, None]
    kj = _tile_order(dilation, 2 * ATTN_BLOCK)[None, :]
    tables = []
    for steps in (qi - kj, qi + ATTN_BLOCK - kj):
        valid = (steps >= 0) & (steps <= window // dilation)
        bias = -slopes[:, None, None] * (steps * dilation).astype(np.float32) * np.float32(LOG2_E)
        tables.append(np.where(valid[None], bias, np.float32(MASK_VALUE)))
    table = np.stack(tables).astype(np.float32)
    table = table.reshape(2, ATTN_HEADS // HEADS_PER_SLAB, HEADS_PER_SLAB, *table.shape[2:])
    return jnp.asarray(table.transpose(1, 0, 2, 3, 4))


def _attention(q, k, v, *, batch, seq):
    slabs = ATTN_WIDTH // LANES
    per_seq = pl.BlockSpec((seq, LANES), lambda bi, j: (bi, j))
    bias = [_attn_bias(w, d) for w, d in DILATED_PATTERNS]
    bias_spec = pl.BlockSpec((None,) + bias[0].shape[1:], lambda bi, j: (j, 0, 0, 0, 0))
    rows16 = pltpu.VMEM((seq, LANES), BF16)
    rows32 = pltpu.VMEM((seq, LANES), F32)
    tiles = ATTN_GROUP * HEADS_PER_SLAB
    return pl.pallas_call(
        _attn_kernel,
        grid=(batch, slabs),
        in_specs=[per_seq, per_seq, per_seq, bias_spec, bias_spec, bias_spec],
        out_specs=per_seq,
        out_shape=jax.ShapeDtypeStruct((batch * seq, ATTN_WIDTH), BF16),
        scratch_shapes=[rows16] * 8 + [rows32] * 4 + [
            pltpu.VMEM((tiles, ATTN_BLOCK, 2 * ATTN_BLOCK), F32),
            pltpu.VMEM((tiles, ATTN_BLOCK, 2 * ATTN_BLOCK), BF16),
            pltpu.VMEM((ATTN_GROUP, ATTN_BLOCK, LANES), F32),
        ],
        compiler_params=_params("arbitrary", "arbitrary"),
        name="attention",
    )(q, k, v, *bias)


def _bmm(a, b):
    return lax.dot_general(a.astype(BF16), b.astype(BF16), (((2,), (1,)), ((0,), (0,))),
                           preferred_element_type=F32)


def _bmm_nt(a, b):
    return lax.dot_general(a.astype(BF16), b.astype(BF16), (((2,), (2,)), ((0,), (0,))),
                           preferred_element_type=F32)


def _unit_lower_inverse(a_strict, row, col):
    c = a_strict.shape[-1]
    eye = (row == col).astype(F32)
    t = None
    size = 1
    while size < c:
        shift = size.bit_length() - 1
        join = ((row >> (shift + 1)) == (col >> (shift + 1))) & (((row >> shift) & 1) == 1) \
            & (((col >> shift) & 1) == 0)
        lk = jnp.where(join, a_strict, 0.0)
        if t is None:
            t = eye - lk
        else:
            t = t - _bmm(t, _bmm(lk, t))
        size *= 2
    return t


def _dn_kernel(q_ref, k_ref, v_ref, gate_ref, small_ref, arow_ref, dtrow_ref, nrm_ref,
               o_ref, s_ref, wq_ref, u_ref, at_ref, kd_ref):
    c = DN_CHUNK
    dh = DN_HEAD_DIM
    tc = q_ref.shape[0]
    g_chunks = tc // c
    nb = g_chunks * DN_HEADS

    @pl.when(pl.program_id(1) == 0)
    def _():
        s_ref[...] = jnp.zeros(s_ref.shape, F32)

    sm = small_ref[...]
    beta_all = jax.nn.sigmoid(sm)
    z = sm + dtrow_ref[...]
    softplus = jnp.maximum(z, 0.0) + jnp.log1p(jnp.exp(-jnp.abs(z)))
    g_all = -jnp.exp(arow_ref[...]) * softplus

    def per_head(ref):
        parts = [ref[:, hd * dh:(hd + 1) * dh].reshape(g_chunks, 1, c, dh) for hd in range(DN_HEADS)]
        return jnp.concatenate(parts, axis=1).reshape(nb, c, dh)

    def per_head_col(x, offset):
        parts = [x[:, offset + hd:offset + hd + 1].reshape(g_chunks, 1, c, 1)
                 for hd in range(DN_HEADS)]
        return jnp.concatenate(parts, axis=1).reshape(nb, c, 1)

    q = per_head(q_ref)
    k = per_head(k_ref)
    v = per_head(v_ref)
    beta = per_head_col(beta_all, 0)
    g = per_head_col(g_all, DN_HEADS)

    row = lax.broadcasted_iota(jnp.int32, (c, c), 0)
    col = lax.broadcasted_iota(jnp.int32, (c, c), 1)
    causal = row >= col
    strict = row > col

    g_mat = jnp.broadcast_to(g, (nb, c, c))
    g_row = jnp.sum(jnp.where(row == col, g_mat, 0.0), axis=1, keepdims=True)
    gc_col = jnp.sum(jnp.where(causal, g_row, 0.0), axis=2, keepdims=True)
    gc_row = jnp.sum(jnp.where(row <= col, g_mat, 0.0), axis=1, keepdims=True)
    decay = jnp.where(causal, jnp.exp(jnp.where(causal, gc_col - gc_row, 0.0)), 0.0)

    k_beta = k * beta
    e_gc = jnp.exp(gc_col)
    g_last = gc_col[:, c - 1:c, :]
    both = _bmm_nt(jnp.concatenate([k_beta, q], axis=1), k)
    a_mat = jnp.where(strict, both[:, :c] * decay, 0.0)
    at_ref[...] = jnp.where(causal, both[:, c:] * decay, 0.0).astype(BF16)
    rhs = jnp.concatenate([v * beta, k_beta * e_gc], axis=-1)
    sol = _bmm(_unit_lower_inverse(a_mat, row, col), rhs)
    u_ref[...] = sol[..., :dh]
    wq_ref[:, :c, :] = sol[..., dh:].astype(BF16)
    wq_ref[:, c:, :] = (q * e_gc).astype(BF16)
    kd_ref[...] = (k * jnp.exp(g_last - gc_col)).astype(BF16)
    e_last = jnp.exp(g_last)

    state = s_ref[...]
    outs = []
    for n in range(g_chunks):
        sl = slice(n * DN_HEADS, (n + 1) * DN_HEADS)
        ws = _bmm(wq_ref[sl], state)
        v_new = u_ref[sl] - ws[:, :c]
        outs.append(ws[:, c:] + _bmm(at_ref[sl], v_new))
        state = state * e_last[sl] + lax.dot_general(
            kd_ref[sl], v_new.astype(BF16), (((1,), (1,)), ((0,), (0,))),
            preferred_element_type=F32)
    s_ref[...] = state

    for hd in range(DN_HEADS):
        sl = slice(hd * dh, (hd + 1) * dh)
        o = jnp.concatenate([outs[n][hd] for n in range(g_chunks)], axis=0)
        o = o * lax.rsqrt(jnp.mean(o * o, axis=-1, keepdims=True) + NORM_EPS)
        o_ref[:, sl] = (o * nrm_ref[...] * gate_ref[:, sl]).astype(o_ref.dtype)


def _deltanet(dq, dk, dv, gate, small, arow, dtrow, nrm, *, batch, seq):
    tc = DN_TOKENS
    per_seq = seq // tc
    nb = (tc // DN_CHUNK) * DN_HEADS
    row = lambda bi, j: (bi * per_seq + j, 0)
    wide = pl.BlockSpec((tc, DN_WIDTH), row)
    return pl.pallas_call(
        _dn_kernel,
        grid=(batch, per_seq),
        in_specs=[wide, wide, wide, wide, pl.BlockSpec((tc, LANES), row),
                  _resident((1, LANES)), _resident((1, LANES)), _resident((1, DN_HEAD_DIM))],
        out_specs=wide,
        out_shape=jax.ShapeDtypeStruct((batch * seq, DN_WIDTH), BF16),
        scratch_shapes=[
            pltpu.VMEM((DN_HEADS, DN_HEAD_DIM, DN_HEAD_DIM), F32),
            pltpu.VMEM((nb, 2 * DN_CHUNK, DN_HEAD_DIM), BF16),
            pltpu.VMEM((nb, DN_CHUNK, DN_HEAD_DIM), F32),
            pltpu.VMEM((nb, DN_CHUNK, DN_CHUNK), BF16),
            pltpu.VMEM((nb, DN_CHUNK, DN_HEAD_DIM), BF16),
        ],
        compiler_params=_params("arbitrary", "arbitrary"),
        name="deltanet",
    )(dq, dk, dv, gate, small, arow, dtrow, nrm)


def _lane_row(values, offset):
    return jnp.zeros((1, LANES), F32).at[0, offset:offset + values.shape[0]].set(values.astype(F32))


def kernel(x, norm_ffn1, ffn1_gate, ffn1_up, ffn1_down, norm_mix, w_in, conv_w, a_log, dt_bias,
           dn_norm, w_out, norm_ffn2, ffn2_gate, ffn2_up, ffn2_down, norm_final):
    batch, seq, d = x.shape
    depth = norm_ffn1.shape[0]
    assert seq % (max(dl for _, dl in DILATED_PATTERNS) * ATTN_BLOCK) == 0
    assert (batch * seq) % FFN_TILE == 0
    assert seq % TOKEN_TILE == 0 and seq % DN_TOKENS == 0 and seq % OUT_ROWS == 0
    xt = x.reshape(batch * seq, d)
    gfin = norm_final.reshape(1, d).astype(F32)
    small_lo = 6 * ATTN_WIDTH
    small_hi = small_lo + 2 * DN_HEADS

    for i in range(depth):
        row = lambda g: g.reshape(1, -1).astype(F32)
        xt = _ffn(xt, row(norm_ffn1[i]), ffn1_gate[i].astype(BF16), ffn1_up[i].astype(BF16),
                  ffn1_down[i].astype(BF16), gfin, final=False)

        wi = w_in[i]
        w_main = jnp.concatenate([wi[:, :small_lo], wi[:, small_hi:]], axis=1).astype(BF16)
        w_small = jnp.pad(wi[:, small_lo:small_hi], ((0, 0), (0, LANES - 2 * DN_HEADS))).astype(BF16)
        aq, ak, av, dq, dk, dv, gate, small = _inproj(
            xt, row(norm_mix[i]), w_main, w_small, conv_w[i].astype(F32), seq=seq)

        attn = _attention(aq, ak, av, batch=batch, seq=seq)
        dn = _deltanet(dq, dk, dv, gate, small, _lane_row(a_log[i], DN_HEADS),
                       _lane_row(dt_bias[i], DN_HEADS), row(dn_norm[i]), batch=batch, seq=seq)

        wo = w_out[i].astype(BF16)
        xt = _ffn(xt, row(norm_ffn2[i]), ffn2_gate[i].astype(BF16), ffn2_up[i].astype(BF16),
                  ffn2_down[i].astype(BF16), gfin, final=(i == depth - 1),
                  mix=(attn, dn, wo[:ATTN_WIDTH], wo[ATTN_WIDTH:]))

    return xt.reshape(batch, seq, d)
```

```python
import functools

import numpy as np
import jax
import jax.numpy as jnp
from jax import lax
from jax.experimental import pallas as pl
from jax.experimental.pallas import tpu as pltpu

F32 = jnp.float32
BF16 = jnp.bfloat16

ATTN_HEADS = 8
ATTN_HEAD_DIM = 64
ATTN_WIDTH = ATTN_HEADS * ATTN_HEAD_DIM
DILATED_PATTERNS = ((128, 1), (512, 4), (2048, 16))
ATTN_BLOCK = 128
DN_HEADS = 4
DN_HEAD_DIM = 128
DN_WIDTH = DN_HEADS * DN_HEAD_DIM
DN_CHUNK = 64
CONV_WIDTH = 4
NORM_EPS = 1e-6
L2_EPS = 1e-6

LANES = 128
SUBLANES = 8
VMEM_LIMIT_BYTES = 56 * 1024 * 1024

MASK_VALUE = -1e30
LOG2_E = 1.4426950408889634

TOKEN_TILE = 512
FFN_TILE = 1024
FF_TILE = 256
DN_TOKENS = 128
HEADS_PER_SLAB = LANES // ATTN_HEAD_DIM
OUT_ROWS = 512
ATTN_GROUP = 8
DEINTERLEAVE_ROWS = 256


def _rms(x):
    return x * lax.rsqrt(jnp.mean(x * x, axis=-1, keepdims=True) + NORM_EPS)


def _silu(x):
    half = 0.5 * x
    return half + half * jnp.tanh(half)


def _mm(a, b):
    return jnp.dot(a.astype(BF16), b.astype(BF16), preferred_element_type=F32)


def _resident(shape):
    nd = len(shape)
    return pl.BlockSpec(shape, lambda *_: (0,) * nd, pipeline_mode=pl.Buffered(1))


def _params(*sem):
    return pltpu.CompilerParams(dimension_semantics=sem, vmem_limit_bytes=VMEM_LIMIT_BYTES)


def _swiglu_step(x, gain_ref, wg_ref, wu_ref, wd_ref, a_ref):
    h = (_rms(x) * gain_ref[...]).astype(BF16)
    d_ff = wg_ref.shape[1]
    for f0 in range(0, d_ff, FF_TILE):
        g = jnp.dot(h, wg_ref[:, f0:f0 + FF_TILE], preferred_element_type=F32)
        u = jnp.dot(h, wu_ref[:, f0:f0 + FF_TILE], preferred_element_type=F32)
        a_ref[:, f0:f0 + FF_TILE] = (_silu(g) * u).astype(BF16)
    return x + 0.5 * jnp.dot(a_ref[...], wd_ref[...], preferred_element_type=F32)


def _ffn_out_kernel(x_ref, attn_ref, dn_ref, woa_ref, wod_ref, gain_ref, wg_ref, wu_ref, wd_ref,
                    gfin_ref, o_ref, a_ref, *, final):
    x = x_ref[...] + jnp.dot(attn_ref[...], woa_ref[...], preferred_element_type=F32)
    x = x + jnp.dot(dn_ref[...], wod_ref[...], preferred_element_type=F32)
    out = _swiglu_step(x, gain_ref, wg_ref, wu_ref, wd_ref, a_ref)
    if final:
        out = _rms(out) * gfin_ref[...]
    o_ref[...] = out


def _ffn_out(x, attn, dn, wo_attn, wo_dn, gain, wg, wu, wd, gfin, *, final):
    t, d = x.shape
    d_ff = wg.shape[1]
    tm = FFN_TILE
    row = lambda i: (i, 0)
    return pl.pallas_call(
        functools.partial(_ffn_out_kernel, final=final),
        grid=(t // tm,),
        in_specs=[pl.BlockSpec((tm, d), row), pl.BlockSpec((tm, attn.shape[1]), row),
                  pl.BlockSpec((tm, dn.shape[1]), row), _resident(wo_attn.shape),
                  _resident(wo_dn.shape), _resident((1, d)), _resident((d, d_ff)),
                  _resident((d, d_ff)), _resident((d_ff, d)), _resident((1, d))],
        out_specs=pl.BlockSpec((tm, d), row),
        out_shape=jax.ShapeDtypeStruct((t, d), F32),
        scratch_shapes=[pltpu.VMEM((tm, d_ff), BF16)],
        compiler_params=_params("arbitrary"),
        name="ffn_out",
    )(x, attn, dn, wo_attn, wo_dn, gain, wg, wu, wd, gfin)


def _ffn_in_kernel(x_ref, gain_ref, wg_ref, wu_ref, wd_ref, o_ref, a_ref):
    o_ref[...] = _swiglu_step(x_ref[...], gain_ref, wg_ref, wu_ref, wd_ref, a_ref)


def _ffn_in(x, gain, wg, wu, wd):
    t, d = x.shape
    d_ff = wg.shape[1]
    tm = FFN_TILE
    row = lambda i: (i, 0)
    return pl.pallas_call(
        _ffn_in_kernel,
        grid=(t // tm,),
        in_specs=[pl.BlockSpec((tm, d), row), _resident((1, d)), _resident((d, d_ff)),
                  _resident((d, d_ff)), _resident((d_ff, d))],
        out_specs=pl.BlockSpec((tm, d), row),
        out_shape=jax.ShapeDtypeStruct((t, d), F32),
        scratch_shapes=[pltpu.VMEM((tm, d_ff), BF16)],
        compiler_params=_params("arbitrary"),
        name="ffn_in",
    )(x, gain, wg, wu, wd)


def _inproj_kernel(x_ref, gain_ref, wm_ref, ws_ref, cw_ref,
                   aq_ref, ak_ref, av_ref, dq_ref, dk_ref, dv_ref, gate_ref, small_ref,
                   xc_ref, *, tiles_per_seq):
    i = pl.program_id(0)
    tm = x_ref.shape[0]
    halo = SUBLANES

    @pl.when(i % tiles_per_seq == 0)
    def _():
        xc_ref[0:halo, :] = jnp.zeros((halo, xc_ref.shape[1]), F32)

    @pl.when(i % tiles_per_seq != 0)
    def _():
        xc_ref[0:halo, :] = xc_ref[tm:tm + halo, :]

    h = (_rms(x_ref[...]) * gain_ref[...]).astype(BF16)

    def section(j, width=ATTN_WIDTH):
        return jnp.dot(h, wm_ref[:, j * width:(j + 1) * width], preferred_element_type=F32)

    for j in range(3):
        xc_ref[halo:halo + tm, j * DN_WIDTH:(j + 1) * DN_WIDTH] = section(3 + j)

    outs = (dq_ref, dk_ref, dv_ref)
    sub = lax.broadcasted_iota(jnp.int32, (halo, DN_HEAD_DIM), 0)
    for c in range(3 * DN_HEADS):
        sl = slice(c * DN_HEAD_DIM, (c + 1) * DN_HEAD_DIM)
        cur = xc_ref[halo:halo + tm, sl]
        tail = xc_ref[0:halo, sl]
        conv = None
        for j in range(CONV_WIDTH):
            back = CONV_WIDTH - 1 - j
            if back:
                shifted = pltpu.roll(cur, back, 0)
                head = jnp.where(sub < back, pltpu.roll(tail, back, 0), shifted[0:halo])
                shifted = jnp.concatenate([head, shifted[halo:]], axis=0)
            else:
                shifted = cur
            term = shifted * cw_ref[j:j + 1, sl]
            conv = term if conv is None else conv + term
        a = _silu(conv)
        if c < 2 * DN_HEADS:
            a = a * lax.rsqrt(jnp.sum(a * a, axis=-1, keepdims=True) + L2_EPS)
        if c < DN_HEADS:
            a = a * (DN_HEAD_DIM ** -0.5)
        hs = slice((c % DN_HEADS) * DN_HEAD_DIM, (c % DN_HEADS + 1) * DN_HEAD_DIM)
        outs[c // DN_HEADS][:, hs] = a

    aq_ref[...] = section(0) * (ATTN_HEAD_DIM ** -0.5 * LOG2_E)
    ak_ref[...] = section(1)
    av_ref[...] = section(2)
    gate_ref[...] = _silu(section(6))
    small_ref[...] = jnp.dot(h, ws_ref[...], preferred_element_type=F32)


def _inproj(x, gain, w_main, w_small, conv_w, *, seq):
    t, d = x.shape
    tm = TOKEN_TILE
    row = lambda i: (i, 0)
    wide = jax.ShapeDtypeStruct((t, ATTN_WIDTH), F32)
    return pl.pallas_call(
        functools.partial(_inproj_kernel, tiles_per_seq=seq // tm),
        grid=(t // tm,),
        in_specs=[pl.BlockSpec((tm, d), row), _resident((1, d)), _resident(w_main.shape),
                  _resident(w_small.shape), _resident(conv_w.shape)],
        out_specs=[pl.BlockSpec((tm, ATTN_WIDTH), row)] * 7 + [pl.BlockSpec((tm, LANES), row)],
        out_shape=[wide] * 7 + [jax.ShapeDtypeStruct((t, LANES), F32)],
        scratch_shapes=[pltpu.VMEM((tm + 2 * SUBLANES, 3 * DN_WIDTH), F32)],
        compiler_params=_params("arbitrary"),
        name="inproj",
    )(x, gain, w_main, w_small, conv_w)


FINE = 8
COARSE = 16


def _chunks(dilation):
    return FINE // dilation


def _tile_order(dilation, size):
    c = _chunks(dilation) if dilation < FINE else 1
    per = size // c
    x = np.arange(size)
    return c * (x % per) + x // per


def _attn_group(refs, stage, bias_ref, dilation, blocks, *, init):
    fine, coarse, acc_ref, m_ref, l_ref = refs
    s_ref, p_ref, alpha_ref = stage
    d = dilation
    blk = ATTN_BLOCK
    seq = acc_ref.shape[0]
    lane = lax.broadcasted_iota(jnp.int32, (blk, LANES), 1)
    head_lanes = [(lane >= h * ATTN_HEAD_DIM) & (lane < (h + 1) * ATTN_HEAD_DIM)
                  for h in range(HEADS_PER_SLAB)]

    def gather(ref, starts, rows):
        parts = [ref[pl.ds(pl.multiple_of(st, 16), rows), :] for st in starts]
        return parts[0] if len(parts) == 1 else jnp.concatenate(parts, axis=0)

    plans = []
    for r, n in blocks:
        n_key = jnp.maximum(n - 1, 0)
        if d < FINE:
            c = _chunks(d)
            per = seq // FINE
            bases = [(r + d * b) * per for b in range(c)]
            q_starts = [base + (blk // c) * n for base in bases]
            k_starts = [base + (blk // c) * n_key for base in bases]
            q_refs, k_ref, v_ref = fine
            state = [(st, blk // c, 1) for st in q_starts]
        else:
            c = 1
            per = seq // COARSE
            q_starts = [r * per + blk * n]
            k_starts = [r * per + blk * n_key]
            q_refs, k_ref, v_ref = coarse
            ratio = d // FINE
            state = [((r % FINE) * (seq // FINE) + ratio * blk * n + r // FINE, blk, ratio)]
        plans.append((q_refs, k_ref, v_ref, q_starts, k_starts, c, state, jnp.minimum(n, 1)))

    def state_rows(ref, state):
        parts = []
        for start, rows, stride in state:
            if stride == 1:
                parts.append(ref[pl.ds(pl.multiple_of(start, 16), rows), :])
            else:
                parts.append(ref[pl.ds(start, rows, stride=stride), :])
        return parts[0] if len(parts) == 1 else jnp.concatenate(parts, axis=0)

    for i, (q_refs, k_ref, _, q_starts, k_starts, c, _, _) in enumerate(plans):
        kb = gather(k_ref, k_starts, 2 * blk // c)
        for h in range(HEADS_PER_SLAB):
            qh = gather(q_refs[h], q_starts, blk // c)
            s_ref[i * HEADS_PER_SLAB + h] = lax.dot_general(
                qh, kb, (((1,), (1,)), ((), ())), preferred_element_type=F32)

    pending = []
    for i, plan in enumerate(plans):
        state, table = plan[6], plan[7]
        m_prev = None if init else state_rows(m_ref, state)
        m_heads, l_heads = [], []
        for h in range(HEADS_PER_SLAB):
            t = i * HEADS_PER_SLAB + h
            s = s_ref[t] + bias_ref[table, h]
            top = jnp.maximum(s[:, :LANES], s[:, LANES:])
            if not init:
                top = jnp.maximum(top, jnp.where(head_lanes[h], m_prev, MASK_VALUE))
            m_new = jnp.broadcast_to(jnp.max(top, axis=-1, keepdims=True), (blk, LANES))
            p = jnp.exp2(s - jnp.concatenate([m_new, m_new], axis=1))
            l_heads.append(jnp.broadcast_to(jnp.sum(p, axis=-1, keepdims=True), (blk, LANES)))
            m_heads.append(m_new)
            p_ref[t] = p.astype(BF16)
        m_new = jnp.where(head_lanes[0], m_heads[0], m_heads[1])
        l_new = jnp.where(head_lanes[0], l_heads[0], l_heads[1])
        if not init:
            alpha = jnp.exp2(m_prev - m_new)
            alpha_ref[i] = alpha
            l_new = state_rows(l_ref, state) * alpha + l_new
        pending.append((m_ref, state, m_new))
        pending.append((l_ref, state, l_new))

    for i, plan in enumerate(plans):
        v_ref, k_starts, c, state = plan[2], plan[4], plan[5], plan[6]
        vb = gather(v_ref, k_starts, 2 * blk // c)
        pv = [jnp.dot(p_ref[i * HEADS_PER_SLAB + h], vb, preferred_element_type=F32)
              for h in range(HEADS_PER_SLAB)]
        acc_new = jnp.where(head_lanes[0], pv[0], pv[1])
        if not init:
            acc_new = state_rows(acc_ref, state) * alpha_ref[i] + acc_new
        pending.append((acc_ref, state, acc_new))

    for ref, state, value in pending:
        offset = 0
        for start, rows, stride in state:
            piece = value[offset:offset + rows]
            if stride == 1:
                ref[pl.ds(pl.multiple_of(start, 16), rows), :] = piece
            else:
                ref[pl.ds(start, rows, stride=stride), :] = piece
            offset += rows


def _deinterleave(src_ref, tmp_ref, dst_fine, dst_coarse, *, masks):
    seq = src_ref.shape[0]
    quarter = seq // 4
    piece = DEINTERLEAVE_ROWS

    for r4 in range(4):
        for off in range(0, quarter, piece):
            tmp_ref[r4 * quarter + off:r4 * quarter + off + piece, :] = \
                src_ref[pl.ds(r4 + 4 * off, piece, stride=4), :]

    for layout, dsts in ((FINE, dst_fine), (COARSE, dst_coarse)):
        per = seq // layout
        sub = layout // 4
        for res in range(layout):
            for off in range(0, per, piece):
                value = tmp_ref[pl.ds((res % 4) * quarter + res // 4 + sub * off, piece,
                                      stride=sub), :]
                rows = slice(res * per + off, res * per + off + piece)
                if masks is None:
                    dsts[0][rows, :] = value.astype(BF16)
                else:
                    for dst, mask in zip(dsts, masks):
                        dst[rows, :] = jnp.where(mask, value, 0.0).astype(BF16)


def _attn_kernel(q_ref, k_ref, v_ref, b1_ref, b4_ref, b16_ref, o_ref,
                 qf0, qf1, kf, vf, qc0, qc1, kc, vc, tmp_ref, acc_ref, m_ref, l_ref,
                 s_ref, p_ref, alpha_ref):
    seq = q_ref.shape[0]
    blk = ATTN_BLOCK
    grp = ATTN_GROUP
    lane = lax.broadcasted_iota(jnp.int32, (DEINTERLEAVE_ROWS, LANES), 1)
    masks = [(lane >= h * ATTN_HEAD_DIM) & (lane < (h + 1) * ATTN_HEAD_DIM)
             for h in range(HEADS_PER_SLAB)]
    _deinterleave(q_ref, tmp_ref, (qf0, qf1), (qc0, qc1), masks=masks)
    _deinterleave(k_ref, tmp_ref, (kf,), (kc,), masks=None)
    _deinterleave(v_ref, tmp_ref, (vf,), (vc,), masks=None)

    refs = (((qf0, qf1), kf, vf), ((qc0, qc1), kc, vc), acc_ref, m_ref, l_ref)
    stage = (s_ref, p_ref, alpha_ref)
    patterns = sorted(zip(DILATED_PATTERNS, (b1_ref, b4_ref, b16_ref)), key=lambda e: -e[0][1])
    for idx, ((_, d), bias_ref) in enumerate(patterns):
        nb = seq // (d * blk)
        run = functools.partial(_attn_group, refs, stage, bias_ref, d, init=(idx == 0))
        if d < grp:
            per_res = nb // grp

            def body(i, carry, run=run, per_res=per_res):
                r = i // per_res
                n0 = (i % per_res) * grp
                run([(r, n0 + j) for j in range(grp)])
                return carry

            lax.fori_loop(0, d * per_res, body, 0)
        else:
            per_blk = d // grp

            def body(i, carry, run=run, per_blk=per_blk):
                n = i // per_blk
                r0 = (i % per_blk) * grp
                run([(r0 + j, n) for j in range(grp)])
                return carry

            lax.fori_loop(0, nb * per_blk, body, 0)

    per = seq // FINE

    def out_body(i, carry):
        res = i // (per // OUT_ROWS)
        off = (i % (per // OUT_ROWS)) * OUT_ROWS
        rows = pl.ds(pl.multiple_of(res * per + off, OUT_ROWS), OUT_ROWS)
        tmp_ref[pl.ds(res + FINE * off, OUT_ROWS, stride=FINE), :] = acc_ref[rows, :] / l_ref[rows, :]
        return carry

    lax.fori_loop(0, FINE * (per // OUT_ROWS), out_body, 0)

    def cast_body(i, carry):
        rows = pl.ds(pl.multiple_of(i * OUT_ROWS, OUT_ROWS), OUT_ROWS)
        o_ref[rows, :] = tmp_ref[rows, :].astype(o_ref.dtype)
        return carry

    lax.fori_loop(0, seq // OUT_ROWS, cast_body, 0)


def _attn_bias(window, dilation):
    slopes = np.array([2.0 ** (-8.0 * (i + 1) / ATTN_HEADS) for i in range(ATTN_HEADS)],
                      dtype=np.float32)
    qi = _tile_order(dilation, ATTN_BLOCK)[:, None]
    kj = _tile_order(dilation, 2 * ATTN_BLOCK)[None, :]
    tables = []
    for steps in (qi - kj, qi + ATTN_BLOCK - kj):
        valid = (steps >= 0) & (steps <= window // dilation)
        bias = -slopes[:, None, None] * (steps * dilation).astype(np.float32) * np.float32(LOG2_E)
        tables.append(np.where(valid[None], bias, np.float32(MASK_VALUE)))
    table = np.stack(tables).astype(np.float32)
    table = table.reshape(2, ATTN_HEADS // HEADS_PER_SLAB, HEADS_PER_SLAB, *table.shape[2:])
    return jnp.asarray(table.transpose(1, 0, 2, 3, 4))


def _attention(q, k, v, *, batch, seq):
    slabs = ATTN_WIDTH // LANES
    per_seq = pl.BlockSpec((seq, LANES), lambda bi, j: (bi, j))
    bias = [_attn_bias(w, d) for w, d in DILATED_PATTERNS]
    bias_spec = pl.BlockSpec((None,) + bias[0].shape[1:], lambda bi, j: (j, 0, 0, 0, 0))
    rows16 = pltpu.VMEM((seq, LANES), BF16)
    rows32 = pltpu.VMEM((seq, LANES), F32)
    tiles = ATTN_GROUP * HEADS_PER_SLAB
    return pl.pallas_call(
        _attn_kernel,
        grid=(batch, slabs),
        in_specs=[per_seq, per_seq, per_seq, bias_spec, bias_spec, bias_spec],
        out_specs=per_seq,
        out_shape=jax.ShapeDtypeStruct((batch * seq, ATTN_WIDTH), BF16),
        scratch_shapes=[rows16] * 8 + [rows32] * 4 + [
            pltpu.VMEM((tiles, ATTN_BLOCK, 2 * ATTN_BLOCK), F32),
            pltpu.VMEM((tiles, ATTN_BLOCK, 2 * ATTN_BLOCK), BF16),
            pltpu.VMEM((ATTN_GROUP, ATTN_BLOCK, LANES), F32),
        ],
        compiler_params=_params("arbitrary", "arbitrary"),
        name="attention",
    )(q, k, v, *bias)


def _bmm(a, b):
    return lax.dot_general(a.astype(BF16), b.astype(BF16), (((2,), (1,)), ((0,), (0,))),
                           preferred_element_type=F32)


def _bmm_nt(a, b):
    return lax.dot_general(a.astype(BF16), b.astype(BF16), (((2,), (2,)), ((0,), (0,))),
                           preferred_element_type=F32)


def _unit_lower_inverse(a_strict, row, col):
    c = a_strict.shape[-1]
    eye = (row == col).astype(F32)
    t = None
    size = 1
    while size < c:
        shift = size.bit_length() - 1
        join = ((row >> (shift + 1)) == (col >> (shift + 1))) & (((row >> shift) & 1) == 1) \
            & (((col >> shift) & 1) == 0)
        lk = jnp.where(join, a_strict, 0.0)
        if t is None:
            t = eye - lk
        else:
            t = t - _bmm(t, _bmm(lk, t))
        size *= 2
    return t


def _dn_kernel(q_ref, k_ref, v_ref, gate_ref, small_ref, arow_ref, dtrow_ref, nrm_ref,
               o_ref, s_ref, wq_ref, u_ref, at_ref, kd_ref):
    c = DN_CHUNK
    dh = DN_HEAD_DIM
    batch, tc = q_ref.shape[:2]
    g_chunks = tc // c
    lanes = batch * DN_HEADS
    nb = g_chunks * lanes

    @pl.when(pl.program_id(0) == 0)
    def _():
        s_ref[...] = jnp.zeros(s_ref.shape, F32)

    sm = small_ref[...]
    beta_all = jax.nn.sigmoid(sm)
    z = sm + dtrow_ref[...]
    softplus = jnp.maximum(z, 0.0) + jnp.log1p(jnp.exp(-jnp.abs(z)))
    g_all = -jnp.exp(arow_ref[...]) * softplus

    def per_head(ref):
        parts = [ref[b, :, hd * dh:(hd + 1) * dh].reshape(g_chunks, 1, c, dh)
                 for b in range(batch) for hd in range(DN_HEADS)]
        return jnp.concatenate(parts, axis=1).reshape(nb, c, dh)

    def per_head_col(x, offset):
        parts = [x[b, :, offset + hd:offset + hd + 1].reshape(g_chunks, 1, c, 1)
                 for b in range(batch) for hd in range(DN_HEADS)]
        return jnp.concatenate(parts, axis=1).reshape(nb, c, 1)

    q = per_head(q_ref)
    k = per_head(k_ref)
    v = per_head(v_ref)
    beta = per_head_col(beta_all, 0)
    g = per_head_col(g_all, DN_HEADS)

    row = lax.broadcasted_iota(jnp.int32, (c, c), 0)
    col = lax.broadcasted_iota(jnp.int32, (c, c), 1)
    causal = row >= col
    strict = row > col

    g_mat = jnp.broadcast_to(g, (nb, c, c))
    g_row = jnp.sum(jnp.where(row == col, g_mat, 0.0), axis=1, keepdims=True)
    gc_col = jnp.sum(jnp.where(causal, g_row, 0.0), axis=2, keepdims=True)
    gc_row = jnp.sum(jnp.where(row <= col, g_mat, 0.0), axis=1, keepdims=True)
    decay = jnp.where(causal, jnp.exp(jnp.where(causal, gc_col - gc_row, 0.0)), 0.0)

    k_beta = k * beta
    e_gc = jnp.exp(gc_col)
    g_last = gc_col[:, c - 1:c, :]
    both = _bmm_nt(jnp.concatenate([k_beta, q], axis=1), k)
    a_mat = jnp.where(strict, both[:, :c] * decay, 0.0)
    at_ref[...] = jnp.where(causal, both[:, c:] * decay, 0.0).astype(BF16)
    rhs = jnp.concatenate([v * beta, k_beta * e_gc], axis=-1)
    sol = _bmm(_unit_lower_inverse(a_mat, row, col), rhs)
    u_ref[...] = sol[..., :dh]
    wq_ref[:, :c, :] = sol[..., dh:].astype(BF16)
    wq_ref[:, c:, :] = (q * e_gc).astype(BF16)
    kd_ref[...] = (k * jnp.exp(g_last - gc_col)).astype(BF16)
    e_last = jnp.exp(g_last)

    state = s_ref[...]
    outs = []
    for n in range(g_chunks):
        sl = slice(n * lanes, (n + 1) * lanes)
        ws = _bmm(wq_ref[sl], state)
        v_new = u_ref[sl] - ws[:, :c]
        outs.append(ws[:, c:] + _bmm(at_ref[sl], v_new))
        state = state * e_last[sl] + lax.dot_general(
            kd_ref[sl], v_new.astype(BF16), (((1,), (1,)), ((0,), (0,))),
            preferred_element_type=F32)
    s_ref[...] = state

    for b in range(batch):
        for hd in range(DN_HEADS):
            sl = slice(hd * dh, (hd + 1) * dh)
            o = jnp.concatenate([outs[n][b * DN_HEADS + hd] for n in range(g_chunks)], axis=0)
            o = o * lax.rsqrt(jnp.mean(o * o, axis=-1, keepdims=True) + NORM_EPS)
            o_ref[b, :, sl] = (o * nrm_ref[...] * gate_ref[b, :, sl]).astype(o_ref.dtype)


def _deltanet(dq, dk, dv, gate, small, arow, dtrow, nrm, *, batch, seq):
    tc = DN_TOKENS
    nb = (tc // DN_CHUNK) * batch * DN_HEADS
    view = lambda t: t.reshape(batch, seq, t.shape[-1])
    step = lambda j: (0, j, 0)
    wide = pl.BlockSpec((batch, tc, DN_WIDTH), step)
    out = pl.pallas_call(
        _dn_kernel,
        grid=(seq // tc,),
        in_specs=[wide, wide, wide, wide, pl.BlockSpec((batch, tc, LANES), step),
                  _resident((1, LANES)), _resident((1, LANES)), _resident((1, DN_HEAD_DIM))],
        out_specs=wide,
        out_shape=jax.ShapeDtypeStruct((batch, seq, DN_WIDTH), BF16),
        scratch_shapes=[
            pltpu.VMEM((batch * DN_HEADS, DN_HEAD_DIM, DN_HEAD_DIM), F32),
            pltpu.VMEM((nb, 2 * DN_CHUNK, DN_HEAD_DIM), BF16),
            pltpu.VMEM((nb, DN_CHUNK, DN_HEAD_DIM), F32),
            pltpu.VMEM((nb, DN_CHUNK, DN_CHUNK), BF16),
            pltpu.VMEM((nb, DN_CHUNK, DN_HEAD_DIM), BF16),
        ],
        compiler_params=_params("arbitrary"),
        name="deltanet",
    )(view(dq), view(dk), view(dv), view(gate), view(small), arow, dtrow, nrm)
    return out.reshape(batch * seq, DN_WIDTH)


def _lane_row(values, offset):
    return jnp.zeros((1, LANES), F32).at[0, offset:offset + values.shape[0]].set(values.astype(F32))


def kernel(x, norm_ffn1, ffn1_gate, ffn1_up, ffn1_down, norm_mix, w_in, conv_w, a_log, dt_bias,
           dn_norm, w_out, norm_ffn2, ffn2_gate, ffn2_up, ffn2_down, norm_final):
    batch, seq, d = x.shape
    depth = norm_ffn1.shape[0]
    assert seq % (max(dl for _, dl in DILATED_PATTERNS) * ATTN_BLOCK) == 0
    assert (batch * seq) % FFN_TILE == 0
    assert seq % TOKEN_TILE == 0 and seq % DN_TOKENS == 0 and seq % OUT_ROWS == 0
    xt = x.reshape(batch * seq, d)
    gfin = norm_final.reshape(1, d).astype(F32)
    small_lo = 6 * ATTN_WIDTH
    small_hi = small_lo + 2 * DN_HEADS

    for i in range(depth):
        row = lambda g: g.reshape(1, -1).astype(F32)
        wi = w_in[i]
        w_main = jnp.concatenate([wi[:, :small_lo], wi[:, small_hi:]], axis=1).astype(BF16)
        w_small = jnp.pad(wi[:, small_lo:small_hi], ((0, 0), (0, LANES - 2 * DN_HEADS))).astype(BF16)
        xt = _ffn_in(xt, row(norm_ffn1[i]), ffn1_gate[i].astype(BF16), ffn1_up[i].astype(BF16),
                     ffn1_down[i].astype(BF16))
        aq, ak, av, dq, dk, dv, gate, small = _inproj(
            xt, row(norm_mix[i]), w_main, w_small, conv_w[i].astype(F32), seq=seq)

        attn = _attention(aq, ak, av, batch=batch, seq=seq)
        dn = _deltanet(dq, dk, dv, gate, small, _lane_row(a_log[i], DN_HEADS),
                       _lane_row(dt_bias[i], DN_HEADS), row(dn_norm[i]), batch=batch, seq=seq)

        wo = w_out[i].astype(BF16)
        xt = _ffn_out(xt, attn, dn, wo[:ATTN_WIDTH], wo[ATTN_WIDTH:], row(norm_ffn2[i]),
                      ffn2_gate[i].astype(BF16), ffn2_up[i].astype(BF16),
                      ffn2_down[i].astype(BF16), gfin, final=(i == depth - 1))

    return xt.reshape(batch, seq, d)
```

```python
import functools

import numpy as np
import jax
import jax.numpy as jnp
from jax import lax
from jax.experimental import pallas as pl
from jax.experimental.pallas import tpu as pltpu

F32 = jnp.float32
BF16 = jnp.bfloat16

ATTN_HEADS = 8
ATTN_HEAD_DIM = 64
ATTN_WIDTH = ATTN_HEADS * ATTN_HEAD_DIM
DILATED_PATTERNS = ((128, 1), (512, 4), (2048, 16))
ATTN_BLOCK = 128
DN_HEADS = 4
DN_HEAD_DIM = 128
DN_WIDTH = DN_HEADS * DN_HEAD_DIM
DN_CHUNK = 64
CONV_WIDTH = 4
NORM_EPS = 1e-6
L2_EPS = 1e-6

LANES = 128
SUBLANES = 8
VMEM_LIMIT_BYTES = 56 * 1024 * 1024

MASK_VALUE = -1e30
LOG2_E = 1.4426950408889634

TOKEN_TILE = 512
FFN_TILE = 1024
FF_TILE = 256
DN_TOKENS = 128
HEADS_PER_SLAB = LANES // ATTN_HEAD_DIM
OUT_ROWS = 512
ATTN_GROUP = 8
DEINTERLEAVE_ROWS = 256
CAST_ROWS = 256


def _rms(x):
    return x * lax.rsqrt(jnp.mean(x * x, axis=-1, keepdims=True) + NORM_EPS)


def _silu(x):
    half = 0.5 * x
    return half + half * jnp.tanh(half)


def _mm(a, b):
    return jnp.dot(a.astype(BF16), b.astype(BF16), preferred_element_type=F32)


def _resident(shape):
    nd = len(shape)
    return pl.BlockSpec(shape, lambda *_: (0,) * nd, pipeline_mode=pl.Buffered(1))


def _params(*sem):
    return pltpu.CompilerParams(dimension_semantics=sem, vmem_limit_bytes=VMEM_LIMIT_BYTES)


def _cast_kernel(w_ref, o_ref):
    o_ref[...] = w_ref[...].astype(o_ref.dtype)


def _to_bf16(w, layer):
    _, r, c = w.shape
    rows = CAST_ROWS
    return pl.pallas_call(
        _cast_kernel,
        grid=(r // rows,),
        in_specs=[pl.BlockSpec((None, rows, c), lambda j: (layer, j, 0))],
        out_specs=pl.BlockSpec((rows, c), lambda j: (j, 0)),
        out_shape=jax.ShapeDtypeStruct((r, c), BF16),
        compiler_params=_params("arbitrary"),
        name="weight_cast",
    )(w)


def _split_w_in_kernel(w_ref, main_ref, small_ref, *, small_lo, small_hi):
    w = w_ref[...]
    main_ref[...] = jnp.concatenate([w[:, :small_lo], w[:, small_hi:]], axis=1).astype(BF16)
    lane = lax.broadcasted_iota(jnp.int32, (w.shape[0], LANES), 1)
    small = jnp.where(lane < small_hi - small_lo, w[:, small_lo:small_lo + LANES], 0.0)
    small_ref[...] = small.astype(BF16)


def _split_w_in(w_in, layer, small_lo, small_hi):
    _, r, c = w_in.shape
    rows = CAST_ROWS
    assert small_lo % LANES == 0 and small_hi - small_lo <= LANES
    return pl.pallas_call(
        functools.partial(_split_w_in_kernel, small_lo=small_lo, small_hi=small_hi),
        grid=(r // rows,),
        in_specs=[pl.BlockSpec((None, rows, c), lambda j: (layer, j, 0))],
        out_specs=[pl.BlockSpec((rows, c - (small_hi - small_lo)), lambda j: (j, 0)),
                   pl.BlockSpec((rows, LANES), lambda j: (j, 0))],
        out_shape=[jax.ShapeDtypeStruct((r, c - (small_hi - small_lo)), BF16),
                   jax.ShapeDtypeStruct((r, LANES), BF16)],
        compiler_params=_params("arbitrary"),
        name="weight_split",
    )(w_in)


def _swiglu_step(x, gain_ref, wg_ref, wu_ref, wd_ref, a_ref):
    h = (_rms(x) * gain_ref[...]).astype(BF16)
    d_ff = wg_ref.shape[1]
    for f0 in range(0, d_ff, FF_TILE):
        g = jnp.dot(h, wg_ref[:, f0:f0 + FF_TILE], preferred_element_type=F32)
        u = jnp.dot(h, wu_ref[:, f0:f0 + FF_TILE], preferred_element_type=F32)
        a_ref[:, f0:f0 + FF_TILE] = (_silu(g) * u).astype(BF16)
    return x + 0.5 * jnp.dot(a_ref[...], wd_ref[...], preferred_element_type=F32)


def _ffn_out_kernel(x_ref, attn_ref, dn_ref, woa_ref, wod_ref, gain_ref, wg_ref, wu_ref, wd_ref,
                    gfin_ref, o_ref, a_ref, *, final):
    x = x_ref[...] + jnp.dot(attn_ref[...], woa_ref[...], preferred_element_type=F32)
    x = x + jnp.dot(dn_ref[...], wod_ref[...], preferred_element_type=F32)
    out = _swiglu_step(x, gain_ref, wg_ref, wu_ref, wd_ref, a_ref)
    if final:
        out = _rms(out) * gfin_ref[...]
    o_ref[...] = out


def _ffn_out(x, attn, dn, w_out, gain, wg, wu, wd, gfin, *, final):
    t, d = x.shape
    d_ff = wg.shape[1]
    tm = FFN_TILE
    row = lambda i: (i, 0)
    assert attn.shape[1] == dn.shape[1] and w_out.shape[0] == 2 * attn.shape[1]
    half = lambda k: pl.BlockSpec((attn.shape[1], d), lambda i: (k, 0),
                                  pipeline_mode=pl.Buffered(1))
    return pl.pallas_call(
        functools.partial(_ffn_out_kernel, final=final),
        grid=(t // tm,),
        in_specs=[pl.BlockSpec((tm, d), row), pl.BlockSpec((tm, attn.shape[1]), row),
                  pl.BlockSpec((tm, dn.shape[1]), row), half(0),
                  half(1), _resident((1, d)), _resident((d, d_ff)),
                  _resident((d, d_ff)), _resident((d_ff, d)), _resident((1, d))],
        out_specs=pl.BlockSpec((tm, d), row),
        out_shape=jax.ShapeDtypeStruct((t, d), F32),
        scratch_shapes=[pltpu.VMEM((tm, d_ff), BF16)],
        compiler_params=_params("arbitrary"),
        name="ffn_out",
    )(x, attn, dn, w_out, w_out, gain, wg, wu, wd, gfin)


def _ffn_in_kernel(x_ref, gain_ref, wg_ref, wu_ref, wd_ref, o_ref, a_ref):
    o_ref[...] = _swiglu_step(x_ref[...], gain_ref, wg_ref, wu_ref, wd_ref, a_ref)


def _ffn_in(x, gain, wg, wu, wd):
    t, d = x.shape
    d_ff = wg.shape[1]
    tm = FFN_TILE
    row = lambda i: (i, 0)
    return pl.pallas_call(
        _ffn_in_kernel,
        grid=(t // tm,),
        in_specs=[pl.BlockSpec((tm, d), row), _resident((1, d)), _resident((d, d_ff)),
                  _resident((d, d_ff)), _resident((d_ff, d))],
        out_specs=pl.BlockSpec((tm, d), row),
        out_shape=jax.ShapeDtypeStruct((t, d), F32),
        scratch_shapes=[pltpu.VMEM((tm, d_ff), BF16)],
        compiler_params=_params("arbitrary"),
        name="ffn_in",
    )(x, gain, wg, wu, wd)


def _inproj_kernel(x_ref, gain_ref, wm_ref, ws_ref, cw_ref,
                   aq_ref, ak_ref, av_ref, dq_ref, dk_ref, dv_ref, gate_ref, small_ref,
                   xc_ref, *, tiles_per_seq):
    i = pl.program_id(0)
    tm = x_ref.shape[0]
    halo = SUBLANES

    @pl.when(i % tiles_per_seq == 0)
    def _():
        xc_ref[0:halo, :] = jnp.zeros((halo, xc_ref.shape[1]), F32)

    @pl.when(i % tiles_per_seq != 0)
    def _():
        xc_ref[0:halo, :] = xc_ref[tm:tm + halo, :]

    h = (_rms(x_ref[...]) * gain_ref[...]).astype(BF16)

    def section(j, width=ATTN_WIDTH):
        return jnp.dot(h, wm_ref[:, j * width:(j + 1) * width], preferred_element_type=F32)

    for j in range(3):
        xc_ref[halo:halo + tm, j * DN_WIDTH:(j + 1) * DN_WIDTH] = section(3 + j)

    outs = (dq_ref, dk_ref, dv_ref)
    sub = lax.broadcasted_iota(jnp.int32, (halo, DN_HEAD_DIM), 0)
    for c in range(3 * DN_HEADS):
        sl = slice(c * DN_HEAD_DIM, (c + 1) * DN_HEAD_DIM)
        cur = xc_ref[halo:halo + tm, sl]
        tail = xc_ref[0:halo, sl]
        conv = None
        for j in range(CONV_WIDTH):
            back = CONV_WIDTH - 1 - j
            if back:
                shifted = pltpu.roll(cur, back, 0)
                head = jnp.where(sub < back, pltpu.roll(tail, back, 0), shifted[0:halo])
                shifted = jnp.concatenate([head, shifted[halo:]], axis=0)
            else:
                shifted = cur
            term = shifted * cw_ref[j:j + 1, sl]
            conv = term if conv is None else conv + term
        a = _silu(conv)
        if c < 2 * DN_HEADS:
            a = a * lax.rsqrt(jnp.sum(a * a, axis=-1, keepdims=True) + L2_EPS)
        if c < DN_HEADS:
            a = a * (DN_HEAD_DIM ** -0.5)
        hs = slice((c % DN_HEADS) * DN_HEAD_DIM, (c % DN_HEADS + 1) * DN_HEAD_DIM)
        outs[c // DN_HEADS][:, hs] = a

    aq_ref[...] = section(0) * (ATTN_HEAD_DIM ** -0.5 * LOG2_E)
    ak_ref[...] = section(1)
    av_ref[...] = section(2)
    gate_ref[...] = _silu(section(6))
    small_ref[...] = jnp.dot(h, ws_ref[...], preferred_element_type=F32)


def _inproj(x, gain, w_main, w_small, conv_w, *, seq):
    t, d = x.shape
    tm = TOKEN_TILE
    row = lambda i: (i, 0)
    wide = jax.ShapeDtypeStruct((t, ATTN_WIDTH), F32)
    return pl.pallas_call(
        functools.partial(_inproj_kernel, tiles_per_seq=seq // tm),
        grid=(t // tm,),
        in_specs=[pl.BlockSpec((tm, d), row), _resident((1, d)), _resident(w_main.shape),
                  _resident(w_small.shape), _resident(conv_w.shape)],
        out_specs=[pl.BlockSpec((tm, ATTN_WIDTH), row)] * 7 + [pl.BlockSpec((tm, LANES), row)],
        out_shape=[wide] * 7 + [jax.ShapeDtypeStruct((t, LANES), F32)],
        scratch_shapes=[pltpu.VMEM((tm + 2 * SUBLANES, 3 * DN_WIDTH), F32)],
        compiler_params=_params("arbitrary"),
        name="inproj",
    )(x, gain, w_main, w_small, conv_w)


FINE = 8
COARSE = 16


def _chunks(dilation):
    return FINE // dilation


def _tile_order(dilation, size):
    c = _chunks(dilation) if dilation < FINE else 1
    per = size // c
    x = np.arange(size)
    return c * (x % per) + x // per


def _attn_group(refs, stage, bias_ref, dilation, blocks, *, init):
    fine, coarse, acc_ref, m_ref, l_ref = refs
    s_ref, p_ref, alpha_ref = stage
    d = dilation
    blk = ATTN_BLOCK
    seq = acc_ref.shape[0]
    lane = lax.broadcasted_iota(jnp.int32, (blk, LANES), 1)
    head_lanes = [(lane >= h * ATTN_HEAD_DIM) & (lane < (h + 1) * ATTN_HEAD_DIM)
                  for h in range(HEADS_PER_SLAB)]

    def gather(ref, starts, rows):
        parts = [ref[pl.ds(pl.multiple_of(st, 16), rows), :] for st in starts]
        return parts[0] if len(parts) == 1 else jnp.concatenate(parts, axis=0)

    plans = []
    for r, n in blocks:
        n_key = jnp.maximum(n - 1, 0)
        if d < FINE:
            c = _chunks(d)
            per = seq // FINE
            bases = [(r + d * b) * per for b in range(c)]
            q_starts = [base + (blk // c) * n for base in bases]
            k_starts = [base + (blk // c) * n_key for base in bases]
            q_refs, k_ref, v_ref = fine
            state = [(st, blk // c, 1) for st in q_starts]
        else:
            c = 1
            per = seq // COARSE
            q_starts = [r * per + blk * n]
            k_starts = [r * per + blk * n_key]
            q_refs, k_ref, v_ref = coarse
            ratio = d // FINE
            state = [((r % FINE) * (seq // FINE) + ratio * blk * n + r // FINE, blk, ratio)]
        plans.append((q_refs, k_ref, v_ref, q_starts, k_starts, c, state, jnp.minimum(n, 1)))

    def state_rows(ref, state):
        parts = []
        for start, rows, stride in state:
            if stride == 1:
                parts.append(ref[pl.ds(pl.multiple_of(start, 16), rows), :])
            else:
                parts.append(ref[pl.ds(start, rows, stride=stride), :])
        return parts[0] if len(parts) == 1 else jnp.concatenate(parts, axis=0)

    for i, (q_refs, k_ref, _, q_starts, k_starts, c, _, _) in enumerate(plans):
        kb = gather(k_ref, k_starts, 2 * blk // c)
        for h in range(HEADS_PER_SLAB):
            qh = gather(q_refs[h], q_starts, blk // c)
            s_ref[i * HEADS_PER_SLAB + h] = lax.dot_general(
                qh, kb, (((1,), (1,)), ((), ())), preferred_element_type=F32)

    pending = []
    for i, plan in enumerate(plans):
        state, table = plan[6], plan[7]
        m_prev = None if init else state_rows(m_ref, state)
        m_heads, l_heads = [], []
        for h in range(HEADS_PER_SLAB):
            t = i * HEADS_PER_SLAB + h
            s = s_ref[t] + bias_ref[table, h]
            top = jnp.maximum(s[:, :LANES], s[:, LANES:])
            if not init:
                top = jnp.maximum(top, jnp.where(head_lanes[h], m_prev, MASK_VALUE))
            m_new = jnp.broadcast_to(jnp.max(top, axis=-1, keepdims=True), (blk, LANES))
            p = jnp.exp2(s - jnp.concatenate([m_new, m_new], axis=1))
            l_heads.append(jnp.broadcast_to(jnp.sum(p, axis=-1, keepdims=True), (blk, LANES)))
            m_heads.append(m_new)
            p_ref[t] = p.astype(BF16)
        m_new = jnp.where(head_lanes[0], m_heads[0], m_heads[1])
        l_new = jnp.where(head_lanes[0], l_heads[0], l_heads[1])
        if not init:
            alpha = jnp.exp2(m_prev - m_new)
            alpha_ref[i] = alpha
            l_new = state_rows(l_ref, state) * alpha + l_new
        pending.append((m_ref, state, m_new))
        pending.append((l_ref, state, l_new))

    for i, plan in enumerate(plans):
        v_ref, k_starts, c, state = plan[2], plan[4], plan[5], plan[6]
        vb = gather(v_ref, k_starts, 2 * blk // c)
        pv = [jnp.dot(p_ref[i * HEADS_PER_SLAB + h], vb, preferred_element_type=F32)
              for h in range(HEADS_PER_SLAB)]
        acc_new = jnp.where(head_lanes[0], pv[0], pv[1])
        if not init:
            acc_new = state_rows(acc_ref, state) * alpha_ref[i] + acc_new
        pending.append((acc_ref, state, acc_new))

    for ref, state, value in pending:
        offset = 0
        for start, rows, stride in state:
            piece = value[offset:offset + rows]
            if stride == 1:
                ref[pl.ds(pl.multiple_of(start, 16), rows), :] = piece
            else:
                ref[pl.ds(start, rows, stride=stride), :] = piece
            offset += rows


def _deinterleave(src_ref, tmp_ref, dst_fine, dst_coarse, *, masks):
    seq = src_ref.shape[0]
    quarter = seq // 4
    piece = DEINTERLEAVE_ROWS

    for r4 in range(4):
        for off in range(0, quarter, piece):
            tmp_ref[r4 * quarter + off:r4 * quarter + off + piece, :] = \
                src_ref[pl.ds(r4 + 4 * off, piece, stride=4), :]

    for layout, dsts in ((FINE, dst_fine), (COARSE, dst_coarse)):
        per = seq // layout
        sub = layout // 4
        for res in range(layout):
            for off in range(0, per, piece):
                value = tmp_ref[pl.ds((res % 4) * quarter + res // 4 + sub * off, piece,
                                      stride=sub), :]
                rows = slice(res * per + off, res * per + off + piece)
                if masks is None:
                    dsts[0][rows, :] = value.astype(BF16)
                else:
                    for dst, mask in zip(dsts, masks):
                        dst[rows, :] = jnp.where(mask, value, 0.0).astype(BF16)


def _attn_kernel(q_ref, k_ref, v_ref, b1_ref, b4_ref, b16_ref, o_ref,
                 qf0, qf1, kf, vf, qc0, qc1, kc, vc, tmp_ref, acc_ref, m_ref, l_ref,
                 s_ref, p_ref, alpha_ref):
    seq = q_ref.shape[0]
    blk = ATTN_BLOCK
    grp = ATTN_GROUP
    lane = lax.broadcasted_iota(jnp.int32, (DEINTERLEAVE_ROWS, LANES), 1)
    masks = [(lane >= h * ATTN_HEAD_DIM) & (lane < (h + 1) * ATTN_HEAD_DIM)
             for h in range(HEADS_PER_SLAB)]
    _deinterleave(q_ref, tmp_ref, (qf0, qf1), (qc0, qc1), masks=masks)
    _deinterleave(k_ref, tmp_ref, (kf,), (kc,), masks=None)
    _deinterleave(v_ref, tmp_ref, (vf,), (vc,), masks=None)

    refs = (((qf0, qf1), kf, vf), ((qc0, qc1), kc, vc), acc_ref, m_ref, l_ref)
    stage = (s_ref, p_ref, alpha_ref)
    patterns = sorted(zip(DILATED_PATTERNS, (b1_ref, b4_ref, b16_ref)), key=lambda e: -e[0][1])
    for idx, ((_, d), bias_ref) in enumerate(patterns):
        nb = seq // (d * blk)
        run = functools.partial(_attn_group, refs, stage, bias_ref, d, init=(idx == 0))
        if d < grp:
            per_res = nb // grp

            def body(i, carry, run=run, per_res=per_res):
                r = i // per_res
                n0 = (i % per_res) * grp
                run([(r, n0 + j) for j in range(grp)])
                return carry

            lax.fori_loop(0, d * per_res, body, 0)
        else:
            per_blk = d // grp

            def body(i, carry, run=run, per_blk=per_blk):
                n = i // per_blk
                r0 = (i % per_blk) * grp
                run([(r0 + j, n) for j in range(grp)])
                return carry

            lax.fori_loop(0, nb * per_blk, body, 0)

    per = seq // FINE

    def out_body(i, carry):
        res = i // (per // OUT_ROWS)
        off = (i % (per // OUT_ROWS)) * OUT_ROWS
        rows = pl.ds(pl.multiple_of(res * per + off, OUT_ROWS), OUT_ROWS)
        tmp_ref[pl.ds(res + FINE * off, OUT_ROWS, stride=FINE), :] = acc_ref[rows, :] / l_ref[rows, :]
        return carry

    lax.fori_loop(0, FINE * (per // OUT_ROWS), out_body, 0)

    def cast_body(i, carry):
        rows = pl.ds(pl.multiple_of(i * OUT_ROWS, OUT_ROWS), OUT_ROWS)
        o_ref[rows, :] = tmp_ref[rows, :].astype(o_ref.dtype)
        return carry

    lax.fori_loop(0, seq // OUT_ROWS, cast_body, 0)


def _attn_bias(window, dilation):
    slopes = np.array([2.0 ** (-8.0 * (i + 1) / ATTN_HEADS) for i in range(ATTN_HEADS)],
                      dtype=np.float32)
    qi = _tile_order(dilation, ATTN_BLOCK)[:, None]
    kj = _tile_order(dilation, 2 * ATTN_BLOCK)[None, :]
    tables = []
    for steps in (qi - kj, qi + ATTN_BLOCK - kj):
        valid = (steps >= 0) & (steps <= window // dilation)
        bias = -slopes[:, None, None] * (steps * dilation).astype(np.float32) * np.float32(LOG2_E)
        tables.append(np.where(valid[None], bias, np.float32(MASK_VALUE)))
    table = np.stack(tables).astype(np.float32)
    table = table.reshape(2, ATTN_HEADS // HEADS_PER_SLAB, HEADS_PER_SLAB, *table.shape[2:])
    return jnp.asarray(table.transpose(1, 0, 2, 3, 4))


def _attention(q, k, v, *, batch, seq):
    slabs = ATTN_WIDTH // LANES
    per_seq = pl.BlockSpec((seq, LANES), lambda bi, j: (bi, j))
    bias = [_attn_bias(w, d) for w, d in DILATED_PATTERNS]
    bias_spec = pl.BlockSpec((None,) + bias[0].shape[1:], lambda bi, j: (j, 0, 0, 0, 0))
    rows16 = pltpu.VMEM((seq, LANES), BF16)
    rows32 = pltpu.VMEM((seq, LANES), F32)
    tiles = ATTN_GROUP * HEADS_PER_SLAB
    return pl.pallas_call(
        _attn_kernel,
        grid=(batch, slabs),
        in_specs=[per_seq, per_seq, per_seq, bias_spec, bias_spec, bias_spec],
        out_specs=per_seq,
        out_shape=jax.ShapeDtypeStruct((batch * seq, ATTN_WIDTH), BF16),
        scratch_shapes=[rows16] * 8 + [rows32] * 4 + [
            pltpu.VMEM((tiles, ATTN_BLOCK, 2 * ATTN_BLOCK), F32),
            pltpu.VMEM((tiles, ATTN_BLOCK, 2 * ATTN_BLOCK), BF16),
            pltpu.VMEM((ATTN_GROUP, ATTN_BLOCK, LANES), F32),
        ],
        compiler_params=_params("arbitrary", "arbitrary"),
        name="attention",
    )(q, k, v, *bias)


def _bmm(a, b):
    return lax.dot_general(a.astype(BF16), b.astype(BF16), (((2,), (1,)), ((0,), (0,))),
                           preferred_element_type=F32)


def _bmm_nt(a, b):
    return lax.dot_general(a.astype(BF16), b.astype(BF16), (((2,), (2,)), ((0,), (0,))),
                           preferred_element_type=F32)


def _unit_lower_inverse(a_strict, row, col):
    c = a_strict.shape[-1]
    eye = (row == col).astype(F32)
    t = None
    size = 1
    while size < c:
        shift = size.bit_length() - 1
        join = ((row >> (shift + 1)) == (col >> (shift + 1))) & (((row >> shift) & 1) == 1) \
            & (((col >> shift) & 1) == 0)
        lk = jnp.where(join, a_strict, 0.0)
        if t is None:
            t = eye - lk
        else:
            t = t - _bmm(t, _bmm(lk, t))
        size *= 2
    return t


def _dn_kernel(q_ref, k_ref, v_ref, gate_ref, small_ref, arow_ref, dtrow_ref, nrm_ref,
               o_ref, s_ref, wq_ref, u_ref, at_ref, kd_ref):
    c = DN_CHUNK
    dh = DN_HEAD_DIM
    batch, tc = q_ref.shape[:2]
    g_chunks = tc // c
    lanes = batch * DN_HEADS
    nb = g_chunks * lanes

    @pl.when(pl.program_id(0) == 0)
    def _():
        s_ref[...] = jnp.zeros(s_ref.shape, F32)

    sm = small_ref[...]
    beta_all = jax.nn.sigmoid(sm)
    z = sm + dtrow_ref[...]
    softplus = jnp.maximum(z, 0.0) + jnp.log1p(jnp.exp(-jnp.abs(z)))
    g_all = -jnp.exp(arow_ref[...]) * softplus

    def per_head(ref):
        parts = [ref[b, :, hd * dh:(hd + 1) * dh].reshape(g_chunks, 1, c, dh)
                 for b in range(batch) for hd in range(DN_HEADS)]
        return jnp.concatenate(parts, axis=1).reshape(nb, c, dh)

    def per_head_col(x, offset):
        parts = [x[b, :, offset + hd:offset + hd + 1].reshape(g_chunks, 1, c, 1)
                 for b in range(batch) for hd in range(DN_HEADS)]
        return jnp.concatenate(parts, axis=1).reshape(nb, c, 1)

    q = per_head(q_ref)
    k = per_head(k_ref)
    v = per_head(v_ref)
    beta = per_head_col(beta_all, 0)
    g = per_head_col(g_all, DN_HEADS)

    row = lax.broadcasted_iota(jnp.int32, (c, c), 0)
    col = lax.broadcasted_iota(jnp.int32, (c, c), 1)
    causal = row >= col
    strict = row > col

    g_mat = jnp.broadcast_to(g, (nb, c, c))
    g_row = jnp.sum(jnp.where(row == col, g_mat, 0.0), axis=1, keepdims=True)
    gc_col = jnp.sum(jnp.where(causal, g_row, 0.0), axis=2, keepdims=True)
    gc_row = jnp.sum(jnp.where(row <= col, g_mat, 0.0), axis=1, keepdims=True)
    decay = jnp.where(causal, jnp.exp(jnp.where(causal, gc_col - gc_row, 0.0)), 0.0)

    k_beta = k * beta
    e_gc = jnp.exp(gc_col)
    g_last = gc_col[:, c - 1:c, :]
    both = _bmm_nt(jnp.concatenate([k_beta, q], axis=1), k)
    a_mat = jnp.where(strict, both[:, :c] * decay, 0.0)
    at_ref[...] = jnp.where(causal, both[:, c:] * decay, 0.0).astype(BF16)
    rhs = jnp.concatenate([v * beta, k_beta * e_gc], axis=-1)
    sol = _bmm(_unit_lower_inverse(a_mat, row, col), rhs)
    u_ref[...] = sol[..., :dh]
    wq_ref[:, :c, :] = sol[..., dh:].astype(BF16)
    wq_ref[:, c:, :] = (q * e_gc).astype(BF16)
    kd_ref[...] = (k * jnp.exp(g_last - gc_col)).astype(BF16)
    e_last = jnp.exp(g_last)

    state = s_ref[...]
    outs = []
    for n in range(g_chunks):
        sl = slice(n * lanes, (n + 1) * lanes)
        ws = _bmm(wq_ref[sl], state)
        v_new = u_ref[sl] - ws[:, :c]
        outs.append(ws[:, c:] + _bmm(at_ref[sl], v_new))
        state = state * e_last[sl] + lax.dot_general(
            kd_ref[sl], v_new.astype(BF16), (((1,), (1,)), ((0,), (0,))),
            preferred_element_type=F32)
    s_ref[...] = state

    for b in range(batch):
        for hd in range(DN_HEADS):
            sl = slice(hd * dh, (hd + 1) * dh)
            o = jnp.concatenate([outs[n][b * DN_HEADS + hd] for n in range(g_chunks)], axis=0)
            o = o * lax.rsqrt(jnp.mean(o * o, axis=-1, keepdims=True) + NORM_EPS)
            o_ref[b, :, sl] = (o * nrm_ref[...] * gate_ref[b, :, sl]).astype(o_ref.dtype)


def _deltanet(dq, dk, dv, gate, small, arow, dtrow, nrm, *, batch, seq):
    tc = DN_TOKENS
    nb = (tc // DN_CHUNK) * batch * DN_HEADS
    view = lambda t: t.reshape(batch, seq, t.shape[-1])
    step = lambda j: (0, j, 0)
    wide = pl.BlockSpec((batch, tc, DN_WIDTH), step)
    out = pl.pallas_call(
        _dn_kernel,
        grid=(seq // tc,),
        in_specs=[wide, wide, wide, wide, pl.BlockSpec((batch, tc, LANES), step),
                  _resident((1, LANES)), _resident((1, LANES)), _resident((1, DN_HEAD_DIM))],
        out_specs=wide,
        out_shape=jax.ShapeDtypeStruct((batch, seq, DN_WIDTH), BF16),
        scratch_shapes=[
            pltpu.VMEM((batch * DN_HEADS, DN_HEAD_DIM, DN_HEAD_DIM), F32),
            pltpu.VMEM((nb, 2 * DN_CHUNK, DN_HEAD_DIM), BF16),
            pltpu.VMEM((nb, DN_CHUNK, DN_HEAD_DIM), F32),
            pltpu.VMEM((nb, DN_CHUNK, DN_CHUNK), BF16),
            pltpu.VMEM((nb, DN_CHUNK, DN_HEAD_DIM), BF16),
        ],
        compiler_params=_params("arbitrary"),
        name="deltanet",
    )(view(dq), view(dk), view(dv), view(gate), view(small), arow, dtrow, nrm)
    return out.reshape(batch * seq, DN_WIDTH)


def _lane_row(values, offset):
    return jnp.zeros((1, LANES), F32).at[0, offset:offset + values.shape[0]].set(values.astype(F32))


def kernel(x, norm_ffn1, ffn1_gate, ffn1_up, ffn1_down, norm_mix, w_in, conv_w, a_log, dt_bias,
           dn_norm, w_out, norm_ffn2, ffn2_gate, ffn2_up, ffn2_down, norm_final):
    batch, seq, d = x.shape
    depth = norm_ffn1.shape[0]
    assert seq % (max(dl for _, dl in DILATED_PATTERNS) * ATTN_BLOCK) == 0
    assert (batch * seq) % FFN_TILE == 0
    assert seq % TOKEN_TILE == 0 and seq % DN_TOKENS == 0 and seq % OUT_ROWS == 0
    xt = x.reshape(batch * seq, d)
    gfin = norm_final.reshape(1, d).astype(F32)
    small_lo = 6 * ATTN_WIDTH
    small_hi = small_lo + 2 * DN_HEADS

    for i in range(depth):
        row = lambda g: g.reshape(1, -1).astype(F32)
        w_main, w_small = _split_w_in(w_in, i, small_lo, small_hi)
        xt = _ffn_in(xt, row(norm_ffn1[i]), _to_bf16(ffn1_gate, i), _to_bf16(ffn1_up, i),
                     _to_bf16(ffn1_down, i))
        aq, ak, av, dq, dk, dv, gate, small = _inproj(
            xt, row(norm_mix[i]), w_main, w_small, conv_w[i].astype(F32), seq=seq)

        attn = _attention(aq, ak, av, batch=batch, seq=seq)
        dn = _deltanet(dq, dk, dv, gate, small, _lane_row(a_log[i], DN_HEADS),
                       _lane_row(dt_bias[i], DN_HEADS), row(dn_norm[i]), batch=batch, seq=seq)

        xt = _ffn_out(xt, attn, dn, _to_bf16(w_out, i), row(norm_ffn2[i]),
                      _to_bf16(ffn2_gate, i), _to_bf16(ffn2_up, i), _to_bf16(ffn2_down, i),
                      gfin, final=(i == depth - 1))

    return xt.reshape(batch, seq, d)
```

```python
import functools

import numpy as np
import jax
import jax.numpy as jnp
from jax import lax
from jax.experimental import pallas as pl
from jax.experimental.pallas import tpu as pltpu

F32 = jnp.float32
BF16 = jnp.bfloat16

ATTN_HEADS = 8
ATTN_HEAD_DIM = 64
ATTN_WIDTH = ATTN_HEADS * ATTN_HEAD_DIM
DILATED_PATTERNS = ((128, 1), (512, 4), (2048, 16))
ATTN_BLOCK = 128
DN_HEADS = 4
DN_HEAD_DIM = 128
DN_WIDTH = DN_HEADS * DN_HEAD_DIM
DN_CHUNK = 64
CONV_WIDTH = 4
NORM_EPS = 1e-6
L2_EPS = 1e-6

LANES = 128
SUBLANES = 8
VMEM_LIMIT_BYTES = 56 * 1024 * 1024

MASK_VALUE = -1e30
LOG2_E = 1.4426950408889634

TOKEN_TILE = 512
FFN_TILE = 1024
FF_TILE = 256
DN_TOKENS = 128
HEADS_PER_SLAB = LANES // ATTN_HEAD_DIM
OUT_ROWS = 512
ATTN_GROUP = 8
DEINTERLEAVE_ROWS = 256


def _rms(x):
    return x * lax.rsqrt(jnp.mean(x * x, axis=-1, keepdims=True) + NORM_EPS)


def _silu(x):
    half = 0.5 * x
    return half + half * jnp.tanh(half)


def _mm(a, b):
    return jnp.dot(a.astype(BF16), b.astype(BF16), preferred_element_type=F32)


def _resident(shape):
    nd = len(shape)
    return pl.BlockSpec(shape, lambda *_: (0,) * nd, pipeline_mode=pl.Buffered(1))


def _params(*sem):
    return pltpu.CompilerParams(dimension_semantics=sem, vmem_limit_bytes=VMEM_LIMIT_BYTES)


def _swiglu_step(x, gain_ref, wg_ref, wu_ref, wd_ref, a_ref):
    h = (_rms(x) * gain_ref[...]).astype(BF16)
    d_ff = wg_ref.shape[1]
    for f0 in range(0, d_ff, FF_TILE):
        g = jnp.dot(h, wg_ref[:, f0:f0 + FF_TILE], preferred_element_type=F32)
        u = jnp.dot(h, wu_ref[:, f0:f0 + FF_TILE], preferred_element_type=F32)
        a_ref[:, f0:f0 + FF_TILE] = (_silu(g) * u).astype(BF16)
    return x + 0.5 * jnp.dot(a_ref[...], wd_ref[...], preferred_element_type=F32)


def _ffn_out_kernel(x_ref, attn_ref, dn_ref, woa_ref, wod_ref, gain_ref, wg_ref, wu_ref, wd_ref,
                    gfin_ref, o_ref, a_ref, *, final):
    x = x_ref[...] + jnp.dot(attn_ref[...], woa_ref[...], preferred_element_type=F32)
    x = x + jnp.dot(dn_ref[...], wod_ref[...], preferred_element_type=F32)
    out = _swiglu_step(x, gain_ref, wg_ref, wu_ref, wd_ref, a_ref)
    if final:
        out = _rms(out) * gfin_ref[...]
    o_ref[...] = out


def _ffn_out(x, attn, dn, wo_attn, wo_dn, gain, wg, wu, wd, gfin, *, final):
    t, d = x.shape
    d_ff = wg.shape[1]
    tm = FFN_TILE
    row = lambda i: (i, 0)
    return pl.pallas_call(
        functools.partial(_ffn_out_kernel, final=final),
        grid=(t // tm,),
        in_specs=[pl.BlockSpec((tm, d), row), pl.BlockSpec((tm, attn.shape[1]), row),
                  pl.BlockSpec((tm, dn.shape[1]), row), _resident(wo_attn.shape),
                  _resident(wo_dn.shape), _resident((1, d)), _resident((d, d_ff)),
                  _resident((d, d_ff)), _resident((d_ff, d)), _resident((1, d))],
        out_specs=pl.BlockSpec((tm, d), row),
        out_shape=jax.ShapeDtypeStruct((t, d), F32),
        scratch_shapes=[pltpu.VMEM((tm, d_ff), BF16)],
        compiler_params=_params("arbitrary"),
        name="ffn_out",
    )(x, attn, dn, wo_attn, wo_dn, gain, wg, wu, wd, gfin)


def _ffn_in_kernel(x_ref, gain_ref, wg_ref, wu_ref, wd_ref, o_ref, a_ref):
    o_ref[...] = _swiglu_step(x_ref[...], gain_ref, wg_ref, wu_ref, wd_ref, a_ref)


def _ffn_in(x, gain, wg, wu, wd):
    t, d = x.shape
    d_ff = wg.shape[1]
    tm = FFN_TILE
    row = lambda i: (i, 0)
    return pl.pallas_call(
        _ffn_in_kernel,
        grid=(t // tm,),
        in_specs=[pl.BlockSpec((tm, d), row), _resident((1, d)), _resident((d, d_ff)),
                  _resident((d, d_ff)), _resident((d_ff, d))],
        out_specs=pl.BlockSpec((tm, d), row),
        out_shape=jax.ShapeDtypeStruct((t, d), F32),
        scratch_shapes=[pltpu.VMEM((tm, d_ff), BF16)],
        compiler_params=_params("arbitrary"),
        name="ffn_in",
    )(x, gain, wg, wu, wd)


def _inproj_kernel(x_ref, gain_ref, wm_ref, ws_ref, cw_ref,
                   aq_ref, ak_ref, av_ref, dq_ref, dk_ref, dv_ref, gate_ref, small_ref,
                   xc_ref, *, tiles_per_seq):
    i = pl.program_id(0)
    tm = x_ref.shape[0]
    halo = SUBLANES

    @pl.when(i % tiles_per_seq == 0)
    def _():
        xc_ref[0:halo, :] = jnp.zeros((halo, xc_ref.shape[1]), F32)

    @pl.when(i % tiles_per_seq != 0)
    def _():
        xc_ref[0:halo, :] = xc_ref[tm:tm + halo, :]

    h = (_rms(x_ref[...]) * gain_ref[...]).astype(BF16)

    def section(j, width=ATTN_WIDTH):
        return jnp.dot(h, wm_ref[:, j * width:(j + 1) * width], preferred_element_type=F32)

    for j in range(3):
        xc_ref[halo:halo + tm, j * DN_WIDTH:(j + 1) * DN_WIDTH] = section(3 + j)

    outs = (dq_ref, dk_ref, dv_ref)
    sub = lax.broadcasted_iota(jnp.int32, (halo, DN_HEAD_DIM), 0)
    for c in range(3 * DN_HEADS):
        sl = slice(c * DN_HEAD_DIM, (c + 1) * DN_HEAD_DIM)
        cur = xc_ref[halo:halo + tm, sl]
        tail = xc_ref[0:halo, sl]
        conv = None
        for j in range(CONV_WIDTH):
            back = CONV_WIDTH - 1 - j
            if back:
                shifted = pltpu.roll(cur, back, 0)
                head = jnp.where(sub < back, pltpu.roll(tail, back, 0), shifted[0:halo])
                shifted = jnp.concatenate([head, shifted[halo:]], axis=0)
            else:
                shifted = cur
            term = shifted * cw_ref[j:j + 1, sl]
            conv = term if conv is None else conv + term
        a = _silu(conv)
        if c < 2 * DN_HEADS:
            a = a * lax.rsqrt(jnp.sum(a * a, axis=-1, keepdims=True) + L2_EPS)
        if c < DN_HEADS:
            a = a * (DN_HEAD_DIM ** -0.5)
        hs = slice((c % DN_HEADS) * DN_HEAD_DIM, (c % DN_HEADS + 1) * DN_HEAD_DIM)
        outs[c // DN_HEADS][:, hs] = a

    aq_ref[...] = section(0) * (ATTN_HEAD_DIM ** -0.5 * LOG2_E)
    ak_ref[...] = section(1)
    av_ref[...] = section(2)
    gate_ref[...] = _silu(section(6))
    small_ref[...] = jnp.dot(h, ws_ref[...], preferred_element_type=F32)


def _inproj(x, gain, w_main, w_small, conv_w, *, seq):
    t, d = x.shape
    tm = TOKEN_TILE
    row = lambda i: (i, 0)
    wide = jax.ShapeDtypeStruct((t, ATTN_WIDTH), F32)
    return pl.pallas_call(
        functools.partial(_inproj_kernel, tiles_per_seq=seq // tm),
        grid=(t // tm,),
        in_specs=[pl.BlockSpec((tm, d), row), _resident((1, d)), _resident(w_main.shape),
                  _resident(w_small.shape), _resident(conv_w.shape)],
        out_specs=[pl.BlockSpec((tm, ATTN_WIDTH), row)] * 7 + [pl.BlockSpec((tm, LANES), row)],
        out_shape=[wide] * 7 + [jax.ShapeDtypeStruct((t, LANES), F32)],
        scratch_shapes=[pltpu.VMEM((tm + 2 * SUBLANES, 3 * DN_WIDTH), F32)],
        compiler_params=_params("arbitrary"),
        name="inproj",
    )(x, gain, w_main, w_small, conv_w)


FINE = 8
COARSE = 16


def _chunks(dilation):
    return FINE // dilation


def _tile_order(dilation, size):
    c = _chunks(dilation) if dilation < FINE else 1
    per = size // c
    x = np.arange(size)
    return c * (x % per) + x // per


def _attn_group(refs, stage, bias_ref, dilation, blocks, *, init):
    fine, coarse, acc_ref, m_ref, l_ref = refs
    s_ref, p_ref, alpha_ref = stage
    d = dilation
    blk = ATTN_BLOCK
    seq = acc_ref.shape[0]
    lane = lax.broadcasted_iota(jnp.int32, (blk, LANES), 1)
    head_lanes = [(lane >= h * ATTN_HEAD_DIM) & (lane < (h + 1) * ATTN_HEAD_DIM)
                  for h in range(HEADS_PER_SLAB)]

    def gather(ref, starts, rows):
        parts = [ref[pl.ds(pl.multiple_of(st, 16), rows), :] for st in starts]
        return parts[0] if len(parts) == 1 else jnp.concatenate(parts, axis=0)

    plans = []
    for r, n in blocks:
        n_key = jnp.maximum(n - 1, 0)
        if d < FINE:
            c = _chunks(d)
            per = seq // FINE
            bases = [(r + d * b) * per for b in range(c)]
            q_starts = [base + (blk // c) * n for base in bases]
            k_starts = [base + (blk // c) * n_key for base in bases]
            q_refs, k_ref, v_ref = fine
            state = [(st, blk // c, 1) for st in q_starts]
        else:
            c = 1
            per = seq // COARSE
            q_starts = [r * per + blk * n]
            k_starts = [r * per + blk * n_key]
            q_refs, k_ref, v_ref = coarse
            ratio = d // FINE
            state = [((r % FINE) * (seq // FINE) + ratio * blk * n + r // FINE, blk, ratio)]
        plans.append((q_refs, k_ref, v_ref, q_starts, k_starts, c, state, jnp.minimum(n, 1)))

    def state_rows(ref, state):
        parts = []
        for start, rows, stride in state:
            if stride == 1:
                parts.append(ref[pl.ds(pl.multiple_of(start, 16), rows), :])
            else:
                parts.append(ref[pl.ds(start, rows, stride=stride), :])
        return parts[0] if len(parts) == 1 else jnp.concatenate(parts, axis=0)

    for i, (q_refs, k_ref, _, q_starts, k_starts, c, _, _) in enumerate(plans):
        kb = gather(k_ref, k_starts, 2 * blk // c)
        for h in range(HEADS_PER_SLAB):
            qh = gather(q_refs[h], q_starts, blk // c)
            s_ref[i * HEADS_PER_SLAB + h] = lax.dot_general(
                qh, kb, (((1,), (1,)), ((), ())), preferred_element_type=F32)

    pending = []
    for i, plan in enumerate(plans):
        state, table = plan[6], plan[7]
        m_prev = None if init else state_rows(m_ref, state)
        m_heads, l_heads = [], []
        for h in range(HEADS_PER_SLAB):
            t = i * HEADS_PER_SLAB + h
            s = s_ref[t] + bias_ref[table, h]
            top = jnp.maximum(s[:, :LANES], s[:, LANES:])
            if not init:
                top = jnp.maximum(top, jnp.where(head_lanes[h], m_prev, MASK_VALUE))
            m_new = jnp.broadcast_to(jnp.max(top, axis=-1, keepdims=True), (blk, LANES))
            p = jnp.exp2(s - jnp.concatenate([m_new, m_new], axis=1))
            l_heads.append(jnp.broadcast_to(jnp.sum(p, axis=-1, keepdims=True), (blk, LANES)))
            m_heads.append(m_new)
            p_ref[t] = p.astype(BF16)
        m_new = jnp.where(head_lanes[0], m_heads[0], m_heads[1])
        l_new = jnp.where(head_lanes[0], l_heads[0], l_heads[1])
        if not init:
            alpha = jnp.exp2(m_prev - m_new)
            alpha_ref[i] = alpha
            l_new = state_rows(l_ref, state) * alpha + l_new
        pending.append((m_ref, state, m_new))
        pending.append((l_ref, state, l_new))

    for i, plan in enumerate(plans):
        v_ref, k_starts, c, state = plan[2], plan[4], plan[5], plan[6]
        vb = gather(v_ref, k_starts, 2 * blk // c)
        pv = [jnp.dot(p_ref[i * HEADS_PER_SLAB + h], vb, preferred_element_type=F32)
              for h in range(HEADS_PER_SLAB)]
        acc_new = jnp.where(head_lanes[0], pv[0], pv[1])
        if not init:
            acc_new = state_rows(acc_ref, state) * alpha_ref[i] + acc_new
        pending.append((acc_ref, state, acc_new))

    for ref, state, value in pending:
        offset = 0
        for start, rows, stride in state:
            piece = value[offset:offset + rows]
            if stride == 1:
                ref[pl.ds(pl.multiple_of(start, 16), rows), :] = piece
            else:
                ref[pl.ds(start, rows, stride=stride), :] = piece
            offset += rows


def _deinterleave(src_ref, tmp_ref, dst_fine, dst_coarse, *, masks):
    seq = src_ref.shape[0]
    quarter = seq // 4
    piece = DEINTERLEAVE_ROWS

    for r4 in range(4):
        for off in range(0, quarter, piece):
            tmp_ref[r4 * quarter + off:r4 * quarter + off + piece, :] = \
                src_ref[pl.ds(r4 + 4 * off, piece, stride=4), :]

    for layout, dsts in ((FINE, dst_fine), (COARSE, dst_coarse)):
        per = seq // layout
        sub = layout // 4
        for res in range(layout):
            for off in range(0, per, piece):
                value = tmp_ref[pl.ds((res % 4) * quarter + res // 4 + sub * off, piece,
                                      stride=sub), :]
                rows = slice(res * per + off, res * per + off + piece)
                if masks is None:
                    dsts[0][rows, :] = value.astype(BF16)
                else:
                    for dst, mask in zip(dsts, masks):
                        dst[rows, :] = jnp.where(mask, value, 0.0).astype(BF16)


def _attn_kernel(q_ref, k_ref, v_ref, b1_ref, b4_ref, b16_ref, o_ref,
                 qf0, qf1, kf, vf, qc0, qc1, kc, vc, tmp_ref, acc_ref, m_ref, l_ref,
                 s_ref, p_ref, alpha_ref):
    seq = q_ref.shape[0]
    blk = ATTN_BLOCK
    grp = ATTN_GROUP
    lane = lax.broadcasted_iota(jnp.int32, (DEINTERLEAVE_ROWS, LANES), 1)
    masks = [(lane >= h * ATTN_HEAD_DIM) & (lane < (h + 1) * ATTN_HEAD_DIM)
             for h in range(HEADS_PER_SLAB)]
    _deinterleave(q_ref, tmp_ref, (qf0, qf1), (qc0, qc1), masks=masks)
    _deinterleave(k_ref, tmp_ref, (kf,), (kc,), masks=None)
    _deinterleave(v_ref, tmp_ref, (vf,), (vc,), masks=None)

    refs = (((qf0, qf1), kf, vf), ((qc0, qc1), kc, vc), acc_ref, m_ref, l_ref)
    stage = (s_ref, p_ref, alpha_ref)
    patterns = sorted(zip(DILATED_PATTERNS, (b1_ref, b4_ref, b16_ref)), key=lambda e: -e[0][1])
    for idx, ((_, d), bias_ref) in enumerate(patterns):
        nb = seq // (d * blk)
        run = functools.partial(_attn_group, refs, stage, bias_ref, d, init=(idx == 0))
        if d < grp:
            per_res = nb // grp

            def body(i, carry, run=run, per_res=per_res):
                r = i // per_res
                n0 = (i % per_res) * grp
                run([(r, n0 + j) for j in range(grp)])
                return carry

            lax.fori_loop(0, d * per_res, body, 0)
        else:
            per_blk = d // grp

            def body(i, carry, run=run, per_blk=per_blk):
                n = i // per_blk
                r0 = (i % per_blk) * grp
                run([(r0 + j, n) for j in range(grp)])
                return carry

            lax.fori_loop(0, nb * per_blk, body, 0)

    per = seq // FINE

    def out_body(i, carry):
        res = i // (per // OUT_ROWS)
        off = (i % (per // OUT_ROWS)) * OUT_ROWS
        rows = pl.ds(pl.multiple_of(res * per + off, OUT_ROWS), OUT_ROWS)
        tmp_ref[pl.ds(res + FINE * off, OUT_ROWS, stride=FINE), :] = acc_ref[rows, :] / l_ref[rows, :]
        return carry

    lax.fori_loop(0, FINE * (per // OUT_ROWS), out_body, 0)

    def cast_body(i, carry):
        rows = pl.ds(pl.multiple_of(i * OUT_ROWS, OUT_ROWS), OUT_ROWS)
        o_ref[rows, :] = tmp_ref[rows, :].astype(o_ref.dtype)
        return carry

    lax.fori_loop(0, seq // OUT_ROWS, cast_body, 0)


def _attn_bias(window, dilation):
    slopes = np.array([2.0 ** (-8.0 * (i + 1) / ATTN_HEADS) for i in range(ATTN_HEADS)],
                      dtype=np.float32)
    qi = _tile_order(dilation, ATTN_BLOCK)[:, None]
    kj = _tile_order(dilation, 2 * ATTN_BLOCK)[None, :]
    tables = []
    for steps in (qi - kj, qi + ATTN_BLOCK - kj):
        valid = (steps >= 0) & (steps <= window // dilation)
        bias = -slopes[:, None, None] * (steps * dilation).astype(np.float32) * np.float32(LOG2_E)
        tables.append(np.where(valid[None], bias, np.float32(MASK_VALUE)))
    table = np.stack(tables).astype(np.float32)
    table = table.reshape(2, ATTN_HEADS // HEADS_PER_SLAB, HEADS_PER_SLAB, *table.shape[2:])
    return jnp.asarray(table.transpose(1, 0, 2, 3, 4))


def _attention(q, k, v, *, batch, seq):
    slabs = ATTN_WIDTH // LANES
    per_seq = pl.BlockSpec((seq, LANES), lambda bi, j: (bi, j))
    bias = [_attn_bias(w, d) for w, d in DILATED_PATTERNS]
    bias_spec = pl.BlockSpec((None,) + bias[0].shape[1:], lambda bi, j: (j, 0, 0, 0, 0))
    rows16 = pltpu.VMEM((seq, LANES), BF16)
    rows32 = pltpu.VMEM((seq, LANES), F32)
    tiles = ATTN_GROUP * HEADS_PER_SLAB
    return pl.pallas_call(
        _attn_kernel,
        grid=(batch, slabs),
        in_specs=[per_seq, per_seq, per_seq, bias_spec, bias_spec, bias_spec],
        out_specs=per_seq,
        out_shape=jax.ShapeDtypeStruct((batch * seq, ATTN_WIDTH), BF16),
        scratch_shapes=[rows16] * 8 + [rows32] * 4 + [
            pltpu.VMEM((tiles, ATTN_BLOCK, 2 * ATTN_BLOCK), F32),
            pltpu.VMEM((tiles, ATTN_BLOCK, 2 * ATTN_BLOCK), BF16),
            pltpu.VMEM((ATTN_GROUP, ATTN_BLOCK, LANES), F32),
        ],
        compiler_params=_params("arbitrary", "arbitrary"),
        name="attention",
    )(q, k, v, *bias)


def _bmm(a, b):
    return lax.dot_general(a.astype(BF16), b.astype(BF16), (((2,), (1,)), ((0,), (0,))),
                           preferred_element_type=F32)


def _bmm_nt(a, b):
    return lax.dot_general(a.astype(BF16), b.astype(BF16), (((2,), (2,)), ((0,), (0,))),
                           preferred_element_type=F32)


MASK_EYE, MASK_CAUSAL, MASK_STRICT, MASK_UPPER, MASK_JOIN = 0, 1, 2, 3, 4


def _dn_masks():
    c = DN_CHUNK
    row, col = np.indices((c, c))
    tables = [row == col, row >= col, row > col, row <= col]
    size = 1
    while size < c:
        shift = size.bit_length() - 1
        tables.append(((row >> (shift + 1)) == (col >> (shift + 1)))
                      & (((row >> shift) & 1) == 1) & (((col >> shift) & 1) == 0))
        size *= 2
    return jnp.asarray(np.stack(tables).astype(np.float32))


def _dn_spread(offset):
    e = np.zeros((LANES, DN_WIDTH), np.float32)
    for hd in range(DN_HEADS):
        e[offset + hd, hd * DN_HEAD_DIM:(hd + 1) * DN_HEAD_DIM] = 1.0
    return jnp.asarray(e, dtype=BF16)


def _unit_lower_inverse(a_strict, mask_ref):
    levels = mask_ref.shape[0] - MASK_JOIN
    t = mask_ref[MASK_EYE] - a_strict * mask_ref[MASK_JOIN]
    for level in range(1, levels):
        lk = a_strict * mask_ref[MASK_JOIN + level]
        t = t - _bmm(t, _bmm(lk, t))
    return t


def _dn_kernel(q_ref, k_ref, v_ref, gate_ref, small_ref, arow_ref, dtrow_ref, nrm_ref, mask_ref,
               ebeta_ref, eg_ref, o_ref, s_ref, wq_ref, u_ref, at_ref, kd_ref):
    c = DN_CHUNK
    dh = DN_HEAD_DIM
    batch, tc = q_ref.shape[:2]
    g_chunks = tc // c
    lanes = batch * DN_HEADS
    nb = g_chunks * lanes

    @pl.when(pl.program_id(0) == 0)
    def _():
        s_ref[...] = jnp.zeros(s_ref.shape, F32)

    sm = small_ref[...]
    beta_all = jax.nn.sigmoid(sm)
    z = sm + dtrow_ref[...]
    softplus = jnp.maximum(z, 0.0) + jnp.log1p(jnp.exp(-jnp.abs(z)))
    g_all = -jnp.exp(arow_ref[...]) * softplus

    def per_head(ref):
        parts = [ref[b, :, hd * dh:(hd + 1) * dh].reshape(g_chunks, 1, c, dh)
                 for b in range(batch) for hd in range(DN_HEADS)]
        return jnp.concatenate(parts, axis=1).reshape(nb, c, dh)

    def split3(x):
        hi = x.astype(BF16)
        rest = x - hi.astype(F32)
        mid = rest.astype(BF16)
        return hi, mid, (rest - mid.astype(F32)).astype(BF16)

    def spread(x, e_ref):
        flat = x.reshape(batch * tc, LANES)
        wide = sum(jnp.dot(part, e_ref[...], preferred_element_type=F32) for part in split3(flat))
        parts = [wide[b * tc:(b + 1) * tc, hd * dh:(hd + 1) * dh].reshape(g_chunks, 1, c, dh)
                 for b in range(batch) for hd in range(DN_HEADS)]
        return jnp.concatenate(parts, axis=1).reshape(nb, c, dh)

    q = per_head(q_ref)
    k = per_head(k_ref)
    v = per_head(v_ref)

    causal = mask_ref[MASK_CAUSAL]

    chunks = batch * g_chunks
    tri = jnp.broadcast_to(causal.astype(BF16), (chunks, c, c))
    gc_all = sum(_bmm(tri, part) for part in split3(g_all.reshape(chunks, c, LANES)))
    beta = spread(beta_all, ebeta_ref)
    gc = spread(gc_all.reshape(batch, tc, LANES), eg_ref)
    gc_mat = gc[..., :c]
    gc_row = jnp.sum(gc_mat * mask_ref[MASK_EYE], axis=1, keepdims=True)
    decay = jnp.exp((gc_mat - gc_row) * causal) * causal

    k_beta = k * beta
    e_gc = jnp.exp(gc)
    g_last = gc[:, c - 1:c, :]
    both = _bmm_nt(jnp.concatenate([k_beta, q], axis=1), k)
    a_mat = both[:, :c] * (decay * mask_ref[MASK_STRICT])
    at_ref[...] = (both[:, c:] * decay).astype(BF16)
    rhs = jnp.concatenate([v * beta, k_beta * e_gc], axis=-1)
    sol = _bmm(_unit_lower_inverse(a_mat, mask_ref), rhs)
    u_ref[...] = sol[..., :dh]
    wq_ref[:, :c, :] = sol[..., dh:].astype(BF16)
    wq_ref[:, c:, :] = (q * e_gc).astype(BF16)
    kd_ref[...] = (k * jnp.exp(g_last - gc)).astype(BF16)
    e_last = jnp.exp(g_last)

    state = s_ref[...]
    outs = []
    for n in range(g_chunks):
        sl = slice(n * lanes, (n + 1) * lanes)
        ws = _bmm(wq_ref[sl], state)
        v_new = u_ref[sl] - ws[:, :c]
        outs.append(ws[:, c:] + _bmm(at_ref[sl], v_new))
        state = state * e_last[sl] + lax.dot_general(
            kd_ref[sl], v_new.astype(BF16), (((1,), (1,)), ((0,), (0,))),
            preferred_element_type=F32)
    s_ref[...] = state

    for b in range(batch):
        for hd in range(DN_HEADS):
            sl = slice(hd * dh, (hd + 1) * dh)
            o = jnp.concatenate([outs[n][b * DN_HEADS + hd] for n in range(g_chunks)], axis=0)
            o = o * lax.rsqrt(jnp.mean(o * o, axis=-1, keepdims=True) + NORM_EPS)
            o_ref[b, :, sl] = (o * nrm_ref[...] * gate_ref[b, :, sl]).astype(o_ref.dtype)


def _deltanet(dq, dk, dv, gate, small, arow, dtrow, nrm, *, batch, seq):
    tc = DN_TOKENS
    nb = (tc // DN_CHUNK) * batch * DN_HEADS
    masks = _dn_masks()
    spread_beta = _dn_spread(0)
    spread_g = _dn_spread(DN_HEADS)
    view = lambda t: t.reshape(batch, seq, t.shape[-1])
    step = lambda j: (0, j, 0)
    wide = pl.BlockSpec((batch, tc, DN_WIDTH), step)
    out = pl.pallas_call(
        _dn_kernel,
        grid=(seq // tc,),
        in_specs=[wide, wide, wide, wide, pl.BlockSpec((batch, tc, LANES), step),
                  _resident((1, LANES)), _resident((1, LANES)), _resident((1, DN_HEAD_DIM)),
                  _resident(masks.shape), _resident(spread_beta.shape),
                  _resident(spread_g.shape)],
        out_specs=wide,
        out_shape=jax.ShapeDtypeStruct((batch, seq, DN_WIDTH), BF16),
        scratch_shapes=[
            pltpu.VMEM((batch * DN_HEADS, DN_HEAD_DIM, DN_HEAD_DIM), F32),
            pltpu.VMEM((nb, 2 * DN_CHUNK, DN_HEAD_DIM), BF16),
            pltpu.VMEM((nb, DN_CHUNK, DN_HEAD_DIM), F32),
            pltpu.VMEM((nb, DN_CHUNK, DN_CHUNK), BF16),
            pltpu.VMEM((nb, DN_CHUNK, DN_HEAD_DIM), BF16),
        ],
        compiler_params=_params("arbitrary"),
        name="deltanet",
    )(view(dq), view(dk), view(dv), view(gate), view(small), arow, dtrow, nrm, masks,
      spread_beta, spread_g)
    return out.reshape(batch * seq, DN_WIDTH)


def _lane_row(values, offset):
    return jnp.zeros((1, LANES), F32).at[0, offset:offset + values.shape[0]].set(values.astype(F32))


def kernel(x, norm_ffn1, ffn1_gate, ffn1_up, ffn1_down, norm_mix, w_in, conv_w, a_log, dt_bias,
           dn_norm, w_out, norm_ffn2, ffn2_gate, ffn2_up, ffn2_down, norm_final):
    batch, seq, d = x.shape
    depth = norm_ffn1.shape[0]
    assert seq % (max(dl for _, dl in DILATED_PATTERNS) * ATTN_BLOCK) == 0
    assert (batch * seq) % FFN_TILE == 0
    assert seq % TOKEN_TILE == 0 and seq % DN_TOKENS == 0 and seq % OUT_ROWS == 0
    xt = x.reshape(batch * seq, d)
    gfin = norm_final.reshape(1, d).astype(F32)
    small_lo = 6 * ATTN_WIDTH
    small_hi = small_lo + 2 * DN_HEADS

    for i in range(depth):
        row = lambda g: g.reshape(1, -1).astype(F32)
        wi = w_in[i]
        w_main = jnp.concatenate([wi[:, :small_lo], wi[:, small_hi:]], axis=1).astype(BF16)
        w_small = jnp.pad(wi[:, small_lo:small_hi], ((0, 0), (0, LANES - 2 * DN_HEADS))).astype(BF16)
        xt = _ffn_in(xt, row(norm_ffn1[i]), ffn1_gate[i].astype(BF16), ffn1_up[i].astype(BF16),
                     ffn1_down[i].astype(BF16))
        aq, ak, av, dq, dk, dv, gate, small = _inproj(
            xt, row(norm_mix[i]), w_main, w_small, conv_w[i].astype(F32), seq=seq)

        attn = _attention(aq, ak, av, batch=batch, seq=seq)
        dn = _deltanet(dq, dk, dv, gate, small, _lane_row(a_log[i], DN_HEADS),
                       _lane_row(dt_bias[i], DN_HEADS), row(dn_norm[i]), batch=batch, seq=seq)

        wo = w_out[i].astype(BF16)
        xt = _ffn_out(xt, attn, dn, wo[:ATTN_WIDTH], wo[ATTN_WIDTH:], row(norm_ffn2[i]),
                      ffn2_gate[i].astype(BF16), ffn2_up[i].astype(BF16),
                      ffn2_down[i].astype(BF16), gfin, final=(i == depth - 1))

    return xt.reshape(batch, seq, d)
```

```python
import functools

import numpy as np
import jax
import jax.numpy as jnp
from jax import lax
from jax.experimental import pallas as pl
from jax.experimental.pallas import tpu as pltpu

F32 = jnp.float32
BF16 = jnp.bfloat16

ATTN_HEADS = 8
ATTN_HEAD_DIM = 64
ATTN_WIDTH = ATTN_HEADS * ATTN_HEAD_DIM
DILATED_PATTERNS = ((128, 1), (512, 4), (2048, 16))
ATTN_BLOCK = 128
DN_HEADS = 4
DN_HEAD_DIM = 128
DN_WIDTH = DN_HEADS * DN_HEAD_DIM
DN_CHUNK = 64
CONV_WIDTH = 4
NORM_EPS = 1e-6
L2_EPS = 1e-6

LANES = 128
SUBLANES = 8
VMEM_LIMIT_BYTES = 56 * 1024 * 1024

MASK_VALUE = -1e30
LOG2_E = 1.4426950408889634

TOKEN_TILE = 512
FFN_TILE = 1024
FF_TILE = 256
DN_TOKENS = 128
HEADS_PER_SLAB = LANES // ATTN_HEAD_DIM
OUT_ROWS = 512
ATTN_GROUP = 8
DEINTERLEAVE_ROWS = 256
WEIGHT_ROWS = 128


def _rms(x):
    return x * lax.rsqrt(jnp.mean(x * x, axis=-1, keepdims=True) + NORM_EPS)


def _silu(x):
    half = 0.5 * x
    return half + half * jnp.tanh(half)


def _mm(a, b):
    return jnp.dot(a.astype(BF16), b.astype(BF16), preferred_element_type=F32)


def _resident(shape):
    nd = len(shape)
    return pl.BlockSpec(shape, lambda *_: (0,) * nd, pipeline_mode=pl.Buffered(1))


def _params(*sem):
    return pltpu.CompilerParams(dimension_semantics=sem, vmem_limit_bytes=VMEM_LIMIT_BYTES)


def _load_bf16(src_ref, dst_ref, stage_ref, sem_ref):
    rows = stage_ref.shape[1]
    chunks = src_ref.shape[0] // rows
    assert src_ref.shape[0] % (2 * rows) == 0 and src_ref.shape[1] == stage_ref.shape[2]

    def copy(c, slot):
        return pltpu.make_async_copy(src_ref.at[pl.ds(c * rows, rows), :], stage_ref.at[slot],
                                     sem_ref.at[slot])

    copy(0, 0).start()

    def pair(j, carry):
        for slot in range(2):
            c = 2 * j + slot

            @pl.when(c + 1 < chunks)
            def _():
                copy(c + 1, 1 - slot).start()

            copy(c, slot).wait()
            dst_ref[pl.ds(pl.multiple_of(c * rows, rows), rows), :] = stage_ref[slot].astype(BF16)
        return carry

    lax.fori_loop(0, chunks // 2, pair, 0)


def _any():
    return pl.BlockSpec(memory_space=pl.ANY)


def _swiglu_step(x, gain_ref, wg_ref, wu_ref, wd_ref, a_ref):
    h = (_rms(x) * gain_ref[...]).astype(BF16)
    d_ff = wg_ref.shape[1]
    for f0 in range(0, d_ff, FF_TILE):
        g = jnp.dot(h, wg_ref[:, f0:f0 + FF_TILE], preferred_element_type=F32)
        u = jnp.dot(h, wu_ref[:, f0:f0 + FF_TILE], preferred_element_type=F32)
        a_ref[:, f0:f0 + FF_TILE] = (_silu(g) * u).astype(BF16)
    return x + 0.5 * jnp.dot(a_ref[...], wd_ref[...], preferred_element_type=F32)


def _swiglu_scratch(tm, d, d_ff):
    return [pltpu.VMEM((tm, d_ff), BF16),
            pltpu.VMEM((d, d_ff), BF16), pltpu.VMEM((d, d_ff), BF16), pltpu.VMEM((d_ff, d), BF16),
            pltpu.VMEM((2, WEIGHT_ROWS, d_ff), F32), pltpu.VMEM((2, WEIGHT_ROWS, d), F32),
            pltpu.SemaphoreType.DMA((2,))]


def _load_swiglu_weights(wg_hbm, wu_hbm, wd_hbm, wg_ref, wu_ref, wd_ref, wide_ref, narrow_ref, sem_ref):
    _load_bf16(wg_hbm, wg_ref, wide_ref, sem_ref)
    _load_bf16(wu_hbm, wu_ref, wide_ref, sem_ref)
    _load_bf16(wd_hbm, wd_ref, narrow_ref, sem_ref)


def _ffn_out_kernel(x_ref, attn_ref, dn_ref, wo_hbm, gain_ref, wg_hbm, wu_hbm, wd_hbm, gfin_ref,
                    o_ref, a_ref, wg_ref, wu_ref, wd_ref, wide_ref, narrow_ref, sem_ref, wo_ref,
                    *, final):
    @pl.when(pl.program_id(0) == 0)
    def _():
        _load_bf16(wo_hbm, wo_ref, narrow_ref, sem_ref)
        _load_swiglu_weights(wg_hbm, wu_hbm, wd_hbm, wg_ref, wu_ref, wd_ref, wide_ref, narrow_ref,
                             sem_ref)

    width = attn_ref.shape[1]
    x = x_ref[...] + jnp.dot(attn_ref[...], wo_ref[:width, :], preferred_element_type=F32)
    x = x + jnp.dot(dn_ref[...], wo_ref[width:, :], preferred_element_type=F32)
    out = _swiglu_step(x, gain_ref, wg_ref, wu_ref, wd_ref, a_ref)
    if final:
        out = _rms(out) * gfin_ref[...]
    o_ref[...] = out


def _ffn_out(x, attn, dn, w_out, gain, wg, wu, wd, gfin, *, final):
    t, d = x.shape
    d_ff = wg.shape[1]
    tm = FFN_TILE
    row = lambda i: (i, 0)
    assert w_out.shape == (attn.shape[1] + dn.shape[1], d)
    return pl.pallas_call(
        functools.partial(_ffn_out_kernel, final=final),
        grid=(t // tm,),
        in_specs=[pl.BlockSpec((tm, d), row), pl.BlockSpec((tm, attn.shape[1]), row),
                  pl.BlockSpec((tm, dn.shape[1]), row), _any(), _resident((1, d)),
                  _any(), _any(), _any(), _resident((1, d))],
        out_specs=pl.BlockSpec((tm, d), row),
        out_shape=jax.ShapeDtypeStruct((t, d), F32),
        scratch_shapes=_swiglu_scratch(tm, d, d_ff) + [pltpu.VMEM(w_out.shape, BF16)],
        compiler_params=_params("arbitrary"),
        name="ffn_out",
    )(x, attn, dn, w_out, gain, wg, wu, wd, gfin)


def _ffn_in_kernel(x_ref, gain_ref, wg_hbm, wu_hbm, wd_hbm, o_ref,
                   a_ref, wg_ref, wu_ref, wd_ref, wide_ref, narrow_ref, sem_ref):
    @pl.when(pl.program_id(0) == 0)
    def _():
        _load_swiglu_weights(wg_hbm, wu_hbm, wd_hbm, wg_ref, wu_ref, wd_ref, wide_ref, narrow_ref,
                             sem_ref)

    o_ref[...] = _swiglu_step(x_ref[...], gain_ref, wg_ref, wu_ref, wd_ref, a_ref)


def _ffn_in(x, gain, wg, wu, wd):
    t, d = x.shape
    d_ff = wg.shape[1]
    tm = FFN_TILE
    row = lambda i: (i, 0)
    return pl.pallas_call(
        _ffn_in_kernel,
        grid=(t // tm,),
        in_specs=[pl.BlockSpec((tm, d), row), _resident((1, d)), _any(), _any(), _any()],
        out_specs=pl.BlockSpec((tm, d), row),
        out_shape=jax.ShapeDtypeStruct((t, d), F32),
        scratch_shapes=_swiglu_scratch(tm, d, d_ff),
        compiler_params=_params("arbitrary"),
        name="ffn_in",
    )(x, gain, wg, wu, wd)


def _inproj_kernel(x_ref, gain_ref, wm_ref, ws_ref, cw_ref,
                   aq_ref, ak_ref, av_ref, dq_ref, dk_ref, dv_ref, gate_ref, small_ref,
                   xc_ref, *, tiles_per_seq):
    i = pl.program_id(0)
    tm = x_ref.shape[0]
    halo = SUBLANES

    @pl.when(i % tiles_per_seq == 0)
    def _():
        xc_ref[0:halo, :] = jnp.zeros((halo, xc_ref.shape[1]), F32)

    @pl.when(i % tiles_per_seq != 0)
    def _():
        xc_ref[0:halo, :] = xc_ref[tm:tm + halo, :]

    h = (_rms(x_ref[...]) * gain_ref[...]).astype(BF16)

    def section(j, width=ATTN_WIDTH):
        return jnp.dot(h, wm_ref[:, j * width:(j + 1) * width], preferred_element_type=F32)

    for j in range(3):
        xc_ref[halo:halo + tm, j * DN_WIDTH:(j + 1) * DN_WIDTH] = section(3 + j)

    outs = (dq_ref, dk_ref, dv_ref)
    sub = lax.broadcasted_iota(jnp.int32, (halo, DN_HEAD_DIM), 0)
    for c in range(3 * DN_HEADS):
        sl = slice(c * DN_HEAD_DIM, (c + 1) * DN_HEAD_DIM)
        cur = xc_ref[halo:halo + tm, sl]
        tail = xc_ref[0:halo, sl]
        conv = None
        for j in range(CONV_WIDTH):
            back = CONV_WIDTH - 1 - j
            if back:
                shifted = pltpu.roll(cur, back, 0)
                head = jnp.where(sub < back, pltpu.roll(tail, back, 0), shifted[0:halo])
                shifted = jnp.concatenate([head, shifted[halo:]], axis=0)
            else:
                shifted = cur
            term = shifted * cw_ref[j:j + 1, sl]
            conv = term if conv is None else conv + term
        a = _silu(conv)
        if c < 2 * DN_HEADS:
            a = a * lax.rsqrt(jnp.sum(a * a, axis=-1, keepdims=True) + L2_EPS)
        if c < DN_HEADS:
            a = a * (DN_HEAD_DIM ** -0.5)
        hs = slice((c % DN_HEADS) * DN_HEAD_DIM, (c % DN_HEADS + 1) * DN_HEAD_DIM)
        outs[c // DN_HEADS][:, hs] = a

    aq_ref[...] = section(0) * (ATTN_HEAD_DIM ** -0.5 * LOG2_E)
    ak_ref[...] = section(1)
    av_ref[...] = section(2)
    gate_ref[...] = _silu(section(6))
    small_ref[...] = jnp.dot(h, ws_ref[...], preferred_element_type=F32)


def _inproj(x, gain, w_main, w_small, conv_w, *, seq):
    t, d = x.shape
    tm = TOKEN_TILE
    row = lambda i: (i, 0)
    wide = jax.ShapeDtypeStruct((t, ATTN_WIDTH), F32)
    return pl.pallas_call(
        functools.partial(_inproj_kernel, tiles_per_seq=seq // tm),
        grid=(t // tm,),
        in_specs=[pl.BlockSpec((tm, d), row), _resident((1, d)), _resident(w_main.shape),
                  _resident(w_small.shape), _resident(conv_w.shape)],
        out_specs=[pl.BlockSpec((tm, ATTN_WIDTH), row)] * 7 + [pl.BlockSpec((tm, LANES), row)],
        out_shape=[wide] * 7 + [jax.ShapeDtypeStruct((t, LANES), F32)],
        scratch_shapes=[pltpu.VMEM((tm + 2 * SUBLANES, 3 * DN_WIDTH), F32)],
        compiler_params=_params("arbitrary"),
        name="inproj",
    )(x, gain, w_main, w_small, conv_w)


FINE = 8
COARSE = 16


def _chunks(dilation):
    return FINE // dilation


def _tile_order(dilation, size):
    c = _chunks(dilation) if dilation < FINE else 1
    per = size // c
    x = np.arange(size)
    return c * (x % per) + x // per


def _attn_group(refs, stage, bias_ref, dilation, blocks, *, init):
    fine, coarse, acc_ref, m_ref, l_ref = refs
    s_ref, p_ref, alpha_ref = stage
    d = dilation
    blk = ATTN_BLOCK
    seq = acc_ref.shape[0]
    lane = lax.broadcasted_iota(jnp.int32, (blk, LANES), 1)
    head_lanes = [(lane >= h * ATTN_HEAD_DIM) & (lane < (h + 1) * ATTN_HEAD_DIM)
                  for h in range(HEADS_PER_SLAB)]

    def gather(ref, starts, rows):
        parts = [ref[pl.ds(pl.multiple_of(st, 16), rows), :] for st in starts]
        return parts[0] if len(parts) == 1 else jnp.concatenate(parts, axis=0)

    plans = []
    for r, n in blocks:
        n_key = jnp.maximum(n - 1, 0)
        if d < FINE:
            c = _chunks(d)
            per = seq // FINE
            bases = [(r + d * b) * per for b in range(c)]
            q_starts = [base + (blk // c) * n for base in bases]
            k_starts = [base + (blk // c) * n_key for base in bases]
            q_refs, k_ref, v_ref = fine
            state = [(st, blk // c, 1) for st in q_starts]
        else:
            c = 1
            per = seq // COARSE
            q_starts = [r * per + blk * n]
            k_starts = [r * per + blk * n_key]
            q_refs, k_ref, v_ref = coarse
            ratio = d // FINE
            state = [((r % FINE) * (seq // FINE) + ratio * blk * n + r // FINE, blk, ratio)]
        plans.append((q_refs, k_ref, v_ref, q_starts, k_starts, c, state, jnp.minimum(n, 1)))

    def state_rows(ref, state):
        parts = []
        for start, rows, stride in state:
            if stride == 1:
                parts.append(ref[pl.ds(pl.multiple_of(start, 16), rows), :])
            else:
                parts.append(ref[pl.ds(start, rows, stride=stride), :])
        return parts[0] if len(parts) == 1 else jnp.concatenate(parts, axis=0)

    for i, (q_refs, k_ref, _, q_starts, k_starts, c, _, _) in enumerate(plans):
        kb = gather(k_ref, k_starts, 2 * blk // c)
        for h in range(HEADS_PER_SLAB):
            qh = gather(q_refs[h], q_starts, blk // c)
            s_ref[i * HEADS_PER_SLAB + h] = lax.dot_general(
                qh, kb, (((1,), (1,)), ((), ())), preferred_element_type=F32)

    pending = []
    for i, plan in enumerate(plans):
        state, table = plan[6], plan[7]
        m_prev = None if init else state_rows(m_ref, state)
        m_heads, l_heads = [], []
        for h in range(HEADS_PER_SLAB):
            t = i * HEADS_PER_SLAB + h
            s = s_ref[t] + bias_ref[table, h]
            top = jnp.maximum(s[:, :LANES], s[:, LANES:])
            if not init:
                top = jnp.maximum(top, jnp.where(head_lanes[h], m_prev, MASK_VALUE))
            m_new = jnp.broadcast_to(jnp.max(top, axis=-1, keepdims=True), (blk, LANES))
            p = jnp.exp2(s - jnp.concatenate([m_new, m_new], axis=1))
            l_heads.append(jnp.broadcast_to(jnp.sum(p, axis=-1, keepdims=True), (blk, LANES)))
            m_heads.append(m_new)
            p_ref[t] = p.astype(BF16)
        m_new = jnp.where(head_lanes[0], m_heads[0], m_heads[1])
        l_new = jnp.where(head_lanes[0], l_heads[0], l_heads[1])
        if not init:
            alpha = jnp.exp2(m_prev - m_new)
            alpha_ref[i] = alpha
            l_new = state_rows(l_ref, state) * alpha + l_new
        pending.append((m_ref, state, m_new))
        pending.append((l_ref, state, l_new))

    for i, plan in enumerate(plans):
        v_ref, k_starts, c, state = plan[2], plan[4], plan[5], plan[6]
        vb = gather(v_ref, k_starts, 2 * blk // c)
        pv = [jnp.dot(p_ref[i * HEADS_PER_SLAB + h], vb, preferred_element_type=F32)
              for h in range(HEADS_PER_SLAB)]
        acc_new = jnp.where(head_lanes[0], pv[0], pv[1])
        if not init:
            acc_new = state_rows(acc_ref, state) * alpha_ref[i] + acc_new
        pending.append((acc_ref, state, acc_new))

    for ref, state, value in pending:
        offset = 0
        for start, rows, stride in state:
            piece = value[offset:offset + rows]
            if stride == 1:
                ref[pl.ds(pl.multiple_of(start, 16), rows), :] = piece
            else:
                ref[pl.ds(start, rows, stride=stride), :] = piece
            offset += rows


def _deinterleave(src_ref, tmp_ref, dst_fine, dst_coarse, *, masks):
    seq = src_ref.shape[0]
    quarter = seq // 4
    piece = DEINTERLEAVE_ROWS

    for r4 in range(4):
        for off in range(0, quarter, piece):
            tmp_ref[r4 * quarter + off:r4 * quarter + off + piece, :] = \
                src_ref[pl.ds(r4 + 4 * off, piece, stride=4), :]

    for layout, dsts in ((FINE, dst_fine), (COARSE, dst_coarse)):
        per = seq // layout
        sub = layout // 4
        for res in range(layout):
            for off in range(0, per, piece):
                value = tmp_ref[pl.ds((res % 4) * quarter + res // 4 + sub * off, piece,
                                      stride=sub), :]
                rows = slice(res * per + off, res * per + off + piece)
                if masks is None:
                    dsts[0][rows, :] = value.astype(BF16)
                else:
                    for dst, mask in zip(dsts, masks):
                        dst[rows, :] = jnp.where(mask, value, 0.0).astype(BF16)


def _attn_kernel(q_ref, k_ref, v_ref, b1_ref, b4_ref, b16_ref, o_ref,
                 qf0, qf1, kf, vf, qc0, qc1, kc, vc, tmp_ref, acc_ref, m_ref, l_ref,
                 s_ref, p_ref, alpha_ref):
    seq = q_ref.shape[0]
    blk = ATTN_BLOCK
    grp = ATTN_GROUP
    lane = lax.broadcasted_iota(jnp.int32, (DEINTERLEAVE_ROWS, LANES), 1)
    masks = [(lane >= h * ATTN_HEAD_DIM) & (lane < (h + 1) * ATTN_HEAD_DIM)
             for h in range(HEADS_PER_SLAB)]
    _deinterleave(q_ref, tmp_ref, (qf0, qf1), (qc0, qc1), masks=masks)
    _deinterleave(k_ref, tmp_ref, (kf,), (kc,), masks=None)
    _deinterleave(v_ref, tmp_ref, (vf,), (vc,), masks=None)

    refs = (((qf0, qf1), kf, vf), ((qc0, qc1), kc, vc), acc_ref, m_ref, l_ref)
    stage = (s_ref, p_ref, alpha_ref)
    patterns = sorted(zip(DILATED_PATTERNS, (b1_ref, b4_ref, b16_ref)), key=lambda e: -e[0][1])
    for idx, ((_, d), bias_ref) in enumerate(patterns):
        nb = seq // (d * blk)
        run = functools.partial(_attn_group, refs, stage, bias_ref, d, init=(idx == 0))
        if d < grp:
            per_res = nb // grp

            def body(i, carry, run=run, per_res=per_res):
                r = i // per_res
                n0 = (i % per_res) * grp
                run([(r, n0 + j) for j in range(grp)])
                return carry

            lax.fori_loop(0, d * per_res, body, 0)
        else:
            per_blk = d // grp

            def body(i, carry, run=run, per_blk=per_blk):
                n = i // per_blk
                r0 = (i % per_blk) * grp
                run([(r0 + j, n) for j in range(grp)])
                return carry

            lax.fori_loop(0, nb * per_blk, body, 0)

    per = seq // FINE

    def out_body(i, carry):
        res = i // (per // OUT_ROWS)
        off = (i % (per // OUT_ROWS)) * OUT_ROWS
        rows = pl.ds(pl.multiple_of(res * per + off, OUT_ROWS), OUT_ROWS)
        tmp_ref[pl.ds(res + FINE * off, OUT_ROWS, stride=FINE), :] = acc_ref[rows, :] / l_ref[rows, :]
        return carry

    lax.fori_loop(0, FINE * (per // OUT_ROWS), out_body, 0)

    def cast_body(i, carry):
        rows = pl.ds(pl.multiple_of(i * OUT_ROWS, OUT_ROWS), OUT_ROWS)
        o_ref[rows, :] = tmp_ref[rows, :].astype(o_ref.dtype)
        return carry

    lax.fori_loop(0, seq // OUT_ROWS, cast_body, 0)


def _attn_bias(window, dilation):
    slopes = np.array([2.0 ** (-8.0 * (i + 1) / ATTN_HEADS) for i in range(ATTN_HEADS)],
                      dtype=np.float32)
    qi = _tile_order(dilation, ATTN_BLOCK)[:, None]
    kj = _tile_order(dilation, 2 * ATTN_BLOCK)[None, :]
    tables = []
    for steps in (qi - kj, qi + ATTN_BLOCK - kj):
        valid = (steps >= 0) & (steps <= window // dilation)
        bias = -slopes[:, None, None] * (steps * dilation).astype(np.float32) * np.float32(LOG2_E)
        tables.append(np.where(valid[None], bias, np.float32(MASK_VALUE)))
    table = np.stack(tables).astype(np.float32)
    table = table.reshape(2, ATTN_HEADS // HEADS_PER_SLAB, HEADS_PER_SLAB, *table.shape[2:])
    return jnp.asarray(table.transpose(1, 0, 2, 3, 4))


def _attention(q, k, v, *, batch, seq):
    slabs = ATTN_WIDTH // LANES
    per_seq = pl.BlockSpec((seq, LANES), lambda bi, j: (bi, j))
    bias = [_attn_bias(w, d) for w, d in DILATED_PATTERNS]
    bias_spec = pl.BlockSpec((None,) + bias[0].shape[1:], lambda bi, j: (j, 0, 0, 0, 0))
    rows16 = pltpu.VMEM((seq, LANES), BF16)
    rows32 = pltpu.VMEM((seq, LANES), F32)
    tiles = ATTN_GROUP * HEADS_PER_SLAB
    return pl.pallas_call(
        _attn_kernel,
        grid=(batch, slabs),
        in_specs=[per_seq, per_seq, per_seq, bias_spec, bias_spec, bias_spec],
        out_specs=per_seq,
        out_shape=jax.ShapeDtypeStruct((batch * seq, ATTN_WIDTH), BF16),
        scratch_shapes=[rows16] * 8 + [rows32] * 4 + [
            pltpu.VMEM((tiles, ATTN_BLOCK, 2 * ATTN_BLOCK), F32),
            pltpu.VMEM((tiles, ATTN_BLOCK, 2 * ATTN_BLOCK), BF16),
            pltpu.VMEM((ATTN_GROUP, ATTN_BLOCK, LANES), F32),
        ],
        compiler_params=_params("arbitrary", "arbitrary"),
        name="attention",
    )(q, k, v, *bias)


def _bmm(a, b):
    return lax.dot_general(a.astype(BF16), b.astype(BF16), (((2,), (1,)), ((0,), (0,))),
                           preferred_element_type=F32)


def _bmm_nt(a, b):
    return lax.dot_general(a.astype(BF16), b.astype(BF16), (((2,), (2,)), ((0,), (0,))),
                           preferred_element_type=F32)


MASK_EYE, MASK_CAUSAL, MASK_STRICT, MASK_UPPER, MASK_JOIN = 0, 1, 2, 3, 4


def _dn_masks():
    c = DN_CHUNK
    row, col = np.indices((c, c))
    tables = [row == col, row >= col, row > col, row <= col]
    size = 1
    while size < c:
        shift = size.bit_length() - 1
        tables.append(((row >> (shift + 1)) == (col >> (shift + 1)))
                      & (((row >> shift) & 1) == 1) & (((col >> shift) & 1) == 0))
        size *= 2
    return jnp.asarray(np.stack(tables).astype(np.float32))


def _dn_spread(offset):
    e = np.zeros((LANES, DN_WIDTH), np.float32)
    for hd in range(DN_HEADS):
        e[offset + hd, hd * DN_HEAD_DIM:(hd + 1) * DN_HEAD_DIM] = 1.0
    return jnp.asarray(e, dtype=BF16)


def _unit_lower_inverse(a_strict, mask_ref):
    levels = mask_ref.shape[0] - MASK_JOIN
    t = mask_ref[MASK_EYE] - a_strict * mask_ref[MASK_JOIN]
    for level in range(1, levels):
        lk = a_strict * mask_ref[MASK_JOIN + level]
        t = t - _bmm(t, _bmm(lk, t))
    return t


def _dn_kernel(q_ref, k_ref, v_ref, gate_ref, small_ref, arow_ref, dtrow_ref, nrm_ref, mask_ref,
               ebeta_ref, eg_ref, o_ref, s_ref, wq_ref, u_ref, at_ref, kd_ref):
    c = DN_CHUNK
    dh = DN_HEAD_DIM
    batch, tc = q_ref.shape[:2]
    g_chunks = tc // c
    lanes = batch * DN_HEADS
    nb = g_chunks * lanes

    @pl.when(pl.program_id(0) == 0)
    def _():
        s_ref[...] = jnp.zeros(s_ref.shape, F32)

    sm = small_ref[...]
    beta_all = jax.nn.sigmoid(sm)
    z = sm + dtrow_ref[...]
    softplus = jnp.maximum(z, 0.0) + jnp.log1p(jnp.exp(-jnp.abs(z)))
    g_all = -jnp.exp(arow_ref[...]) * softplus

    def per_head(ref):
        parts = [ref[b, :, hd * dh:(hd + 1) * dh].reshape(g_chunks, 1, c, dh)
                 for b in range(batch) for hd in range(DN_HEADS)]
        return jnp.concatenate(parts, axis=1).reshape(nb, c, dh)

    def split3(x):
        hi = x.astype(BF16)
        rest = x - hi.astype(F32)
        mid = rest.astype(BF16)
        return hi, mid, (rest - mid.astype(F32)).astype(BF16)

    def spread(x, e_ref):
        flat = x.reshape(batch * tc, LANES)
        wide = sum(jnp.dot(part, e_ref[...], preferred_element_type=F32) for part in split3(flat))
        parts = [wide[b * tc:(b + 1) * tc, hd * dh:(hd + 1) * dh].reshape(g_chunks, 1, c, dh)
                 for b in range(batch) for hd in range(DN_HEADS)]
        return jnp.concatenate(parts, axis=1).reshape(nb, c, dh)

    q = per_head(q_ref)
    k = per_head(k_ref)
    v = per_head(v_ref)

    causal = mask_ref[MASK_CAUSAL]

    chunks = batch * g_chunks
    tri = jnp.broadcast_to(causal.astype(BF16), (chunks, c, c))
    gc_all = sum(_bmm(tri, part) for part in split3(g_all.reshape(chunks, c, LANES)))
    beta = spread(beta_all, ebeta_ref)
    gc = spread(gc_all.reshape(batch, tc, LANES), eg_ref)
    gc_mat = gc[..., :c]
    gc_row = jnp.sum(gc_mat * mask_ref[MASK_EYE], axis=1, keepdims=True)
    decay = jnp.exp((gc_mat - gc_row) * causal) * causal

    k_beta = k * beta
    e_gc = jnp.exp(gc)
    g_last = gc[:, c - 1:c, :]
    both = _bmm_nt(jnp.concatenate([k_beta, q], axis=1), k)
    a_mat = both[:, :c] * (decay * mask_ref[MASK_STRICT])
    at_ref[...] = (both[:, c:] * decay).astype(BF16)
    rhs = jnp.concatenate([v * beta, k_beta * e_gc], axis=-1)
    sol = _bmm(_unit_lower_inverse(a_mat, mask_ref), rhs)
    u_ref[...] = sol[..., :dh]
    wq_ref[:, :c, :] = sol[..., dh:].astype(BF16)
    wq_ref[:, c:, :] = (q * e_gc).astype(BF16)
    kd_ref[...] = (k * jnp.exp(g_last - gc)).astype(BF16)
    e_last = jnp.exp(g_last)

    state = s_ref[...]
    outs = []
    for n in range(g_chunks):
        sl = slice(n * lanes, (n + 1) * lanes)
        ws = _bmm(wq_ref[sl], state)
        v_new = u_ref[sl] - ws[:, :c]
        outs.append(ws[:, c:] + _bmm(at_ref[sl], v_new))
        state = state * e_last[sl] + lax.dot_general(
            kd_ref[sl], v_new.astype(BF16), (((1,), (1,)), ((0,), (0,))),
            preferred_element_type=F32)
    s_ref[...] = state

    for b in range(batch):
        for hd in range(DN_HEADS):
            sl = slice(hd * dh, (hd + 1) * dh)
            o = jnp.concatenate([outs[n][b * DN_HEADS + hd] for n in range(g_chunks)], axis=0)
            o = o * lax.rsqrt(jnp.mean(o * o, axis=-1, keepdims=True) + NORM_EPS)
            o_ref[b, :, sl] = (o * nrm_ref[...] * gate_ref[b, :, sl]).astype(o_ref.dtype)


def _deltanet(dq, dk, dv, gate, small, arow, dtrow, nrm, *, batch, seq):
    tc = DN_TOKENS
    nb = (tc // DN_CHUNK) * batch * DN_HEADS
    masks = _dn_masks()
    spread_beta = _dn_spread(0)
    spread_g = _dn_spread(DN_HEADS)
    view = lambda t: t.reshape(batch, seq, t.shape[-1])
    step = lambda j: (0, j, 0)
    wide = pl.BlockSpec((batch, tc, DN_WIDTH), step)
    out = pl.pallas_call(
        _dn_kernel,
        grid=(seq // tc,),
        in_specs=[wide, wide, wide, wide, pl.BlockSpec((batch, tc, LANES), step),
                  _resident((1, LANES)), _resident((1, LANES)), _resident((1, DN_HEAD_DIM)),
                  _resident(masks.shape), _resident(spread_beta.shape),
                  _resident(spread_g.shape)],
        out_specs=wide,
        out_shape=jax.ShapeDtypeStruct((batch, seq, DN_WIDTH), BF16),
        scratch_shapes=[
            pltpu.VMEM((batch * DN_HEADS, DN_HEAD_DIM, DN_HEAD_DIM), F32),
            pltpu.VMEM((nb, 2 * DN_CHUNK, DN_HEAD_DIM), BF16),
            pltpu.VMEM((nb, DN_CHUNK, DN_HEAD_DIM), F32),
            pltpu.VMEM((nb, DN_CHUNK, DN_CHUNK), BF16),
            pltpu.VMEM((nb, DN_CHUNK, DN_HEAD_DIM), BF16),
        ],
        compiler_params=_params("arbitrary"),
        name="deltanet",
    )(view(dq), view(dk), view(dv), view(gate), view(small), arow, dtrow, nrm, masks,
      spread_beta, spread_g)
    return out.reshape(batch * seq, DN_WIDTH)


def _lane_row(values, offset):
    return jnp.zeros((1, LANES), F32).at[0, offset:offset + values.shape[0]].set(values.astype(F32))


def kernel(x, norm_ffn1, ffn1_gate, ffn1_up, ffn1_down, norm_mix, w_in, conv_w, a_log, dt_bias,
           dn_norm, w_out, norm_ffn2, ffn2_gate, ffn2_up, ffn2_down, norm_final):
    batch, seq, d = x.shape
    depth = norm_ffn1.shape[0]
    assert seq % (max(dl for _, dl in DILATED_PATTERNS) * ATTN_BLOCK) == 0
    assert (batch * seq) % FFN_TILE == 0
    assert seq % TOKEN_TILE == 0 and seq % DN_TOKENS == 0 and seq % OUT_ROWS == 0
    xt = x.reshape(batch * seq, d)
    gfin = norm_final.reshape(1, d).astype(F32)
    small_lo = 6 * ATTN_WIDTH
    small_hi = small_lo + 2 * DN_HEADS

    for i in range(depth):
        row = lambda g: g.reshape(1, -1).astype(F32)
        wi = w_in[i]
        w_main = jnp.concatenate([wi[:, :small_lo], wi[:, small_hi:]], axis=1).astype(BF16)
        w_small = jnp.pad(wi[:, small_lo:small_hi], ((0, 0), (0, LANES - 2 * DN_HEADS))).astype(BF16)
        xt = _ffn_in(xt, row(norm_ffn1[i]), ffn1_gate[i], ffn1_up[i], ffn1_down[i])
        aq, ak, av, dq, dk, dv, gate, small = _inproj(
            xt, row(norm_mix[i]), w_main, w_small, conv_w[i].astype(F32), seq=seq)

        attn = _attention(aq, ak, av, batch=batch, seq=seq)
        dn = _deltanet(dq, dk, dv, gate, small, _lane_row(a_log[i], DN_HEADS),
                       _lane_row(dt_bias[i], DN_HEADS), row(dn_norm[i]), batch=batch, seq=seq)

        xt = _ffn_out(xt, attn, dn, w_out[i], row(norm_ffn2[i]), ffn2_gate[i], ffn2_up[i],
                      ffn2_down[i], gfin, final=(i == depth - 1))

    return xt.reshape(batch, seq, d)
```

```python
import functools

import numpy as np
import jax
import jax.numpy as jnp
from jax import lax
from jax.experimental import pallas as pl
from jax.experimental.pallas import tpu as pltpu

F32 = jnp.float32
BF16 = jnp.bfloat16

ATTN_HEADS = 8
ATTN_HEAD_DIM = 64
ATTN_WIDTH = ATTN_HEADS * ATTN_HEAD_DIM
DILATED_PATTERNS = ((128, 1), (512, 4), (2048, 16))
ATTN_BLOCK = 128
DN_HEADS = 4
DN_HEAD_DIM = 128
DN_WIDTH = DN_HEADS * DN_HEAD_DIM
DN_CHUNK = 64
CONV_WIDTH = 4
NORM_EPS = 1e-6
L2_EPS = 1e-6

LANES = 128
SUBLANES = 8
VMEM_LIMIT_BYTES = 56 * 1024 * 1024

MASK_VALUE = -1e30
LOG2_E = 1.4426950408889634

TOKEN_TILE = 512
FFN_TILE = 1024
FF_TILE = 256
DN_TOKENS = 128
HEADS_PER_SLAB = LANES // ATTN_HEAD_DIM
OUT_ROWS = 512
ATTN_GROUP = 8
DEINTERLEAVE_ROWS = 256
WEIGHT_ROWS = 64
WEIGHT_SLOTS = 6


def _rms(x):
    return x * lax.rsqrt(jnp.mean(x * x, axis=-1, keepdims=True) + NORM_EPS)


def _silu(x):
    half = 0.5 * x
    return half + half * jnp.tanh(half)


def _mm(a, b):
    return jnp.dot(a.astype(BF16), b.astype(BF16), preferred_element_type=F32)


def _resident(shape):
    nd = len(shape)
    return pl.BlockSpec(shape, lambda *_: (0,) * nd, pipeline_mode=pl.Buffered(1))


def _params(*sem):
    return pltpu.CompilerParams(dimension_semantics=sem, vmem_limit_bytes=VMEM_LIMIT_BYTES)


def _load_bf16(src_ref, dst_ref, stage_ref, sem_ref):
    slots, rows = stage_ref.shape[:2]
    chunks = src_ref.shape[0] // rows
    ahead = slots - 1
    assert src_ref.shape[0] % rows == 0 and src_ref.shape[1] == stage_ref.shape[2]

    def copy(c):
        slot = c % slots
        return pltpu.make_async_copy(src_ref.at[pl.ds(c * rows, rows), :], stage_ref.at[slot],
                                     sem_ref.at[slot])

    for c in range(min(ahead, chunks)):
        copy(c).start()

    def step(c, carry):
        @pl.when(c + ahead < chunks)
        def _():
            copy(c + ahead).start()

        copy(c).wait()
        dst_ref[pl.ds(pl.multiple_of(c * rows, rows), rows), :] = \
            stage_ref[c % slots].astype(BF16)
        return carry

    lax.fori_loop(0, chunks, step, 0)


def _any():
    return pl.BlockSpec(memory_space=pl.ANY)


def _swiglu_step(x, gain_ref, wg_ref, wu_ref, wd_ref, a_ref):
    h = (_rms(x) * gain_ref[...]).astype(BF16)
    d_ff = wg_ref.shape[1]
    for f0 in range(0, d_ff, FF_TILE):
        g = jnp.dot(h, wg_ref[:, f0:f0 + FF_TILE], preferred_element_type=F32)
        u = jnp.dot(h, wu_ref[:, f0:f0 + FF_TILE], preferred_element_type=F32)
        a_ref[:, f0:f0 + FF_TILE] = (_silu(g) * u).astype(BF16)
    return x + 0.5 * jnp.dot(a_ref[...], wd_ref[...], preferred_element_type=F32)


def _swiglu_scratch(tm, d, d_ff):
    return [pltpu.VMEM((tm, d_ff), BF16),
            pltpu.VMEM((d, d_ff), BF16), pltpu.VMEM((d, d_ff), BF16), pltpu.VMEM((d_ff, d), BF16),
            pltpu.VMEM((WEIGHT_SLOTS, WEIGHT_ROWS, d_ff), F32),
            pltpu.VMEM((WEIGHT_SLOTS, WEIGHT_ROWS, d), F32),
            pltpu.SemaphoreType.DMA((WEIGHT_SLOTS,))]


def _load_swiglu_weights(wg_hbm, wu_hbm, wd_hbm, wg_ref, wu_ref, wd_ref, wide_ref, narrow_ref, sem_ref):
    _load_bf16(wg_hbm, wg_ref, wide_ref, sem_ref)
    _load_bf16(wu_hbm, wu_ref, wide_ref, sem_ref)
    _load_bf16(wd_hbm, wd_ref, narrow_ref, sem_ref)


def _ffn_out_kernel(x_ref, attn_ref, dn_ref, wo_hbm, gain_ref, wg_hbm, wu_hbm, wd_hbm, gfin_ref,
                    o_ref, a_ref, wg_ref, wu_ref, wd_ref, wide_ref, narrow_ref, sem_ref, wo_ref,
                    *, final):
    @pl.when(pl.program_id(0) == 0)
    def _():
        _load_bf16(wo_hbm, wo_ref, narrow_ref, sem_ref)
        _load_swiglu_weights(wg_hbm, wu_hbm, wd_hbm, wg_ref, wu_ref, wd_ref, wide_ref, narrow_ref,
                             sem_ref)

    width = attn_ref.shape[1]
    x = x_ref[...] + jnp.dot(attn_ref[...], wo_ref[:width, :], preferred_element_type=F32)
    x = x + jnp.dot(dn_ref[...], wo_ref[width:, :], preferred_element_type=F32)
    out = _swiglu_step(x, gain_ref, wg_ref, wu_ref, wd_ref, a_ref)
    if final:
        out = _rms(out) * gfin_ref[...]
    o_ref[...] = out


def _ffn_out(x, attn, dn, w_out, gain, wg, wu, wd, gfin, *, final):
    t, d = x.shape
    d_ff = wg.shape[1]
    tm = FFN_TILE
    row = lambda i: (i, 0)
    assert w_out.shape == (attn.shape[1] + dn.shape[1], d)
    return pl.pallas_call(
        functools.partial(_ffn_out_kernel, final=final),
        grid=(t // tm,),
        in_specs=[pl.BlockSpec((tm, d), row), pl.BlockSpec((tm, attn.shape[1]), row),
                  pl.BlockSpec((tm, dn.shape[1]), row), _any(), _resident((1, d)),
                  _any(), _any(), _any(), _resident((1, d))],
        out_specs=pl.BlockSpec((tm, d), row),
        out_shape=jax.ShapeDtypeStruct((t, d), F32),
        scratch_shapes=_swiglu_scratch(tm, d, d_ff) + [pltpu.VMEM(w_out.shape, BF16)],
        compiler_params=_params("arbitrary"),
        name="ffn_out",
    )(x, attn, dn, w_out, gain, wg, wu, wd, gfin)


def _ffn_in_kernel(x_ref, gain_ref, wg_hbm, wu_hbm, wd_hbm, o_ref,
                   a_ref, wg_ref, wu_ref, wd_ref, wide_ref, narrow_ref, sem_ref):
    @pl.when(pl.program_id(0) == 0)
    def _():
        _load_swiglu_weights(wg_hbm, wu_hbm, wd_hbm, wg_ref, wu_ref, wd_ref, wide_ref, narrow_ref,
                             sem_ref)

    o_ref[...] = _swiglu_step(x_ref[...], gain_ref, wg_ref, wu_ref, wd_ref, a_ref)


def _ffn_in(x, gain, wg, wu, wd):
    t, d = x.shape
    d_ff = wg.shape[1]
    tm = FFN_TILE
    row = lambda i: (i, 0)
    return pl.pallas_call(
        _ffn_in_kernel,
        grid=(t // tm,),
        in_specs=[pl.BlockSpec((tm, d), row), _resident((1, d)), _any(), _any(), _any()],
        out_specs=pl.BlockSpec((tm, d), row),
        out_shape=jax.ShapeDtypeStruct((t, d), F32),
        scratch_shapes=_swiglu_scratch(tm, d, d_ff),
        compiler_params=_params("arbitrary"),
        name="ffn_in",
    )(x, gain, wg, wu, wd)


def _inproj_kernel(x_ref, gain_ref, wm_ref, ws_ref, cw_ref,
                   aq_ref, ak_ref, av_ref, dq_ref, dk_ref, dv_ref, gate_ref, small_ref,
                   xc_ref, *, tiles_per_seq):
    i = pl.program_id(0)
    tm = x_ref.shape[0]
    halo = SUBLANES

    @pl.when(i % tiles_per_seq == 0)
    def _():
        xc_ref[0:halo, :] = jnp.zeros((halo, xc_ref.shape[1]), F32)

    @pl.when(i % tiles_per_seq != 0)
    def _():
        xc_ref[0:halo, :] = xc_ref[tm:tm + halo, :]

    h = (_rms(x_ref[...]) * gain_ref[...]).astype(BF16)

    def section(j, width=ATTN_WIDTH):
        return jnp.dot(h, wm_ref[:, j * width:(j + 1) * width], preferred_element_type=F32)

    for j in range(3):
        xc_ref[halo:halo + tm, j * DN_WIDTH:(j + 1) * DN_WIDTH] = section(3 + j)

    outs = (dq_ref, dk_ref, dv_ref)
    sub = lax.broadcasted_iota(jnp.int32, (halo, DN_HEAD_DIM), 0)
    for c in range(3 * DN_HEADS):
        sl = slice(c * DN_HEAD_DIM, (c + 1) * DN_HEAD_DIM)
        cur = xc_ref[halo:halo + tm, sl]
        tail = xc_ref[0:halo, sl]
        conv = None
        for j in range(CONV_WIDTH):
            back = CONV_WIDTH - 1 - j
            if back:
                shifted = pltpu.roll(cur, back, 0)
                head = jnp.where(sub < back, pltpu.roll(tail, back, 0), shifted[0:halo])
                shifted = jnp.concatenate([head, shifted[halo:]], axis=0)
            else:
                shifted = cur
            term = shifted * cw_ref[j:j + 1, sl]
            conv = term if conv is None else conv + term
        a = _silu(conv)
        if c < 2 * DN_HEADS:
            a = a * lax.rsqrt(jnp.sum(a * a, axis=-1, keepdims=True) + L2_EPS)
        if c < DN_HEADS:
            a = a * (DN_HEAD_DIM ** -0.5)
        hs = slice((c % DN_HEADS) * DN_HEAD_DIM, (c % DN_HEADS + 1) * DN_HEAD_DIM)
        outs[c // DN_HEADS][:, hs] = a

    aq_ref[...] = section(0) * (ATTN_HEAD_DIM ** -0.5 * LOG2_E)
    ak_ref[...] = section(1)
    av_ref[...] = section(2)
    gate_ref[...] = _silu(section(6))
    small_ref[...] = jnp.dot(h, ws_ref[...], preferred_element_type=F32)


def _inproj(x, gain, w_main, w_small, conv_w, *, seq):
    t, d = x.shape
    tm = TOKEN_TILE
    row = lambda i: (i, 0)
    wide = jax.ShapeDtypeStruct((t, ATTN_WIDTH), F32)
    return pl.pallas_call(
        functools.partial(_inproj_kernel, tiles_per_seq=seq // tm),
        grid=(t // tm,),
        in_specs=[pl.BlockSpec((tm, d), row), _resident((1, d)), _resident(w_main.shape),
                  _resident(w_small.shape), _resident(conv_w.shape)],
        out_specs=[pl.BlockSpec((tm, ATTN_WIDTH), row)] * 7 + [pl.BlockSpec((tm, LANES), row)],
        out_shape=[wide] * 7 + [jax.ShapeDtypeStruct((t, LANES), F32)],
        scratch_shapes=[pltpu.VMEM((tm + 2 * SUBLANES, 3 * DN_WIDTH), F32)],
        compiler_params=_params("arbitrary"),
        name="inproj",
    )(x, gain, w_main, w_small, conv_w)


FINE = 8
COARSE = 16


def _chunks(dilation):
    return FINE // dilation


def _tile_order(dilation, size):
    c = _chunks(dilation) if dilation < FINE else 1
    per = size // c
    x = np.arange(size)
    return c * (x % per) + x // per


def _attn_group(refs, stage, bias_ref, dilation, blocks, *, init):
    fine, coarse, acc_ref, m_ref, l_ref = refs
    s_ref, p_ref, alpha_ref = stage
    d = dilation
    blk = ATTN_BLOCK
    seq = acc_ref.shape[0]
    lane = lax.broadcasted_iota(jnp.int32, (blk, LANES), 1)
    head_lanes = [(lane >= h * ATTN_HEAD_DIM) & (lane < (h + 1) * ATTN_HEAD_DIM)
                  for h in range(HEADS_PER_SLAB)]

    def gather(ref, starts, rows):
        parts = [ref[pl.ds(pl.multiple_of(st, 16), rows), :] for st in starts]
        return parts[0] if len(parts) == 1 else jnp.concatenate(parts, axis=0)

    plans = []
    for r, n in blocks:
        n_key = jnp.maximum(n - 1, 0)
        if d < FINE:
            c = _chunks(d)
            per = seq // FINE
            bases = [(r + d * b) * per for b in range(c)]
            q_starts = [base + (blk // c) * n for base in bases]
            k_starts = [base + (blk // c) * n_key for base in bases]
            q_refs, k_ref, v_ref = fine
            state = [(st, blk // c, 1) for st in q_starts]
        else:
            c = 1
            per = seq // COARSE
            q_starts = [r * per + blk * n]
            k_starts = [r * per + blk * n_key]
            q_refs, k_ref, v_ref = coarse
            ratio = d // FINE
            state = [((r % FINE) * (seq // FINE) + ratio * blk * n + r // FINE, blk, ratio)]
        plans.append((q_refs, k_ref, v_ref, q_starts, k_starts, c, state, jnp.minimum(n, 1)))

    def state_rows(ref, state):
        parts = []
        for start, rows, stride in state:
            if stride == 1:
                parts.append(ref[pl.ds(pl.multiple_of(start, 16), rows), :])
            else:
                parts.append(ref[pl.ds(start, rows, stride=stride), :])
        return parts[0] if len(parts) == 1 else jnp.concatenate(parts, axis=0)

    for i, (q_refs, k_ref, _, q_starts, k_starts, c, _, _) in enumerate(plans):
        kb = gather(k_ref, k_starts, 2 * blk // c)
        for h in range(HEADS_PER_SLAB):
            qh = gather(q_refs[h], q_starts, blk // c)
            s_ref[i * HEADS_PER_SLAB + h] = lax.dot_general(
                qh, kb, (((1,), (1,)), ((), ())), preferred_element_type=F32)

    pending = []
    for i, plan in enumerate(plans):
        state, table = plan[6], plan[7]
        m_prev = None if init else state_rows(m_ref, state)
        m_heads, l_heads = [], []
        for h in range(HEADS_PER_SLAB):
            t = i * HEADS_PER_SLAB + h
            s = s_ref[t] + bias_ref[table, h]
            top = jnp.maximum(s[:, :LANES], s[:, LANES:])
            if not init:
                top = jnp.maximum(top, jnp.where(head_lanes[h], m_prev, MASK_VALUE))
            m_new = jnp.broadcast_to(jnp.max(top, axis=-1, keepdims=True), (blk, LANES))
            p = jnp.exp2(s - jnp.concatenate([m_new, m_new], axis=1))
            l_heads.append(jnp.broadcast_to(jnp.sum(p, axis=-1, keepdims=True), (blk, LANES)))
            m_heads.append(m_new)
            p_ref[t] = p.astype(BF16)
        m_new = jnp.where(head_lanes[0], m_heads[0], m_heads[1])
        l_new = jnp.where(head_lanes[0], l_heads[0], l_heads[1])
        if not init:
            alpha = jnp.exp2(m_prev - m_new)
            alpha_ref[i] = alpha
            l_new = state_rows(l_ref, state) * alpha + l_new
        pending.append((m_ref, state, m_new))
        pending.append((l_ref, state, l_new))

    for i, plan in enumerate(plans):
        v_ref, k_starts, c, state = plan[2], plan[4], plan[5], plan[6]
        vb = gather(v_ref, k_starts, 2 * blk // c)
        pv = [jnp.dot(p_ref[i * HEADS_PER_SLAB + h], vb, preferred_element_type=F32)
              for h in range(HEADS_PER_SLAB)]
        acc_new = jnp.where(head_lanes[0], pv[0], pv[1])
        if not init:
            acc_new = state_rows(acc_ref, state) * alpha_ref[i] + acc_new
        pending.append((acc_ref, state, acc_new))

    for ref, state, value in pending:
        offset = 0
        for start, rows, stride in state:
            piece = value[offset:offset + rows]
            if stride == 1:
                ref[pl.ds(pl.multiple_of(start, 16), rows), :] = piece
            else:
                ref[pl.ds(start, rows, stride=stride), :] = piece
            offset += rows


def _deinterleave(src_ref, tmp_ref, dst_fine, dst_coarse, *, masks):
    seq = src_ref.shape[0]
    quarter = seq // 4
    piece = DEINTERLEAVE_ROWS

    for r4 in range(4):
        for off in range(0, quarter, piece):
            tmp_ref[r4 * quarter + off:r4 * quarter + off + piece, :] = \
                src_ref[pl.ds(r4 + 4 * off, piece, stride=4), :]

    for layout, dsts in ((FINE, dst_fine), (COARSE, dst_coarse)):
        per = seq // layout
        sub = layout // 4
        for res in range(layout):
            for off in range(0, per, piece):
                value = tmp_ref[pl.ds((res % 4) * quarter + res // 4 + sub * off, piece,
                                      stride=sub), :]
                rows = slice(res * per + off, res * per + off + piece)
                if masks is None:
                    dsts[0][rows, :] = value.astype(BF16)
                else:
                    for dst, mask in zip(dsts, masks):
                        dst[rows, :] = jnp.where(mask, value, 0.0).astype(BF16)


def _attn_kernel(q_ref, k_ref, v_ref, b1_ref, b4_ref, b16_ref, o_ref,
                 qf0, qf1, kf, vf, qc0, qc1, kc, vc, tmp_ref, acc_ref, m_ref, l_ref,
                 s_ref, p_ref, alpha_ref):
    seq = q_ref.shape[0]
    blk = ATTN_BLOCK
    grp = ATTN_GROUP
    lane = lax.broadcasted_iota(jnp.int32, (DEINTERLEAVE_ROWS, LANES), 1)
    masks = [(lane >= h * ATTN_HEAD_DIM) & (lane < (h + 1) * ATTN_HEAD_DIM)
             for h in range(HEADS_PER_SLAB)]
    _deinterleave(q_ref, tmp_ref, (qf0, qf1), (qc0, qc1), masks=masks)
    _deinterleave(k_ref, tmp_ref, (kf,), (kc,), masks=None)
    _deinterleave(v_ref, tmp_ref, (vf,), (vc,), masks=None)

    refs = (((qf0, qf1), kf, vf), ((qc0, qc1), kc, vc), acc_ref, m_ref, l_ref)
    stage = (s_ref, p_ref, alpha_ref)
    patterns = sorted(zip(DILATED_PATTERNS, (b1_ref, b4_ref, b16_ref)), key=lambda e: -e[0][1])
    for idx, ((_, d), bias_ref) in enumerate(patterns):
        nb = seq // (d * blk)
        run = functools.partial(_attn_group, refs, stage, bias_ref, d, init=(idx == 0))
        if d < grp:
            per_res = nb // grp

            def body(i, carry, run=run, per_res=per_res):
                r = i // per_res
                n0 = (i % per_res) * grp
                run([(r, n0 + j) for j in range(grp)])
                return carry

            lax.fori_loop(0, d * per_res, body, 0)
        else:
            per_blk = d // grp

            def body(i, carry, run=run, per_blk=per_blk):
                n = i // per_blk
                r0 = (i % per_blk) * grp
                run([(r0 + j, n) for j in range(grp)])
                return carry

            lax.fori_loop(0, nb * per_blk, body, 0)

    per = seq // FINE

    def out_body(i, carry):
        res = i // (per // OUT_ROWS)
        off = (i % (per // OUT_ROWS)) * OUT_ROWS
        rows = pl.ds(pl.multiple_of(res * per + off, OUT_ROWS), OUT_ROWS)
        tmp_ref[pl.ds(res + FINE * off, OUT_ROWS, stride=FINE), :] = acc_ref[rows, :] / l_ref[rows, :]
        return carry

    lax.fori_loop(0, FINE * (per // OUT_ROWS), out_body, 0)

    def cast_body(i, carry):
        rows = pl.ds(pl.multiple_of(i * OUT_ROWS, OUT_ROWS), OUT_ROWS)
        o_ref[rows, :] = tmp_ref[rows, :].astype(o_ref.dtype)
        return carry

    lax.fori_loop(0, seq // OUT_ROWS, cast_body, 0)


def _attn_bias(window, dilation):
    slopes = np.array([2.0 ** (-8.0 * (i + 1) / ATTN_HEADS) for i in range(ATTN_HEADS)],
                      dtype=np.float32)
    qi = _tile_order(dilation, ATTN_BLOCK)[:, None]
    kj = _tile_order(dilation, 2 * ATTN_BLOCK)[None, :]
    tables = []
    for steps in (qi - kj, qi + ATTN_BLOCK - kj):
        valid = (steps >= 0) & (steps <= window // dilation)
        bias = -slopes[:, None, None] * (steps * dilation).astype(np.float32) * np.float32(LOG2_E)
        tables.append(np.where(valid[None], bias, np.float32(MASK_VALUE)))
    table = np.stack(tables).astype(np.float32)
    table = table.reshape(2, ATTN_HEADS // HEADS_PER_SLAB, HEADS_PER_SLAB, *table.shape[2:])
    return jnp.asarray(table.transpose(1, 0, 2, 3, 4))


def _attention(q, k, v, *, batch, seq):
    slabs = ATTN_WIDTH // LANES
    per_seq = pl.BlockSpec((seq, LANES), lambda bi, j: (bi, j))
    bias = [_attn_bias(w, d) for w, d in DILATED_PATTERNS]
    bias_spec = pl.BlockSpec((None,) + bias[0].shape[1:], lambda bi, j: (j, 0, 0, 0, 0))
    rows16 = pltpu.VMEM((seq, LANES), BF16)
    rows32 = pltpu.VMEM((seq, LANES), F32)
    tiles = ATTN_GROUP * HEADS_PER_SLAB
    return pl.pallas_call(
        _attn_kernel,
        grid=(batch, slabs),
        in_specs=[per_seq, per_seq, per_seq, bias_spec, bias_spec, bias_spec],
        out_specs=per_seq,
        out_shape=jax.ShapeDtypeStruct((batch * seq, ATTN_WIDTH), BF16),
        scratch_shapes=[rows16] * 8 + [rows32] * 4 + [
            pltpu.VMEM((tiles, ATTN_BLOCK, 2 * ATTN_BLOCK), F32),
            pltpu.VMEM((tiles, ATTN_BLOCK, 2 * ATTN_BLOCK), BF16),
            pltpu.VMEM((ATTN_GROUP, ATTN_BLOCK, LANES), F32),
        ],
        compiler_params=_params("arbitrary", "arbitrary"),
        name="attention",
    )(q, k, v, *bias)


def _bmm(a, b):
    return lax.dot_general(a.astype(BF16), b.astype(BF16), (((2,), (1,)), ((0,), (0,))),
                           preferred_element_type=F32)


def _bmm_nt(a, b):
    return lax.dot_general(a.astype(BF16), b.astype(BF16), (((2,), (2,)), ((0,), (0,))),
                           preferred_element_type=F32)


MASK_EYE, MASK_CAUSAL, MASK_STRICT, MASK_UPPER, MASK_JOIN = 0, 1, 2, 3, 4


def _dn_masks():
    c = DN_CHUNK
    row, col = np.indices((c, c))
    tables = [row == col, row >= col, row > col, row <= col]
    size = 1
    while size < c:
        shift = size.bit_length() - 1
        tables.append(((row >> (shift + 1)) == (col >> (shift + 1)))
                      & (((row >> shift) & 1) == 1) & (((col >> shift) & 1) == 0))
        size *= 2
    return jnp.asarray(np.stack(tables).astype(np.float32))


def _dn_spread(offset):
    e = np.zeros((LANES, DN_WIDTH), np.float32)
    for hd in range(DN_HEADS):
        e[offset + hd, hd * DN_HEAD_DIM:(hd + 1) * DN_HEAD_DIM] = 1.0
    return jnp.asarray(e, dtype=BF16)


def _unit_lower_inverse(a_strict, mask_ref):
    levels = mask_ref.shape[0] - MASK_JOIN
    t = mask_ref[MASK_EYE] - a_strict * mask_ref[MASK_JOIN]
    for level in range(1, levels):
        lk = a_strict * mask_ref[MASK_JOIN + level]
        t = t - _bmm(t, _bmm(lk, t))
    return t


def _dn_kernel(q_ref, k_ref, v_ref, gate_ref, small_ref, arow_ref, dtrow_ref, nrm_ref, mask_ref,
               ebeta_ref, eg_ref, o_ref, s_ref, wq_ref, u_ref, at_ref, kd_ref):
    c = DN_CHUNK
    dh = DN_HEAD_DIM
    batch, tc = q_ref.shape[:2]
    g_chunks = tc // c
    lanes = batch * DN_HEADS
    nb = g_chunks * lanes

    @pl.when(pl.program_id(0) == 0)
    def _():
        s_ref[...] = jnp.zeros(s_ref.shape, F32)

    sm = small_ref[...]
    beta_all = jax.nn.sigmoid(sm)
    z = sm + dtrow_ref[...]
    softplus = jnp.maximum(z, 0.0) + jnp.log1p(jnp.exp(-jnp.abs(z)))
    g_all = -jnp.exp(arow_ref[...]) * softplus

    def per_head(ref):
        parts = [ref[b, :, hd * dh:(hd + 1) * dh].reshape(g_chunks, 1, c, dh)
                 for b in range(batch) for hd in range(DN_HEADS)]
        return jnp.concatenate(parts, axis=1).reshape(nb, c, dh)

    def split3(x):
        hi = x.astype(BF16)
        rest = x - hi.astype(F32)
        mid = rest.astype(BF16)
        return hi, mid, (rest - mid.astype(F32)).astype(BF16)

    def spread(x, e_ref):
        flat = x.reshape(batch * tc, LANES)
        wide = sum(jnp.dot(part, e_ref[...], preferred_element_type=F32) for part in split3(flat))
        parts = [wide[b * tc:(b + 1) * tc, hd * dh:(hd + 1) * dh].reshape(g_chunks, 1, c, dh)
                 for b in range(batch) for hd in range(DN_HEADS)]
        return jnp.concatenate(parts, axis=1).reshape(nb, c, dh)

    q = per_head(q_ref)
    k = per_head(k_ref)
    v = per_head(v_ref)

    causal = mask_ref[MASK_CAUSAL]

    chunks = batch * g_chunks
    tri = jnp.broadcast_to(causal.astype(BF16), (chunks, c, c))
    gc_all = sum(_bmm(tri, part) for part in split3(g_all.reshape(chunks, c, LANES)))
    beta = spread(beta_all, ebeta_ref)
    gc = spread(gc_all.reshape(batch, tc, LANES), eg_ref)
    gc_mat = gc[..., :c]
    gc_row = jnp.sum(gc_mat * mask_ref[MASK_EYE], axis=1, keepdims=True)
    decay = jnp.exp((gc_mat - gc_row) * causal) * causal

    k_beta = k * beta
    e_gc = jnp.exp(gc)
    g_last = gc[:, c - 1:c, :]
    both = _bmm_nt(jnp.concatenate([k_beta, q], axis=1), k)
    a_mat = both[:, :c] * (decay * mask_ref[MASK_STRICT])
    at_ref[...] = (both[:, c:] * decay).astype(BF16)
    rhs = jnp.concatenate([v * beta, k_beta * e_gc], axis=-1)
    sol = _bmm(_unit_lower_inverse(a_mat, mask_ref), rhs)
    u_ref[...] = sol[..., :dh]
    wq_ref[:, :c, :] = sol[..., dh:].astype(BF16)
    wq_ref[:, c:, :] = (q * e_gc).astype(BF16)
    kd_ref[...] = (k * jnp.exp(g_last - gc)).astype(BF16)
    e_last = jnp.exp(g_last)

    state = s_ref[...]
    outs = []
    for n in range(g_chunks):
        sl = slice(n * lanes, (n + 1) * lanes)
        ws = _bmm(wq_ref[sl], state)
        v_new = u_ref[sl] - ws[:, :c]
        outs.append(ws[:, c:] + _bmm(at_ref[sl], v_new))
        state = state * e_last[sl] + lax.dot_general(
            kd_ref[sl], v_new.astype(BF16), (((1,), (1,)), ((0,), (0,))),
            preferred_element_type=F32)
    s_ref[...] = state

    for b in range(batch):
        for hd in range(DN_HEADS):
            sl = slice(hd * dh, (hd + 1) * dh)
            o = jnp.concatenate([outs[n][b * DN_HEADS + hd] for n in range(g_chunks)], axis=0)
            o = o * lax.rsqrt(jnp.mean(o * o, axis=-1, keepdims=True) + NORM_EPS)
            o_ref[b, :, sl] = (o * nrm_ref[...] * gate_ref[b, :, sl]).astype(o_ref.dtype)


def _deltanet(dq, dk, dv, gate, small, arow, dtrow, nrm, *, batch, seq):
    tc = DN_TOKENS
    nb = (tc // DN_CHUNK) * batch * DN_HEADS
    masks = _dn_masks()
    spread_beta = _dn_spread(0)
    spread_g = _dn_spread(DN_HEADS)
    view = lambda t: t.reshape(batch, seq, t.shape[-1])
    step = lambda j: (0, j, 0)
    wide = pl.BlockSpec((batch, tc, DN_WIDTH), step)
    out = pl.pallas_call(
        _dn_kernel,
        grid=(seq // tc,),
        in_specs=[wide, wide, wide, wide, pl.BlockSpec((batch, tc, LANES), step),
                  _resident((1, LANES)), _resident((1, LANES)), _resident((1, DN_HEAD_DIM)),
                  _resident(masks.shape), _resident(spread_beta.shape),
                  _resident(spread_g.shape)],
        out_specs=wide,
        out_shape=jax.ShapeDtypeStruct((batch, seq, DN_WIDTH), BF16),
        scratch_shapes=[
            pltpu.VMEM((batch * DN_HEADS, DN_HEAD_DIM, DN_HEAD_DIM), F32),
            pltpu.VMEM((nb, 2 * DN_CHUNK, DN_HEAD_DIM), BF16),
            pltpu.VMEM((nb, DN_CHUNK, DN_HEAD_DIM), F32),
            pltpu.VMEM((nb, DN_CHUNK, DN_CHUNK), BF16),
            pltpu.VMEM((nb, DN_CHUNK, DN_HEAD_DIM), BF16),
        ],
        compiler_params=_params("arbitrary"),
        name="deltanet",
    )(view(dq), view(dk), view(dv), view(gate), view(small), arow, dtrow, nrm, masks,
      spread_beta, spread_g)
    return out.reshape(batch * seq, DN_WIDTH)


def _lane_row(values, offset):
    return jnp.zeros((1, LANES), F32).at[0, offset:offset + values.shape[0]].set(values.astype(F32))


def kernel(x, norm_ffn1, ffn1_gate, ffn1_up, ffn1_down, norm_mix, w_in, conv_w, a_log, dt_bias,
           dn_norm, w_out, norm_ffn2, ffn2_gate, ffn2_up, ffn2_down, norm_final):
    batch, seq, d = x.shape
    depth = norm_ffn1.shape[0]
    assert seq % (max(dl for _, dl in DILATED_PATTERNS) * ATTN_BLOCK) == 0
    assert (batch * seq) % FFN_TILE == 0
    assert seq % TOKEN_TILE == 0 and seq % DN_TOKENS == 0 and seq % OUT_ROWS == 0
    xt = x.reshape(batch * seq, d)
    gfin = norm_final.reshape(1, d).astype(F32)
    small_lo = 6 * ATTN_WIDTH
    small_hi = small_lo + 2 * DN_HEADS

    for i in range(depth):
        row = lambda g: g.reshape(1, -1).astype(F32)
        wi = w_in[i]
        w_main = jnp.concatenate([wi[:, :small_lo], wi[:, small_hi:]], axis=1).astype(BF16)
        w_small = jnp.pad(wi[:, small_lo:small_hi], ((0, 0), (0, LANES - 2 * DN_HEADS))).astype(BF16)
        xt = _ffn_in(xt, row(norm_ffn1[i]), ffn1_gate[i], ffn1_up[i], ffn1_down[i])
        aq, ak, av, dq, dk, dv, gate, small = _inproj(
            xt, row(norm_mix[i]), w_main, w_small, conv_w[i].astype(F32), seq=seq)

        attn = _attention(aq, ak, av, batch=batch, seq=seq)
        dn = _deltanet(dq, dk, dv, gate, small, _lane_row(a_log[i], DN_HEADS),
                       _lane_row(dt_bias[i], DN_HEADS), row(dn_norm[i]), batch=batch, seq=seq)

        xt = _ffn_out(xt, attn, dn, w_out[i], row(norm_ffn2[i]), ffn2_gate[i], ffn2_up[i],
                      ffn2_down[i], gfin, final=(i == depth - 1))

    return xt.reshape(batch, seq, d)
```

```python
import functools

import numpy as np
import jax
import jax.numpy as jnp
from jax import lax
from jax.experimental import pallas as pl
from jax.experimental.pallas import tpu as pltpu

F32 = jnp.float32
BF16 = jnp.bfloat16

ATTN_HEADS = 8
ATTN_HEAD_DIM = 64
ATTN_WIDTH = ATTN_HEADS * ATTN_HEAD_DIM
DILATED_PATTERNS = ((128, 1), (512, 4), (2048, 16))
ATTN_BLOCK = 128
DN_HEADS = 4
DN_HEAD_DIM = 128
DN_WIDTH = DN_HEADS * DN_HEAD_DIM
DN_CHUNK = 64
CONV_WIDTH = 4
NORM_EPS = 1e-6
L2_EPS = 1e-6

LANES = 128
SUBLANES = 8
VMEM_LIMIT_BYTES = 56 * 1024 * 1024

MASK_VALUE = -1e30
LOG2_E = 1.4426950408889634

TOKEN_TILE = 512
FFN_TILE = 1024
FF_TILE = 256
DN_TOKENS = 128
HEADS_PER_SLAB = LANES // ATTN_HEAD_DIM
OUT_ROWS = 512
ATTN_GROUP = 8
DEINTERLEAVE_ROWS = 256
WEIGHT_ROWS = 64
WEIGHT_SLOTS = 6


def _rms(x):
    return x * lax.rsqrt(jnp.mean(x * x, axis=-1, keepdims=True) + NORM_EPS)


def _silu(x):
    half = 0.5 * x
    return half + half * jnp.tanh(half)


def _mm(a, b):
    return jnp.dot(a.astype(BF16), b.astype(BF16), preferred_element_type=F32)


def _resident(shape):
    nd = len(shape)
    return pl.BlockSpec(shape, lambda *_: (0,) * nd, pipeline_mode=pl.Buffered(1))


def _params(*sem):
    return pltpu.CompilerParams(dimension_semantics=sem, vmem_limit_bytes=VMEM_LIMIT_BYTES)


def _load_rows(src_ref, stage_ref, sem_ref, emit):
    slots, rows = stage_ref.shape[:2]
    chunks = src_ref.shape[0] // rows
    ahead = slots - 1
    assert src_ref.shape[0] % rows == 0 and src_ref.shape[1] == stage_ref.shape[2]

    def copy(c):
        slot = c % slots
        return pltpu.make_async_copy(src_ref.at[pl.ds(c * rows, rows), :], stage_ref.at[slot],
                                     sem_ref.at[slot])

    for c in range(min(ahead, chunks)):
        copy(c).start()

    def step(c, carry):
        @pl.when(c + ahead < chunks)
        def _():
            copy(c + ahead).start()

        copy(c).wait()
        emit(pl.ds(pl.multiple_of(c * rows, rows), rows), stage_ref[c % slots])
        return carry

    lax.fori_loop(0, chunks, step, 0)


def _load_bf16(src_ref, dst_ref, stage_ref, sem_ref):
    def emit(rows, chunk):
        dst_ref[rows, :] = chunk.astype(BF16)

    _load_rows(src_ref, stage_ref, sem_ref, emit)


def _any():
    return pl.BlockSpec(memory_space=pl.ANY)


def _swiglu_step(x, gain_ref, wg_ref, wu_ref, wd_ref, a_ref):
    h = (_rms(x) * gain_ref[...]).astype(BF16)
    d_ff = wg_ref.shape[1]
    for f0 in range(0, d_ff, FF_TILE):
        g = jnp.dot(h, wg_ref[:, f0:f0 + FF_TILE], preferred_element_type=F32)
        u = jnp.dot(h, wu_ref[:, f0:f0 + FF_TILE], preferred_element_type=F32)
        a_ref[:, f0:f0 + FF_TILE] = (_silu(g) * u).astype(BF16)
    return x + 0.5 * jnp.dot(a_ref[...], wd_ref[...], preferred_element_type=F32)


def _swiglu_scratch(tm, d, d_ff):
    return [pltpu.VMEM((tm, d_ff), BF16),
            pltpu.VMEM((d, d_ff), BF16), pltpu.VMEM((d, d_ff), BF16), pltpu.VMEM((d_ff, d), BF16),
            pltpu.VMEM((WEIGHT_SLOTS, WEIGHT_ROWS, d_ff), F32),
            pltpu.VMEM((WEIGHT_SLOTS, WEIGHT_ROWS, d), F32),
            pltpu.SemaphoreType.DMA((WEIGHT_SLOTS,))]


def _load_swiglu_weights(wg_hbm, wu_hbm, wd_hbm, wg_ref, wu_ref, wd_ref, wide_ref, narrow_ref, sem_ref):
    _load_bf16(wg_hbm, wg_ref, wide_ref, sem_ref)
    _load_bf16(wu_hbm, wu_ref, wide_ref, sem_ref)
    _load_bf16(wd_hbm, wd_ref, narrow_ref, sem_ref)


def _ffn_out_kernel(x_ref, attn_ref, dn_ref, wo_hbm, gain_ref, wg_hbm, wu_hbm, wd_hbm, gfin_ref,
                    o_ref, a_ref, wg_ref, wu_ref, wd_ref, wide_ref, narrow_ref, sem_ref, wo_ref,
                    *, final):
    @pl.when(pl.program_id(0) == 0)
    def _():
        _load_bf16(wo_hbm, wo_ref, narrow_ref, sem_ref)
        _load_swiglu_weights(wg_hbm, wu_hbm, wd_hbm, wg_ref, wu_ref, wd_ref, wide_ref, narrow_ref,
                             sem_ref)

    width = attn_ref.shape[1]
    x = x_ref[...] + jnp.dot(attn_ref[...], wo_ref[:width, :], preferred_element_type=F32)
    x = x + jnp.dot(dn_ref[...], wo_ref[width:, :], preferred_element_type=F32)
    out = _swiglu_step(x, gain_ref, wg_ref, wu_ref, wd_ref, a_ref)
    if final:
        out = _rms(out) * gfin_ref[...]
    o_ref[...] = out


def _ffn_out(x, attn, dn, w_out, gain, wg, wu, wd, gfin, *, final):
    t, d = x.shape
    d_ff = wg.shape[1]
    tm = FFN_TILE
    row = lambda i: (i, 0)
    assert w_out.shape == (attn.shape[1] + dn.shape[1], d)
    return pl.pallas_call(
        functools.partial(_ffn_out_kernel, final=final),
        grid=(t // tm,),
        in_specs=[pl.BlockSpec((tm, d), row), pl.BlockSpec((tm, attn.shape[1]), row),
                  pl.BlockSpec((tm, dn.shape[1]), row), _any(), _resident((1, d)),
                  _any(), _any(), _any(), _resident((1, d))],
        out_specs=pl.BlockSpec((tm, d), row),
        out_shape=jax.ShapeDtypeStruct((t, d), F32),
        scratch_shapes=_swiglu_scratch(tm, d, d_ff) + [pltpu.VMEM(w_out.shape, BF16)],
        compiler_params=_params("arbitrary"),
        name="ffn_out",
    )(x, attn, dn, w_out, gain, wg, wu, wd, gfin)


def _ffn_in_kernel(x_ref, gain_ref, wg_hbm, wu_hbm, wd_hbm, o_ref,
                   a_ref, wg_ref, wu_ref, wd_ref, wide_ref, narrow_ref, sem_ref):
    @pl.when(pl.program_id(0) == 0)
    def _():
        _load_swiglu_weights(wg_hbm, wu_hbm, wd_hbm, wg_ref, wu_ref, wd_ref, wide_ref, narrow_ref,
                             sem_ref)

    o_ref[...] = _swiglu_step(x_ref[...], gain_ref, wg_ref, wu_ref, wd_ref, a_ref)


def _ffn_in(x, gain, wg, wu, wd):
    t, d = x.shape
    d_ff = wg.shape[1]
    tm = FFN_TILE
    row = lambda i: (i, 0)
    return pl.pallas_call(
        _ffn_in_kernel,
        grid=(t // tm,),
        in_specs=[pl.BlockSpec((tm, d), row), _resident((1, d)), _any(), _any(), _any()],
        out_specs=pl.BlockSpec((tm, d), row),
        out_shape=jax.ShapeDtypeStruct((t, d), F32),
        scratch_shapes=_swiglu_scratch(tm, d, d_ff),
        compiler_params=_params("arbitrary"),
        name="ffn_in",
    )(x, gain, wg, wu, wd)


def _inproj_kernel(x_ref, gain_ref, w_hbm, cw_ref,
                   aq_ref, ak_ref, av_ref, dq_ref, dk_ref, dv_ref, gate_ref, small_ref,
                   xc_ref, wm_ref, ws_ref, stage_ref, sem_ref, *, tiles_per_seq, small_lo, small_hi):
    i = pl.program_id(0)
    tm = x_ref.shape[0]
    halo = SUBLANES

    @pl.when(i == 0)
    def _():
        def emit(rows, chunk):
            wm_ref[rows, :] = jnp.concatenate(
                [chunk[:, :small_lo], chunk[:, small_hi:]], axis=1).astype(BF16)
            lane = lax.broadcasted_iota(jnp.int32, (chunk.shape[0], LANES), 1)
            small = jnp.where(lane < small_hi - small_lo, chunk[:, small_lo:small_lo + LANES], 0.0)
            ws_ref[rows, :] = small.astype(BF16)

        _load_rows(w_hbm, stage_ref, sem_ref, emit)

    @pl.when(i % tiles_per_seq == 0)
    def _():
        xc_ref[0:halo, :] = jnp.zeros((halo, xc_ref.shape[1]), F32)

    @pl.when(i % tiles_per_seq != 0)
    def _():
        xc_ref[0:halo, :] = xc_ref[tm:tm + halo, :]

    h = (_rms(x_ref[...]) * gain_ref[...]).astype(BF16)

    def section(j, width=ATTN_WIDTH):
        return jnp.dot(h, wm_ref[:, j * width:(j + 1) * width], preferred_element_type=F32)

    for j in range(3):
        xc_ref[halo:halo + tm, j * DN_WIDTH:(j + 1) * DN_WIDTH] = section(3 + j)

    outs = (dq_ref, dk_ref, dv_ref)
    sub = lax.broadcasted_iota(jnp.int32, (halo, DN_HEAD_DIM), 0)
    for c in range(3 * DN_HEADS):
        sl = slice(c * DN_HEAD_DIM, (c + 1) * DN_HEAD_DIM)
        cur = xc_ref[halo:halo + tm, sl]
        tail = xc_ref[0:halo, sl]
        conv = None
        for j in range(CONV_WIDTH):
            back = CONV_WIDTH - 1 - j
            if back:
                shifted = pltpu.roll(cur, back, 0)
                head = jnp.where(sub < back, pltpu.roll(tail, back, 0), shifted[0:halo])
                shifted = jnp.concatenate([head, shifted[halo:]], axis=0)
            else:
                shifted = cur
            term = shifted * cw_ref[j:j + 1, sl]
            conv = term if conv is None else conv + term
        a = _silu(conv)
        if c < 2 * DN_HEADS:
            a = a * lax.rsqrt(jnp.sum(a * a, axis=-1, keepdims=True) + L2_EPS)
        if c < DN_HEADS:
            a = a * (DN_HEAD_DIM ** -0.5)
        hs = slice((c % DN_HEADS) * DN_HEAD_DIM, (c % DN_HEADS + 1) * DN_HEAD_DIM)
        outs[c // DN_HEADS][:, hs] = a

    aq_ref[...] = section(0) * (ATTN_HEAD_DIM ** -0.5 * LOG2_E)
    ak_ref[...] = section(1)
    av_ref[...] = section(2)
    gate_ref[...] = _silu(section(6))
    small_ref[...] = jnp.dot(h, ws_ref[...], preferred_element_type=F32)


def _inproj(x, gain, w_in, conv_w, *, seq):
    t, d = x.shape
    tm = TOKEN_TILE
    row = lambda i: (i, 0)
    wide = jax.ShapeDtypeStruct((t, ATTN_WIDTH), F32)
    small_lo = 6 * ATTN_WIDTH
    small_hi = small_lo + 2 * DN_HEADS
    cols = w_in.shape[1]
    assert small_lo % LANES == 0 and cols - small_hi == DN_WIDTH
    return pl.pallas_call(
        functools.partial(_inproj_kernel, tiles_per_seq=seq // tm, small_lo=small_lo,
                          small_hi=small_hi),
        grid=(t // tm,),
        in_specs=[pl.BlockSpec((tm, d), row), _resident((1, d)), _any(),
                  _resident(conv_w.shape)],
        out_specs=[pl.BlockSpec((tm, ATTN_WIDTH), row)] * 7 + [pl.BlockSpec((tm, LANES), row)],
        out_shape=[wide] * 7 + [jax.ShapeDtypeStruct((t, LANES), F32)],
        scratch_shapes=[pltpu.VMEM((tm + 2 * SUBLANES, 3 * DN_WIDTH), F32),
                        pltpu.VMEM((d, cols - (small_hi - small_lo)), BF16),
                        pltpu.VMEM((d, LANES), BF16),
                        pltpu.VMEM((WEIGHT_SLOTS, WEIGHT_ROWS, cols), F32),
                        pltpu.SemaphoreType.DMA((WEIGHT_SLOTS,))],
        compiler_params=_params("arbitrary"),
        name="inproj",
    )(x, gain, w_in, conv_w)


FINE = 8
COARSE = 16


def _chunks(dilation):
    return FINE // dilation


def _tile_order(dilation, size):
    c = _chunks(dilation) if dilation < FINE else 1
    per = size // c
    x = np.arange(size)
    return c * (x % per) + x // per


def _attn_group(refs, stage, bias_ref, dilation, blocks, *, init):
    fine, coarse, acc_ref, m_ref, l_ref = refs
    s_ref, p_ref, alpha_ref = stage
    d = dilation
    blk = ATTN_BLOCK
    seq = acc_ref.shape[0]
    lane = lax.broadcasted_iota(jnp.int32, (blk, LANES), 1)
    head_lanes = [(lane >= h * ATTN_HEAD_DIM) & (lane < (h + 1) * ATTN_HEAD_DIM)
                  for h in range(HEADS_PER_SLAB)]

    def gather(ref, starts, rows):
        parts = [ref[pl.ds(pl.multiple_of(st, 16), rows), :] for st in starts]
        return parts[0] if len(parts) == 1 else jnp.concatenate(parts, axis=0)

    plans = []
    for r, n in blocks:
        n_key = jnp.maximum(n - 1, 0)
        if d < FINE:
            c = _chunks(d)
            per = seq // FINE
            bases = [(r + d * b) * per for b in range(c)]
            q_starts = [base + (blk // c) * n for base in bases]
            k_starts = [base + (blk // c) * n_key for base in bases]
            q_refs, k_ref, v_ref = fine
            state = [(st, blk // c, 1) for st in q_starts]
        else:
            c = 1
            per = seq // COARSE
            q_starts = [r * per + blk * n]
            k_starts = [r * per + blk * n_key]
            q_refs, k_ref, v_ref = coarse
            ratio = d // FINE
            state = [((r % FINE) * (seq // FINE) + ratio * blk * n + r // FINE, blk, ratio)]
        plans.append((q_refs, k_ref, v_ref, q_starts, k_starts, c, state, jnp.minimum(n, 1)))

    def state_rows(ref, state):
        parts = []
        for start, rows, stride in state:
            if stride == 1:
                parts.append(ref[pl.ds(pl.multiple_of(start, 16), rows), :])
            else:
                parts.append(ref[pl.ds(start, rows, stride=stride), :])
        return parts[0] if len(parts) == 1 else jnp.concatenate(parts, axis=0)

    for i, (q_refs, k_ref, _, q_starts, k_starts, c, _, _) in enumerate(plans):
        kb = gather(k_ref, k_starts, 2 * blk // c)
        for h in range(HEADS_PER_SLAB):
            qh = gather(q_refs[h], q_starts, blk // c)
            s_ref[i * HEADS_PER_SLAB + h] = lax.dot_general(
                qh, kb, (((1,), (1,)), ((), ())), preferred_element_type=F32)

    pending = []
    for i, plan in enumerate(plans):
        state, table = plan[6], plan[7]
        m_prev = None if init else state_rows(m_ref, state)
        m_heads, l_heads = [], []
        for h in range(HEADS_PER_SLAB):
            t = i * HEADS_PER_SLAB + h
            s = s_ref[t] + bias_ref[table, h]
            top = jnp.maximum(s[:, :LANES], s[:, LANES:])
            if not init:
                top = jnp.maximum(top, jnp.where(head_lanes[h], m_prev, MASK_VALUE))
            m_new = jnp.broadcast_to(jnp.max(top, axis=-1, keepdims=True), (blk, LANES))
            p = jnp.exp2(s - jnp.concatenate([m_new, m_new], axis=1))
            l_heads.append(jnp.broadcast_to(jnp.sum(p, axis=-1, keepdims=True), (blk, LANES)))
            m_heads.append(m_new)
            p_ref[t] = p.astype(BF16)
        m_new = jnp.where(head_lanes[0], m_heads[0], m_heads[1])
        l_new = jnp.where(head_lanes[0], l_heads[0], l_heads[1])
        if not init:
            alpha = jnp.exp2(m_prev - m_new)
            alpha_ref[i] = alpha
            l_new = state_rows(l_ref, state) * alpha + l_new
        pending.append((m_ref, state, m_new))
        pending.append((l_ref, state, l_new))

    for i, plan in enumerate(plans):
        v_ref, k_starts, c, state = plan[2], plan[4], plan[5], plan[6]
        vb = gather(v_ref, k_starts, 2 * blk // c)
        pv = [jnp.dot(p_ref[i * HEADS_PER_SLAB + h], vb, preferred_element_type=F32)
              for h in range(HEADS_PER_SLAB)]
        acc_new = jnp.where(head_lanes[0], pv[0], pv[1])
        if not init:
            acc_new = state_rows(acc_ref, state) * alpha_ref[i] + acc_new
        pending.append((acc_ref, state, acc_new))

    for ref, state, value in pending:
        offset = 0
        for start, rows, stride in state:
            piece = value[offset:offset + rows]
            if stride == 1:
                ref[pl.ds(pl.multiple_of(start, 16), rows), :] = piece
            else:
                ref[pl.ds(start, rows, stride=stride), :] = piece
            offset += rows


def _deinterleave(src_ref, tmp_ref, dst_fine, dst_coarse, *, masks):
    seq = src_ref.shape[0]
    quarter = seq // 4
    piece = DEINTERLEAVE_ROWS

    for r4 in range(4):
        for off in range(0, quarter, piece):
            tmp_ref[r4 * quarter + off:r4 * quarter + off + piece, :] = \
                src_ref[pl.ds(r4 + 4 * off, piece, stride=4), :]

    for layout, dsts in ((FINE, dst_fine), (COARSE, dst_coarse)):
        per = seq // layout
        sub = layout // 4
        for res in range(layout):
            for off in range(0, per, piece):
                value = tmp_ref[pl.ds((res % 4) * quarter + res // 4 + sub * off, piece,
                                      stride=sub), :]
                rows = slice(res * per + off, res * per + off + piece)
                if masks is None:
                    dsts[0][rows, :] = value.astype(BF16)
                else:
                    for dst, mask in zip(dsts, masks):
                        dst[rows, :] = jnp.where(mask, value, 0.0).astype(BF16)


def _attn_kernel(q_ref, k_ref, v_ref, b1_ref, b4_ref, b16_ref, o_ref,
                 qf0, qf1, kf, vf, qc0, qc1, kc, vc, tmp_ref, acc_ref, m_ref, l_ref,
                 s_ref, p_ref, alpha_ref):
    seq = q_ref.shape[0]
    blk = ATTN_BLOCK
    grp = ATTN_GROUP
    lane = lax.broadcasted_iota(jnp.int32, (DEINTERLEAVE_ROWS, LANES), 1)
    masks = [(lane >= h * ATTN_HEAD_DIM) & (lane < (h + 1) * ATTN_HEAD_DIM)
             for h in range(HEADS_PER_SLAB)]
    _deinterleave(q_ref, tmp_ref, (qf0, qf1), (qc0, qc1), masks=masks)
    _deinterleave(k_ref, tmp_ref, (kf,), (kc,), masks=None)
    _deinterleave(v_ref, tmp_ref, (vf,), (vc,), masks=None)

    refs = (((qf0, qf1), kf, vf), ((qc0, qc1), kc, vc), acc_ref, m_ref, l_ref)
    stage = (s_ref, p_ref, alpha_ref)
    patterns = sorted(zip(DILATED_PATTERNS, (b1_ref, b4_ref, b16_ref)), key=lambda e: -e[0][1])
    for idx, ((_, d), bias_ref) in enumerate(patterns):
        nb = seq // (d * blk)
        run = functools.partial(_attn_group, refs, stage, bias_ref, d, init=(idx == 0))
        if d < grp:
            per_res = nb // grp

            def body(i, carry, run=run, per_res=per_res):
                r = i // per_res
                n0 = (i % per_res) * grp
                run([(r, n0 + j) for j in range(grp)])
                return carry

            lax.fori_loop(0, d * per_res, body, 0)
        else:
            per_blk = d // grp

            def body(i, carry, run=run, per_blk=per_blk):
                n = i // per_blk
                r0 = (i % per_blk) * grp
                run([(r0 + j, n) for j in range(grp)])
                return carry

            lax.fori_loop(0, nb * per_blk, body, 0)

    per = seq // FINE

    def out_body(i, carry):
        res = i // (per // OUT_ROWS)
        off = (i % (per // OUT_ROWS)) * OUT_ROWS
        rows = pl.ds(pl.multiple_of(res * per + off, OUT_ROWS), OUT_ROWS)
        tmp_ref[pl.ds(res + FINE * off, OUT_ROWS, stride=FINE), :] = acc_ref[rows, :] / l_ref[rows, :]
        return carry

    lax.fori_loop(0, FINE * (per // OUT_ROWS), out_body, 0)

    def cast_body(i, carry):
        rows = pl.ds(pl.multiple_of(i * OUT_ROWS, OUT_ROWS), OUT_ROWS)
        o_ref[rows, :] = tmp_ref[rows, :].astype(o_ref.dtype)
        return carry

    lax.fori_loop(0, seq // OUT_ROWS, cast_body, 0)


def _attn_bias(window, dilation):
    slopes = np.array([2.0 ** (-8.0 * (i + 1) / ATTN_HEADS) for i in range(ATTN_HEADS)],
                      dtype=np.float32)
    qi = _tile_order(dilation, ATTN_BLOCK)[:, None]
    kj = _tile_order(dilation, 2 * ATTN_BLOCK)[None, :]
    tables = []
    for steps in (qi - kj, qi + ATTN_BLOCK - kj):
        valid = (steps >= 0) & (steps <= window // dilation)
        bias = -slopes[:, None, None] * (steps * dilation).astype(np.float32) * np.float32(LOG2_E)
        tables.append(np.where(valid[None], bias, np.float32(MASK_VALUE)))
    table = np.stack(tables).astype(np.float32)
    table = table.reshape(2, ATTN_HEADS // HEADS_PER_SLAB, HEADS_PER_SLAB, *table.shape[2:])
    return jnp.asarray(table.transpose(1, 0, 2, 3, 4))


def _attention(q, k, v, *, batch, seq):
    slabs = ATTN_WIDTH // LANES
    per_seq = pl.BlockSpec((seq, LANES), lambda bi, j: (bi, j))
    bias = [_attn_bias(w, d) for w, d in DILATED_PATTERNS]
    bias_spec = pl.BlockSpec((None,) + bias[0].shape[1:], lambda bi, j: (j, 0, 0, 0, 0))
    rows16 = pltpu.VMEM((seq, LANES), BF16)
    rows32 = pltpu.VMEM((seq, LANES), F32)
    tiles = ATTN_GROUP * HEADS_PER_SLAB
    return pl.pallas_call(
        _attn_kernel,
        grid=(batch, slabs),
        in_specs=[per_seq, per_seq, per_seq, bias_spec, bias_spec, bias_spec],
        out_specs=per_seq,
        out_shape=jax.ShapeDtypeStruct((batch * seq, ATTN_WIDTH), BF16),
        scratch_shapes=[rows16] * 8 + [rows32] * 4 + [
            pltpu.VMEM((tiles, ATTN_BLOCK, 2 * ATTN_BLOCK), F32),
            pltpu.VMEM((tiles, ATTN_BLOCK, 2 * ATTN_BLOCK), BF16),
            pltpu.VMEM((ATTN_GROUP, ATTN_BLOCK, LANES), F32),
        ],
        compiler_params=_params("arbitrary", "arbitrary"),
        name="attention",
    )(q, k, v, *bias)


def _bmm(a, b):
    return lax.dot_general(a.astype(BF16), b.astype(BF16), (((2,), (1,)), ((0,), (0,))),
                           preferred_element_type=F32)


def _bmm_nt(a, b):
    return lax.dot_general(a.astype(BF16), b.astype(BF16), (((2,), (2,)), ((0,), (0,))),
                           preferred_element_type=F32)


MASK_EYE, MASK_CAUSAL, MASK_STRICT, MASK_UPPER, MASK_JOIN = 0, 1, 2, 3, 4


def _dn_masks():
    c = DN_CHUNK
    row, col = np.indices((c, c))
    tables = [row == col, row >= col, row > col, row <= col]
    size = 1
    while size < c:
        shift = size.bit_length() - 1
        tables.append(((row >> (shift + 1)) == (col >> (shift + 1)))
                      & (((row >> shift) & 1) == 1) & (((col >> shift) & 1) == 0))
        size *= 2
    return jnp.asarray(np.stack(tables).astype(np.float32))


def _dn_spread(offset):
    e = np.zeros((LANES, DN_WIDTH), np.float32)
    for hd in range(DN_HEADS):
        e[offset + hd, hd * DN_HEAD_DIM:(hd + 1) * DN_HEAD_DIM] = 1.0
    return jnp.asarray(e, dtype=BF16)


def _unit_lower_inverse(a_strict, mask_ref):
    levels = mask_ref.shape[0] - MASK_JOIN
    t = mask_ref[MASK_EYE] - a_strict * mask_ref[MASK_JOIN]
    for level in range(1, levels):
        lk = a_strict * mask_ref[MASK_JOIN + level]
        t = t - _bmm(t, _bmm(lk, t))
    return t


def _dn_kernel(q_ref, k_ref, v_ref, gate_ref, small_ref, arow_ref, dtrow_ref, nrm_ref, mask_ref,
               ebeta_ref, eg_ref, o_ref, s_ref, wq_ref, u_ref, at_ref, kd_ref):
    c = DN_CHUNK
    dh = DN_HEAD_DIM
    batch, tc = q_ref.shape[:2]
    g_chunks = tc // c
    lanes = batch * DN_HEADS
    nb = g_chunks * lanes

    @pl.when(pl.program_id(0) == 0)
    def _():
        s_ref[...] = jnp.zeros(s_ref.shape, F32)

    sm = small_ref[...]
    beta_all = jax.nn.sigmoid(sm)
    z = sm + dtrow_ref[...]
    softplus = jnp.maximum(z, 0.0) + jnp.log1p(jnp.exp(-jnp.abs(z)))
    g_all = -jnp.exp(arow_ref[...]) * softplus

    def per_head(ref):
        parts = [ref[b, :, hd * dh:(hd + 1) * dh].reshape(g_chunks, 1, c, dh)
                 for b in range(batch) for hd in range(DN_HEADS)]
        return jnp.concatenate(parts, axis=1).reshape(nb, c, dh)

    def split3(x):
        hi = x.astype(BF16)
        rest = x - hi.astype(F32)
        mid = rest.astype(BF16)
        return hi, mid, (rest - mid.astype(F32)).astype(BF16)

    def spread(x, e_ref):
        flat = x.reshape(batch * tc, LANES)
        wide = sum(jnp.dot(part, e_ref[...], preferred_element_type=F32) for part in split3(flat))
        parts = [wide[b * tc:(b + 1) * tc, hd * dh:(hd + 1) * dh].reshape(g_chunks, 1, c, dh)
                 for b in range(batch) for hd in range(DN_HEADS)]
        return jnp.concatenate(parts, axis=1).reshape(nb, c, dh)

    q = per_head(q_ref)
    k = per_head(k_ref)
    v = per_head(v_ref)

    causal = mask_ref[MASK_CAUSAL]

    chunks = batch * g_chunks
    tri = jnp.broadcast_to(causal.astype(BF16), (chunks, c, c))
    gc_all = sum(_bmm(tri, part) for part in split3(g_all.reshape(chunks, c, LANES)))
    beta = spread(beta_all, ebeta_ref)
    gc = spread(gc_all.reshape(batch, tc, LANES), eg_ref)
    gc_mat = gc[..., :c]
    gc_row = jnp.sum(gc_mat * mask_ref[MASK_EYE], axis=1, keepdims=True)
    decay = jnp.exp((gc_mat - gc_row) * causal) * causal

    k_beta = k * beta
    e_gc = jnp.exp(gc)
    g_last = gc[:, c - 1:c, :]
    both = _bmm_nt(jnp.concatenate([k_beta, q], axis=1), k)
    a_mat = both[:, :c] * (decay * mask_ref[MASK_STRICT])
    at_ref[...] = (both[:, c:] * decay).astype(BF16)
    rhs = jnp.concatenate([v * beta, k_beta * e_gc], axis=-1)
    sol = _bmm(_unit_lower_inverse(a_mat, mask_ref), rhs)
    u_ref[...] = sol[..., :dh]
    wq_ref[:, :c, :] = sol[..., dh:].astype(BF16)
    wq_ref[:, c:, :] = (q * e_gc).astype(BF16)
    kd_ref[...] = (k * jnp.exp(g_last - gc)).astype(BF16)
    e_last = jnp.exp(g_last)

    state = s_ref[...]
    outs = []
    for n in range(g_chunks):
        sl = slice(n * lanes, (n + 1) * lanes)
        ws = _bmm(wq_ref[sl], state)
        v_new = u_ref[sl] - ws[:, :c]
        outs.append(ws[:, c:] + _bmm(at_ref[sl], v_new))
        state = state * e_last[sl] + lax.dot_general(
            kd_ref[sl], v_new.astype(BF16), (((1,), (1,)), ((0,), (0,))),
            preferred_element_type=F32)
    s_ref[...] = state

    for b in range(batch):
        for hd in range(DN_HEADS):
            sl = slice(hd * dh, (hd + 1) * dh)
            o = jnp.concatenate([outs[n][b * DN_HEADS + hd] for n in range(g_chunks)], axis=0)
            o = o * lax.rsqrt(jnp.mean(o * o, axis=-1, keepdims=True) + NORM_EPS)
            o_ref[b, :, sl] = (o * nrm_ref[...] * gate_ref[b, :, sl]).astype(o_ref.dtype)


def _deltanet(dq, dk, dv, gate, small, arow, dtrow, nrm, *, batch, seq):
    tc = DN_TOKENS
    nb = (tc // DN_CHUNK) * batch * DN_HEADS
    masks = _dn_masks()
    spread_beta = _dn_spread(0)
    spread_g = _dn_spread(DN_HEADS)
    view = lambda t: t.reshape(batch, seq, t.shape[-1])
    step = lambda j: (0, j, 0)
    wide = pl.BlockSpec((batch, tc, DN_WIDTH), step)
    out = pl.pallas_call(
        _dn_kernel,
        grid=(seq // tc,),
        in_specs=[wide, wide, wide, wide, pl.BlockSpec((batch, tc, LANES), step),
                  _resident((1, LANES)), _resident((1, LANES)), _resident((1, DN_HEAD_DIM)),
                  _resident(masks.shape), _resident(spread_beta.shape),
                  _resident(spread_g.shape)],
        out_specs=wide,
        out_shape=jax.ShapeDtypeStruct((batch, seq, DN_WIDTH), BF16),
        scratch_shapes=[
            pltpu.VMEM((batch * DN_HEADS, DN_HEAD_DIM, DN_HEAD_DIM), F32),
            pltpu.VMEM((nb, 2 * DN_CHUNK, DN_HEAD_DIM), BF16),
            pltpu.VMEM((nb, DN_CHUNK, DN_HEAD_DIM), F32),
            pltpu.VMEM((nb, DN_CHUNK, DN_CHUNK), BF16),
            pltpu.VMEM((nb, DN_CHUNK, DN_HEAD_DIM), BF16),
        ],
        compiler_params=_params("arbitrary"),
        name="deltanet",
    )(view(dq), view(dk), view(dv), view(gate), view(small), arow, dtrow, nrm, masks,
      spread_beta, spread_g)
    return out.reshape(batch * seq, DN_WIDTH)


def _lane_row(values, offset):
    return jnp.zeros((1, LANES), F32).at[0, offset:offset + values.shape[0]].set(values.astype(F32))


def kernel(x, norm_ffn1, ffn1_gate, ffn1_up, ffn1_down, norm_mix, w_in, conv_w, a_log, dt_bias,
           dn_norm, w_out, norm_ffn2, ffn2_gate, ffn2_up, ffn2_down, norm_final):
    batch, seq, d = x.shape
    depth = norm_ffn1.shape[0]
    assert seq % (max(dl for _, dl in DILATED_PATTERNS) * ATTN_BLOCK) == 0
    assert (batch * seq) % FFN_TILE == 0
    assert seq % TOKEN_TILE == 0 and seq % DN_TOKENS == 0 and seq % OUT_ROWS == 0
    xt = x.reshape(batch * seq, d)
    gfin = norm_final.reshape(1, d).astype(F32)

    for i in range(depth):
        row = lambda g: g.reshape(1, -1).astype(F32)
        xt = _ffn_in(xt, row(norm_ffn1[i]), ffn1_gate[i], ffn1_up[i], ffn1_down[i])
        aq, ak, av, dq, dk, dv, gate, small = _inproj(
            xt, row(norm_mix[i]), w_in[i], conv_w[i].astype(F32), seq=seq)

        attn = _attention(aq, ak, av, batch=batch, seq=seq)
        dn = _deltanet(dq, dk, dv, gate, small, _lane_row(a_log[i], DN_HEADS),
                       _lane_row(dt_bias[i], DN_HEADS), row(dn_norm[i]), batch=batch, seq=seq)

        xt = _ffn_out(xt, attn, dn, w_out[i], row(norm_ffn2[i]), ffn2_gate[i], ffn2_up[i],
                      ffn2_down[i], gfin, final=(i == depth - 1))

    return xt.reshape(batch, seq, d)
```

```python
import functools

import numpy as np
import jax
import jax.numpy as jnp
from jax import lax
from jax.experimental import pallas as pl
from jax.experimental.pallas import tpu as pltpu

F32 = jnp.float32
BF16 = jnp.bfloat16

ATTN_HEADS = 8
ATTN_HEAD_DIM = 64
ATTN_WIDTH = ATTN_HEADS * ATTN_HEAD_DIM
DILATED_PATTERNS = ((128, 1), (512, 4), (2048, 16))
ATTN_BLOCK = 128
DN_HEADS = 4
DN_HEAD_DIM = 128
DN_WIDTH = DN_HEADS * DN_HEAD_DIM
DN_CHUNK = 64
CONV_WIDTH = 4
NORM_EPS = 1e-6
L2_EPS = 1e-6

LANES = 128
SUBLANES = 8
VMEM_LIMIT_BYTES = 56 * 1024 * 1024

MASK_VALUE = -1e30
LOG2_E = 1.4426950408889634

TOKEN_TILE = 512
FFN_TILE = 1024
FF_TILE = 256
DN_TOKENS = 128
HEADS_PER_SLAB = LANES // ATTN_HEAD_DIM
OUT_ROWS = 512
ATTN_GROUP = 8
DEINTERLEAVE_ROWS = 256
WEIGHT_ROWS = 64
WEIGHT_SLOTS = 6


def _rms(x):
    return x * lax.rsqrt(jnp.mean(x * x, axis=-1, keepdims=True) + NORM_EPS)


def _silu(x):
    half = 0.5 * x
    return half + half * jnp.tanh(half)


def _mm(a, b):
    return jnp.dot(a.astype(BF16), b.astype(BF16), preferred_element_type=F32)


def _resident(shape):
    nd = len(shape)
    return pl.BlockSpec(shape, lambda *_: (0,) * nd, pipeline_mode=pl.Buffered(1))


def _params(*sem):
    return pltpu.CompilerParams(dimension_semantics=sem, vmem_limit_bytes=VMEM_LIMIT_BYTES)


def _load_rows(src_ref, stage_ref, sem_ref, emit):
    slots, rows = stage_ref.shape[:2]
    chunks = src_ref.shape[0] // rows
    ahead = slots - 1
    assert src_ref.shape[0] % rows == 0 and src_ref.shape[1] == stage_ref.shape[2]

    def copy(c):
        slot = c % slots
        return pltpu.make_async_copy(src_ref.at[pl.ds(c * rows, rows), :], stage_ref.at[slot],
                                     sem_ref.at[slot])

    for c in range(min(ahead, chunks)):
        copy(c).start()

    def step(c, carry):
        @pl.when(c + ahead < chunks)
        def _():
            copy(c + ahead).start()

        copy(c).wait()
        emit(pl.ds(pl.multiple_of(c * rows, rows), rows), stage_ref[c % slots])
        return carry

    lax.fori_loop(0, chunks, step, 0)


def _load_bf16(src_ref, dst_ref, stage_ref, sem_ref):
    def emit(rows, chunk):
        dst_ref[rows, :] = chunk.astype(BF16)

    _load_rows(src_ref, stage_ref, sem_ref, emit)


def _any():
    return pl.BlockSpec(memory_space=pl.ANY)


def _swiglu_step(x, gain_ref, wg_ref, wu_ref, wd_ref, a_ref):
    h = (_rms(x) * gain_ref[...]).astype(BF16)
    d_ff = wg_ref.shape[1]
    for f0 in range(0, d_ff, FF_TILE):
        g = jnp.dot(h, wg_ref[:, f0:f0 + FF_TILE], preferred_element_type=F32)
        u = jnp.dot(h, wu_ref[:, f0:f0 + FF_TILE], preferred_element_type=F32)
        a_ref[:, f0:f0 + FF_TILE] = (_silu(g) * u).astype(BF16)
    return x + 0.5 * jnp.dot(a_ref[...], wd_ref[...], preferred_element_type=F32)


def _swiglu_scratch(tm, d, d_ff):
    return [pltpu.VMEM((tm, d_ff), BF16),
            pltpu.VMEM((d, d_ff), BF16), pltpu.VMEM((d, d_ff), BF16), pltpu.VMEM((d_ff, d), BF16),
            pltpu.VMEM((WEIGHT_SLOTS, WEIGHT_ROWS, d_ff), F32),
            pltpu.VMEM((WEIGHT_SLOTS, WEIGHT_ROWS, d), F32),
            pltpu.SemaphoreType.DMA((WEIGHT_SLOTS,))]


def _load_swiglu_weights(wg_hbm, wu_hbm, wd_hbm, wg_ref, wu_ref, wd_ref, wide_ref, narrow_ref, sem_ref):
    _load_bf16(wg_hbm, wg_ref, wide_ref, sem_ref)
    _load_bf16(wu_hbm, wu_ref, wide_ref, sem_ref)
    _load_bf16(wd_hbm, wd_ref, narrow_ref, sem_ref)


def _ffn_out_kernel(x_ref, attn_ref, dn_ref, wo_ref, gain_ref, wg_ref, wu_ref, wd_ref, gfin_ref,
                    o_ref, a_ref, *, final):
    width = attn_ref.shape[1]
    x = x_ref[...] + jnp.dot(attn_ref[...], wo_ref[:width, :], preferred_element_type=F32)
    x = x + jnp.dot(dn_ref[...], wo_ref[width:, :], preferred_element_type=F32)
    out = _swiglu_step(x, gain_ref, wg_ref, wu_ref, wd_ref, a_ref)
    if final:
        out = _rms(out) * gfin_ref[...]
    o_ref[...] = out


def _ffn_out(x, attn, dn, w_out, gain, wg, wu, wd, gfin, *, final):
    t, d = x.shape
    d_ff = wg.shape[1]
    tm = FFN_TILE
    row = lambda i: (i, 0)
    assert w_out.shape == (attn.shape[1] + dn.shape[1], d)
    return pl.pallas_call(
        functools.partial(_ffn_out_kernel, final=final),
        grid=(t // tm,),
        in_specs=[pl.BlockSpec((tm, d), row), pl.BlockSpec((tm, attn.shape[1]), row),
                  pl.BlockSpec((tm, dn.shape[1]), row), _resident(w_out.shape), _resident((1, d)),
                  _resident(wg.shape), _resident(wu.shape), _resident(wd.shape),
                  _resident((1, d))],
        out_specs=pl.BlockSpec((tm, d), row),
        out_shape=jax.ShapeDtypeStruct((t, d), F32),
        scratch_shapes=[pltpu.VMEM((tm, d_ff), BF16)],
        compiler_params=_params("arbitrary"),
        name="ffn_out",
    )(x, attn, dn, w_out, gain, wg, wu, wd, gfin)


def _ffn_in_kernel(x_ref, gain_ref, wg_hbm, wu_hbm, wd_hbm, o_ref,
                   a_ref, wg_ref, wu_ref, wd_ref, wide_ref, narrow_ref, sem_ref):
    @pl.when(pl.program_id(0) == 0)
    def _():
        _load_swiglu_weights(wg_hbm, wu_hbm, wd_hbm, wg_ref, wu_ref, wd_ref, wide_ref, narrow_ref,
                             sem_ref)

    o_ref[...] = _swiglu_step(x_ref[...], gain_ref, wg_ref, wu_ref, wd_ref, a_ref)


def _ffn_in(x, gain, wg, wu, wd):
    t, d = x.shape
    d_ff = wg.shape[1]
    tm = FFN_TILE
    row = lambda i: (i, 0)
    return pl.pallas_call(
        _ffn_in_kernel,
        grid=(t // tm,),
        in_specs=[pl.BlockSpec((tm, d), row), _resident((1, d)), _any(), _any(), _any()],
        out_specs=pl.BlockSpec((tm, d), row),
        out_shape=jax.ShapeDtypeStruct((t, d), F32),
        scratch_shapes=_swiglu_scratch(tm, d, d_ff),
        compiler_params=_params("arbitrary"),
        name="ffn_in",
    )(x, gain, wg, wu, wd)


def _inproj_kernel(x_ref, gain_ref, w_hbm, cw_ref,
                   aq_ref, ak_ref, av_ref, dq_ref, dk_ref, dv_ref, gate_ref, small_ref,
                   xc_ref, wm_ref, ws_ref, stage_ref, sem_ref, *, tiles_per_seq, small_lo, small_hi):
    i = pl.program_id(0)
    tm = x_ref.shape[0]
    halo = SUBLANES

    @pl.when(i == 0)
    def _():
        def emit(rows, chunk):
            wm_ref[rows, :] = jnp.concatenate(
                [chunk[:, :small_lo], chunk[:, small_hi:]], axis=1).astype(BF16)
            lane = lax.broadcasted_iota(jnp.int32, (chunk.shape[0], LANES), 1)
            small = jnp.where(lane < small_hi - small_lo, chunk[:, small_lo:small_lo + LANES], 0.0)
            ws_ref[rows, :] = small.astype(BF16)

        _load_rows(w_hbm, stage_ref, sem_ref, emit)

    @pl.when(i % tiles_per_seq == 0)
    def _():
        xc_ref[0:halo, :] = jnp.zeros((halo, xc_ref.shape[1]), F32)

    @pl.when(i % tiles_per_seq != 0)
    def _():
        xc_ref[0:halo, :] = xc_ref[tm:tm + halo, :]

    h = (_rms(x_ref[...]) * gain_ref[...]).astype(BF16)

    def section(j, width=ATTN_WIDTH):
        return jnp.dot(h, wm_ref[:, j * width:(j + 1) * width], preferred_element_type=F32)

    for j in range(3):
        xc_ref[halo:halo + tm, j * DN_WIDTH:(j + 1) * DN_WIDTH] = section(3 + j)

    outs = (dq_ref, dk_ref, dv_ref)
    sub = lax.broadcasted_iota(jnp.int32, (halo, DN_HEAD_DIM), 0)
    for c in range(3 * DN_HEADS):
        sl = slice(c * DN_HEAD_DIM, (c + 1) * DN_HEAD_DIM)
        cur = xc_ref[halo:halo + tm, sl]
        tail = xc_ref[0:halo, sl]
        conv = None
        for j in range(CONV_WIDTH):
            back = CONV_WIDTH - 1 - j
            if back:
                shifted = pltpu.roll(cur, back, 0)
                head = jnp.where(sub < back, pltpu.roll(tail, back, 0), shifted[0:halo])
                shifted = jnp.concatenate([head, shifted[halo:]], axis=0)
            else:
                shifted = cur
            term = shifted * cw_ref[j:j + 1, sl]
            conv = term if conv is None else conv + term
        a = _silu(conv)
        if c < 2 * DN_HEADS:
            a = a * lax.rsqrt(jnp.sum(a * a, axis=-1, keepdims=True) + L2_EPS)
        if c < DN_HEADS:
            a = a * (DN_HEAD_DIM ** -0.5)
        hs = slice((c % DN_HEADS) * DN_HEAD_DIM, (c % DN_HEADS + 1) * DN_HEAD_DIM)
        outs[c // DN_HEADS][:, hs] = a

    aq_ref[...] = section(0) * (ATTN_HEAD_DIM ** -0.5 * LOG2_E)
    ak_ref[...] = section(1)
    av_ref[...] = section(2)
    gate_ref[...] = _silu(section(6))
    small_ref[...] = jnp.dot(h, ws_ref[...], preferred_element_type=F32)


def _inproj(x, gain, w_in, conv_w, *, seq):
    t, d = x.shape
    tm = TOKEN_TILE
    row = lambda i: (i, 0)
    wide = jax.ShapeDtypeStruct((t, ATTN_WIDTH), F32)
    small_lo = 6 * ATTN_WIDTH
    small_hi = small_lo + 2 * DN_HEADS
    cols = w_in.shape[1]
    assert small_lo % LANES == 0 and cols - small_hi == DN_WIDTH
    return pl.pallas_call(
        functools.partial(_inproj_kernel, tiles_per_seq=seq // tm, small_lo=small_lo,
                          small_hi=small_hi),
        grid=(t // tm,),
        in_specs=[pl.BlockSpec((tm, d), row), _resident((1, d)), _any(),
                  _resident(conv_w.shape)],
        out_specs=[pl.BlockSpec((tm, ATTN_WIDTH), row)] * 7 + [pl.BlockSpec((tm, LANES), row)],
        out_shape=[wide] * 7 + [jax.ShapeDtypeStruct((t, LANES), F32)],
        scratch_shapes=[pltpu.VMEM((tm + 2 * SUBLANES, 3 * DN_WIDTH), F32),
                        pltpu.VMEM((d, cols - (small_hi - small_lo)), BF16),
                        pltpu.VMEM((d, LANES), BF16),
                        pltpu.VMEM((WEIGHT_SLOTS, WEIGHT_ROWS, cols), F32),
                        pltpu.SemaphoreType.DMA((WEIGHT_SLOTS,))],
        compiler_params=_params("arbitrary"),
        name="inproj",
    )(x, gain, w_in, conv_w)


FINE = 8
COARSE = 16


def _chunks(dilation):
    return FINE // dilation


def _tile_order(dilation, size):
    c = _chunks(dilation) if dilation < FINE else 1
    per = size // c
    x = np.arange(size)
    return c * (x % per) + x // per


def _attn_group(refs, stage, bias_ref, dilation, blocks, *, init):
    fine, coarse, acc_ref, m_ref, l_ref = refs
    s_ref, p_ref, alpha_ref = stage
    d = dilation
    blk = ATTN_BLOCK
    seq = acc_ref.shape[0]
    lane = lax.broadcasted_iota(jnp.int32, (blk, LANES), 1)
    head_lanes = [(lane >= h * ATTN_HEAD_DIM) & (lane < (h + 1) * ATTN_HEAD_DIM)
                  for h in range(HEADS_PER_SLAB)]

    def gather(ref, starts, rows):
        parts = [ref[pl.ds(pl.multiple_of(st, 16), rows), :] for st in starts]
        return parts[0] if len(parts) == 1 else jnp.concatenate(parts, axis=0)

    plans = []
    for r, n in blocks:
        n_key = jnp.maximum(n - 1, 0)
        if d < FINE:
            c = _chunks(d)
            per = seq // FINE
            bases = [(r + d * b) * per for b in range(c)]
            q_starts = [base + (blk // c) * n for base in bases]
            k_starts = [base + (blk // c) * n_key for base in bases]
            q_refs, k_ref, v_ref = fine
            state = [(st, blk // c, 1) for st in q_starts]
        else:
            c = 1
            per = seq // COARSE
            q_starts = [r * per + blk * n]
            k_starts = [r * per + blk * n_key]
            q_refs, k_ref, v_ref = coarse
            ratio = d // FINE
            state = [((r % FINE) * (seq // FINE) + ratio * blk * n + r // FINE, blk, ratio)]
        plans.append((q_refs, k_ref, v_ref, q_starts, k_starts, c, state, jnp.minimum(n, 1)))

    def state_rows(ref, state):
        parts = []
        for start, rows, stride in state:
            if stride == 1:
                parts.append(ref[pl.ds(pl.multiple_of(start, 16), rows), :])
            else:
                parts.append(ref[pl.ds(start, rows, stride=stride), :])
        return parts[0] if len(parts) == 1 else jnp.concatenate(parts, axis=0)

    for i, (q_refs, k_ref, _, q_starts, k_starts, c, _, _) in enumerate(plans):
        kb = gather(k_ref, k_starts, 2 * blk // c)
        for h in range(HEADS_PER_SLAB):
            qh = gather(q_refs[h], q_starts, blk // c)
            s_ref[i * HEADS_PER_SLAB + h] = lax.dot_general(
                qh, kb, (((1,), (1,)), ((), ())), preferred_element_type=F32)

    pending = []
    for i, plan in enumerate(plans):
        state, table = plan[6], plan[7]
        m_prev = None if init else state_rows(m_ref, state)
        m_heads, l_heads = [], []
        for h in range(HEADS_PER_SLAB):
            t = i * HEADS_PER_SLAB + h
            s = s_ref[t] + bias_ref[table, h]
            top = jnp.maximum(s[:, :LANES], s[:, LANES:])
            if not init:
                top = jnp.maximum(top, jnp.where(head_lanes[h], m_prev, MASK_VALUE))
            m_new = jnp.broadcast_to(jnp.max(top, axis=-1, keepdims=True), (blk, LANES))
            p = jnp.exp2(s - jnp.concatenate([m_new, m_new], axis=1))
            l_heads.append(jnp.broadcast_to(jnp.sum(p, axis=-1, keepdims=True), (blk, LANES)))
            m_heads.append(m_new)
            p_ref[t] = p.astype(BF16)
        m_new = jnp.where(head_lanes[0], m_heads[0], m_heads[1])
        l_new = jnp.where(head_lanes[0], l_heads[0], l_heads[1])
        if not init:
            alpha = jnp.exp2(m_prev - m_new)
            alpha_ref[i] = alpha
            l_new = state_rows(l_ref, state) * alpha + l_new
        pending.append((m_ref, state, m_new))
        pending.append((l_ref, state, l_new))

    for i, plan in enumerate(plans):
        v_ref, k_starts, c, state = plan[2], plan[4], plan[5], plan[6]
        vb = gather(v_ref, k_starts, 2 * blk // c)
        pv = [jnp.dot(p_ref[i * HEADS_PER_SLAB + h], vb, preferred_element_type=F32)
              for h in range(HEADS_PER_SLAB)]
        acc_new = jnp.where(head_lanes[0], pv[0], pv[1])
        if not init:
            acc_new = state_rows(acc_ref, state) * alpha_ref[i] + acc_new
        pending.append((acc_ref, state, acc_new))

    for ref, state, value in pending:
        offset = 0
        for start, rows, stride in state:
            piece = value[offset:offset + rows]
            if stride == 1:
                ref[pl.ds(pl.multiple_of(start, 16), rows), :] = piece
            else:
                ref[pl.ds(start, rows, stride=stride), :] = piece
            offset += rows


def _deinterleave(src_ref, tmp_ref, dst_fine, dst_coarse, *, masks):
    seq = src_ref.shape[0]
    quarter = seq // 4
    piece = DEINTERLEAVE_ROWS

    for r4 in range(4):
        for off in range(0, quarter, piece):
            tmp_ref[r4 * quarter + off:r4 * quarter + off + piece, :] = \
                src_ref[pl.ds(r4 + 4 * off, piece, stride=4), :]

    for layout, dsts in ((FINE, dst_fine), (COARSE, dst_coarse)):
        per = seq // layout
        sub = layout // 4
        for res in range(layout):
            for off in range(0, per, piece):
                value = tmp_ref[pl.ds((res % 4) * quarter + res // 4 + sub * off, piece,
                                      stride=sub), :]
                rows = slice(res * per + off, res * per + off + piece)
                if masks is None:
                    dsts[0][rows, :] = value.astype(BF16)
                else:
                    for dst, mask in zip(dsts, masks):
                        dst[rows, :] = jnp.where(mask, value, 0.0).astype(BF16)


def _attn_kernel(q_ref, k_ref, v_ref, b1_ref, b4_ref, b16_ref, *rest, riders):
    for src_ref, dst_ref in zip(rest[:riders], rest[riders + 1:2 * riders + 1]):
        dst_ref[...] = src_ref[...].astype(dst_ref.dtype)
    o_ref = rest[riders]
    (qf0, qf1, kf, vf, qc0, qc1, kc, vc, tmp_ref, acc_ref, m_ref, l_ref,
     s_ref, p_ref, alpha_ref) = rest[2 * riders + 1:]
    seq = q_ref.shape[0]
    blk = ATTN_BLOCK
    grp = ATTN_GROUP
    lane = lax.broadcasted_iota(jnp.int32, (DEINTERLEAVE_ROWS, LANES), 1)
    masks = [(lane >= h * ATTN_HEAD_DIM) & (lane < (h + 1) * ATTN_HEAD_DIM)
             for h in range(HEADS_PER_SLAB)]
    _deinterleave(q_ref, tmp_ref, (qf0, qf1), (qc0, qc1), masks=masks)
    _deinterleave(k_ref, tmp_ref, (kf,), (kc,), masks=None)
    _deinterleave(v_ref, tmp_ref, (vf,), (vc,), masks=None)

    refs = (((qf0, qf1), kf, vf), ((qc0, qc1), kc, vc), acc_ref, m_ref, l_ref)
    stage = (s_ref, p_ref, alpha_ref)
    patterns = sorted(zip(DILATED_PATTERNS, (b1_ref, b4_ref, b16_ref)), key=lambda e: -e[0][1])
    for idx, ((_, d), bias_ref) in enumerate(patterns):
        nb = seq // (d * blk)
        run = functools.partial(_attn_group, refs, stage, bias_ref, d, init=(idx == 0))
        if d < grp:
            per_res = nb // grp

            def body(i, carry, run=run, per_res=per_res):
                r = i // per_res
                n0 = (i % per_res) * grp
                run([(r, n0 + j) for j in range(grp)])
                return carry

            lax.fori_loop(0, d * per_res, body, 0)
        else:
            per_blk = d // grp

            def body(i, carry, run=run, per_blk=per_blk):
                n = i // per_blk
                r0 = (i % per_blk) * grp
                run([(r0 + j, n) for j in range(grp)])
                return carry

            lax.fori_loop(0, nb * per_blk, body, 0)

    per = seq // FINE

    def out_body(i, carry):
        res = i // (per // OUT_ROWS)
        off = (i % (per // OUT_ROWS)) * OUT_ROWS
        rows = pl.ds(pl.multiple_of(res * per + off, OUT_ROWS), OUT_ROWS)
        tmp_ref[pl.ds(res + FINE * off, OUT_ROWS, stride=FINE), :] = acc_ref[rows, :] / l_ref[rows, :]
        return carry

    lax.fori_loop(0, FINE * (per // OUT_ROWS), out_body, 0)

    def cast_body(i, carry):
        rows = pl.ds(pl.multiple_of(i * OUT_ROWS, OUT_ROWS), OUT_ROWS)
        o_ref[rows, :] = tmp_ref[rows, :].astype(o_ref.dtype)
        return carry

    lax.fori_loop(0, seq // OUT_ROWS, cast_body, 0)


def _attn_bias(window, dilation):
    slopes = np.array([2.0 ** (-8.0 * (i + 1) / ATTN_HEADS) for i in range(ATTN_HEADS)],
                      dtype=np.float32)
    qi = _tile_order(dilation, ATTN_BLOCK)[:, None]
    kj = _tile_order(dilation, 2 * ATTN_BLOCK)[None, :]
    tables = []
    for steps in (qi - kj, qi + ATTN_BLOCK - kj):
        valid = (steps >= 0) & (steps <= window // dilation)
        bias = -slopes[:, None, None] * (steps * dilation).astype(np.float32) * np.float32(LOG2_E)
        tables.append(np.where(valid[None], bias, np.float32(MASK_VALUE)))
    table = np.stack(tables).astype(np.float32)
    table = table.reshape(2, ATTN_HEADS // HEADS_PER_SLAB, HEADS_PER_SLAB, *table.shape[2:])
    return jnp.asarray(table.transpose(1, 0, 2, 3, 4))


def _attention(q, k, v, weights, *, batch, seq):
    slabs = ATTN_WIDTH // LANES
    steps = batch * slabs
    rider = lambda w: pl.BlockSpec((w.shape[0] // steps, w.shape[1]),
                                   lambda bi, j: (bi * slabs + j, 0))
    assert all(w.shape[0] % (16 * steps) == 0 for w in weights)
    per_seq = pl.BlockSpec((seq, LANES), lambda bi, j: (bi, j))
    bias = [_attn_bias(w, d) for w, d in DILATED_PATTERNS]
    bias_spec = pl.BlockSpec((None,) + bias[0].shape[1:], lambda bi, j: (j, 0, 0, 0, 0))
    rows16 = pltpu.VMEM((seq, LANES), BF16)
    rows32 = pltpu.VMEM((seq, LANES), F32)
    tiles = ATTN_GROUP * HEADS_PER_SLAB
    return pl.pallas_call(
        functools.partial(_attn_kernel, riders=len(weights)),
        grid=(batch, slabs),
        in_specs=[per_seq, per_seq, per_seq, bias_spec, bias_spec, bias_spec]
        + [rider(w) for w in weights],
        out_specs=[per_seq] + [rider(w) for w in weights],
        out_shape=[jax.ShapeDtypeStruct((batch * seq, ATTN_WIDTH), BF16)]
        + [jax.ShapeDtypeStruct(w.shape, BF16) for w in weights],
        scratch_shapes=[rows16] * 8 + [rows32] * 4 + [
            pltpu.VMEM((tiles, ATTN_BLOCK, 2 * ATTN_BLOCK), F32),
            pltpu.VMEM((tiles, ATTN_BLOCK, 2 * ATTN_BLOCK), BF16),
            pltpu.VMEM((ATTN_GROUP, ATTN_BLOCK, LANES), F32),
        ],
        compiler_params=_params("arbitrary", "arbitrary"),
        name="attention",
    )(q, k, v, *bias, *weights)


def _bmm(a, b):
    return lax.dot_general(a.astype(BF16), b.astype(BF16), (((2,), (1,)), ((0,), (0,))),
                           preferred_element_type=F32)


def _bmm_nt(a, b):
    return lax.dot_general(a.astype(BF16), b.astype(BF16), (((2,), (2,)), ((0,), (0,))),
                           preferred_element_type=F32)


MASK_EYE, MASK_CAUSAL, MASK_STRICT, MASK_UPPER, MASK_JOIN = 0, 1, 2, 3, 4


def _dn_masks():
    c = DN_CHUNK
    row, col = np.indices((c, c))
    tables = [row == col, row >= col, row > col, row <= col]
    size = 1
    while size < c:
        shift = size.bit_length() - 1
        tables.append(((row >> (shift + 1)) == (col >> (shift + 1)))
                      & (((row >> shift) & 1) == 1) & (((col >> shift) & 1) == 0))
        size *= 2
    return jnp.asarray(np.stack(tables).astype(np.float32))


def _dn_spread(offset):
    e = np.zeros((LANES, DN_WIDTH), np.float32)
    for hd in range(DN_HEADS):
        e[offset + hd, hd * DN_HEAD_DIM:(hd + 1) * DN_HEAD_DIM] = 1.0
    return jnp.asarray(e, dtype=BF16)


def _unit_lower_inverse(a_strict, mask_ref):
    levels = mask_ref.shape[0] - MASK_JOIN
    t = mask_ref[MASK_EYE] - a_strict * mask_ref[MASK_JOIN]
    for level in range(1, levels):
        lk = a_strict * mask_ref[MASK_JOIN + level]
        t = t - _bmm(t, _bmm(lk, t))
    return t


def _dn_kernel(q_ref, k_ref, v_ref, gate_ref, small_ref, arow_ref, dtrow_ref, nrm_ref, mask_ref,
               ebeta_ref, eg_ref, o_ref, s_ref, wq_ref, u_ref, at_ref, kd_ref):
    c = DN_CHUNK
    dh = DN_HEAD_DIM
    batch, tc = q_ref.shape[:2]
    g_chunks = tc // c
    lanes = batch * DN_HEADS
    nb = g_chunks * lanes

    @pl.when(pl.program_id(0) == 0)
    def _():
        s_ref[...] = jnp.zeros(s_ref.shape, F32)

    sm = small_ref[...]
    beta_all = jax.nn.sigmoid(sm)
    z = sm + dtrow_ref[...]
    softplus = jnp.maximum(z, 0.0) + jnp.log1p(jnp.exp(-jnp.abs(z)))
    g_all = -jnp.exp(arow_ref[...]) * softplus

    def per_head(ref):
        parts = [ref[b, :, hd * dh:(hd + 1) * dh].reshape(g_chunks, 1, c, dh)
                 for b in range(batch) for hd in range(DN_HEADS)]
        return jnp.concatenate(parts, axis=1).reshape(nb, c, dh)

    def split3(x):
        hi = x.astype(BF16)
        rest = x - hi.astype(F32)
        mid = rest.astype(BF16)
        return hi, mid, (rest - mid.astype(F32)).astype(BF16)

    def spread(x, e_ref):
        flat = x.reshape(batch * tc, LANES)
        wide = sum(jnp.dot(part, e_ref[...], preferred_element_type=F32) for part in split3(flat))
        parts = [wide[b * tc:(b + 1) * tc, hd * dh:(hd + 1) * dh].reshape(g_chunks, 1, c, dh)
                 for b in range(batch) for hd in range(DN_HEADS)]
        return jnp.concatenate(parts, axis=1).reshape(nb, c, dh)

    q = per_head(q_ref)
    k = per_head(k_ref)
    v = per_head(v_ref)

    causal = mask_ref[MASK_CAUSAL]

    chunks = batch * g_chunks
    tri = jnp.broadcast_to(causal.astype(BF16), (chunks, c, c))
    gc_all = sum(_bmm(tri, part) for part in split3(g_all.reshape(chunks, c, LANES)))
    beta = spread(beta_all, ebeta_ref)
    gc = spread(gc_all.reshape(batch, tc, LANES), eg_ref)
    gc_mat = gc[..., :c]
    gc_row = jnp.sum(gc_mat * mask_ref[MASK_EYE], axis=1, keepdims=True)
    decay = jnp.exp((gc_mat - gc_row) * causal) * causal

    k_beta = k * beta
    e_gc = jnp.exp(gc)
    g_last = gc[:, c - 1:c, :]
    both = _bmm_nt(jnp.concatenate([k_beta, q], axis=1), k)
    a_mat = both[:, :c] * (decay * mask_ref[MASK_STRICT])
    at_ref[...] = (both[:, c:] * decay).astype(BF16)
    rhs = jnp.concatenate([v * beta, k_beta * e_gc], axis=-1)
    sol = _bmm(_unit_lower_inverse(a_mat, mask_ref), rhs)
    u_ref[...] = sol[..., :dh]
    wq_ref[:, :c, :] = sol[..., dh:].astype(BF16)
    wq_ref[:, c:, :] = (q * e_gc).astype(BF16)
    kd_ref[...] = (k * jnp.exp(g_last - gc)).astype(BF16)
    e_last = jnp.exp(g_last)

    state = s_ref[...]
    outs = []
    for n in range(g_chunks):
        sl = slice(n * lanes, (n + 1) * lanes)
        ws = _bmm(wq_ref[sl], state)
        v_new = u_ref[sl] - ws[:, :c]
        outs.append(ws[:, c:] + _bmm(at_ref[sl], v_new))
        state = state * e_last[sl] + lax.dot_general(
            kd_ref[sl], v_new.astype(BF16), (((1,), (1,)), ((0,), (0,))),
            preferred_element_type=F32)
    s_ref[...] = state

    for b in range(batch):
        for hd in range(DN_HEADS):
            sl = slice(hd * dh, (hd + 1) * dh)
            o = jnp.concatenate([outs[n][b * DN_HEADS + hd] for n in range(g_chunks)], axis=0)
            o = o * lax.rsqrt(jnp.mean(o * o, axis=-1, keepdims=True) + NORM_EPS)
            o_ref[b, :, sl] = (o * nrm_ref[...] * gate_ref[b, :, sl]).astype(o_ref.dtype)


def _deltanet(dq, dk, dv, gate, small, arow, dtrow, nrm, *, batch, seq):
    tc = DN_TOKENS
    nb = (tc // DN_CHUNK) * batch * DN_HEADS
    masks = _dn_masks()
    spread_beta = _dn_spread(0)
    spread_g = _dn_spread(DN_HEADS)
    view = lambda t: t.reshape(batch, seq, t.shape[-1])
    step = lambda j: (0, j, 0)
    wide = pl.BlockSpec((batch, tc, DN_WIDTH), step)
    out = pl.pallas_call(
        _dn_kernel,
        grid=(seq // tc,),
        in_specs=[wide, wide, wide, wide, pl.BlockSpec((batch, tc, LANES), step),
                  _resident((1, LANES)), _resident((1, LANES)), _resident((1, DN_HEAD_DIM)),
                  _resident(masks.shape), _resident(spread_beta.shape),
                  _resident(spread_g.shape)],
        out_specs=wide,
        out_shape=jax.ShapeDtypeStruct((batch, seq, DN_WIDTH), BF16),
        scratch_shapes=[
            pltpu.VMEM((batch * DN_HEADS, DN_HEAD_DIM, DN_HEAD_DIM), F32),
            pltpu.VMEM((nb, 2 * DN_CHUNK, DN_HEAD_DIM), BF16),
            pltpu.VMEM((nb, DN_CHUNK, DN_HEAD_DIM), F32),
            pltpu.VMEM((nb, DN_CHUNK, DN_CHUNK), BF16),
            pltpu.VMEM((nb, DN_CHUNK, DN_HEAD_DIM), BF16),
        ],
        compiler_params=_params("arbitrary"),
        name="deltanet",
    )(view(dq), view(dk), view(dv), view(gate), view(small), arow, dtrow, nrm, masks,
      spread_beta, spread_g)
    return out.reshape(batch * seq, DN_WIDTH)


def _lane_row(values, offset):
    return jnp.zeros((1, LANES), F32).at[0, offset:offset + values.shape[0]].set(values.astype(F32))


def kernel(x, norm_ffn1, ffn1_gate, ffn1_up, ffn1_down, norm_mix, w_in, conv_w, a_log, dt_bias,
           dn_norm, w_out, norm_ffn2, ffn2_gate, ffn2_up, ffn2_down, norm_final):
    batch, seq, d = x.shape
    depth = norm_ffn1.shape[0]
    assert seq % (max(dl for _, dl in DILATED_PATTERNS) * ATTN_BLOCK) == 0
    assert (batch * seq) % FFN_TILE == 0
    assert seq % TOKEN_TILE == 0 and seq % DN_TOKENS == 0 and seq % OUT_ROWS == 0
    xt = x.reshape(batch * seq, d)
    gfin = norm_final.reshape(1, d).astype(F32)

    for i in range(depth):
        row = lambda g: g.reshape(1, -1).astype(F32)
        xt = _ffn_in(xt, row(norm_ffn1[i]), ffn1_gate[i], ffn1_up[i], ffn1_down[i])
        aq, ak, av, dq, dk, dv, gate, small = _inproj(
            xt, row(norm_mix[i]), w_in[i], conv_w[i].astype(F32), seq=seq)

        attn, wo, wg2, wu2, wd2 = _attention(
            aq, ak, av, [w_out[i], ffn2_gate[i], ffn2_up[i], ffn2_down[i]], batch=batch, seq=seq)
        dn = _deltanet(dq, dk, dv, gate, small, _lane_row(a_log[i], DN_HEADS),
                       _lane_row(dt_bias[i], DN_HEADS), row(dn_norm[i]), batch=batch, seq=seq)

        xt = _ffn_out(xt, attn, dn, wo, row(norm_ffn2[i]), wg2, wu2, wd2, gfin,
                      final=(i == depth - 1))

    return xt.reshape(batch, seq, d)
```

```python
import functools

import numpy as np
import jax
import jax.numpy as jnp
from jax import lax
from jax.experimental import pallas as pl
from jax.experimental.pallas import tpu as pltpu

F32 = jnp.float32
BF16 = jnp.bfloat16

ATTN_HEADS = 8
ATTN_HEAD_DIM = 64
ATTN_WIDTH = ATTN_HEADS * ATTN_HEAD_DIM
DILATED_PATTERNS = ((128, 1), (512, 4), (2048, 16))
ATTN_BLOCK = 128
DN_HEADS = 4
DN_HEAD_DIM = 128
DN_WIDTH = DN_HEADS * DN_HEAD_DIM
DN_CHUNK = 64
CONV_WIDTH = 4
NORM_EPS = 1e-6
L2_EPS = 1e-6

LANES = 128
SUBLANES = 8
VMEM_LIMIT_BYTES = 56 * 1024 * 1024

MASK_VALUE = -1e30
LOG2_E = 1.4426950408889634

TOKEN_TILE = 512
FFN_TILE = 1024
FF_TILE = 256
DN_TOKENS = 128
HEADS_PER_SLAB = LANES // ATTN_HEAD_DIM
OUT_ROWS = 512
ATTN_GROUP = 8
DEINTERLEAVE_ROWS = 256
WEIGHT_ROWS = 64
WEIGHT_SLOTS = 6


def _rms(x):
    return x * lax.rsqrt(jnp.mean(x * x, axis=-1, keepdims=True) + NORM_EPS)


def _silu(x):
    half = 0.5 * x
    return half + half * jnp.tanh(half)


def _mm(a, b):
    return jnp.dot(a.astype(BF16), b.astype(BF16), preferred_element_type=F32)


def _resident(shape):
    nd = len(shape)
    return pl.BlockSpec(shape, lambda *_: (0,) * nd, pipeline_mode=pl.Buffered(1))


def _params(*sem):
    return pltpu.CompilerParams(dimension_semantics=sem, vmem_limit_bytes=VMEM_LIMIT_BYTES)


def _load_rows(src_ref, stage_ref, sem_ref, emit):
    slots, rows = stage_ref.shape[:2]
    chunks = src_ref.shape[0] // rows
    ahead = slots - 1
    assert src_ref.shape[0] % rows == 0 and src_ref.shape[1] == stage_ref.shape[2]

    def copy(c):
        slot = c % slots
        return pltpu.make_async_copy(src_ref.at[pl.ds(c * rows, rows), :], stage_ref.at[slot],
                                     sem_ref.at[slot])

    for c in range(min(ahead, chunks)):
        copy(c).start()

    def step(c, carry):
        @pl.when(c + ahead < chunks)
        def _():
            copy(c + ahead).start()

        copy(c).wait()
        emit(pl.ds(pl.multiple_of(c * rows, rows), rows), stage_ref[c % slots])
        return carry

    lax.fori_loop(0, chunks, step, 0)


def _load_bf16(src_ref, dst_ref, stage_ref, sem_ref):
    def emit(rows, chunk):
        dst_ref[rows, :] = chunk.astype(BF16)

    _load_rows(src_ref, stage_ref, sem_ref, emit)


def _any():
    return pl.BlockSpec(memory_space=pl.ANY)


def _swiglu_step(x, gain_ref, wg_ref, wu_ref, wd_ref, a_ref):
    h = (_rms(x) * gain_ref[...]).astype(BF16)
    d_ff = wg_ref.shape[1]
    for f0 in range(0, d_ff, FF_TILE):
        g = jnp.dot(h, wg_ref[:, f0:f0 + FF_TILE], preferred_element_type=F32)
        u = jnp.dot(h, wu_ref[:, f0:f0 + FF_TILE], preferred_element_type=F32)
        a_ref[:, f0:f0 + FF_TILE] = (_silu(g) * u).astype(BF16)
    return x + 0.5 * jnp.dot(a_ref[...], wd_ref[...], preferred_element_type=F32)


def _swiglu_scratch(tm, d, d_ff):
    return [pltpu.VMEM((tm, d_ff), BF16),
            pltpu.VMEM((d, d_ff), BF16), pltpu.VMEM((d, d_ff), BF16), pltpu.VMEM((d_ff, d), BF16),
            pltpu.VMEM((WEIGHT_SLOTS, WEIGHT_ROWS, d_ff), F32),
            pltpu.VMEM((WEIGHT_SLOTS, WEIGHT_ROWS, d), F32),
            pltpu.SemaphoreType.DMA((WEIGHT_SLOTS,))]


def _load_swiglu_weights(wg_hbm, wu_hbm, wd_hbm, wg_ref, wu_ref, wd_ref, wide_ref, narrow_ref, sem_ref):
    _load_bf16(wg_hbm, wg_ref, wide_ref, sem_ref)
    _load_bf16(wu_hbm, wu_ref, wide_ref, sem_ref)
    _load_bf16(wd_hbm, wd_ref, narrow_ref, sem_ref)


def _ffn_out_kernel(x_ref, attn_ref, dn_ref, wo_ref, gain_ref, wg_ref, wu_ref, wd_ref, gfin_ref,
                    o_ref, a_ref, *, final):
    width = attn_ref.shape[1]
    x = x_ref[...] + jnp.dot(attn_ref[...], wo_ref[:width, :], preferred_element_type=F32)
    x = x + jnp.dot(dn_ref[...], wo_ref[width:, :], preferred_element_type=F32)
    out = _swiglu_step(x, gain_ref, wg_ref, wu_ref, wd_ref, a_ref)
    if final:
        out = _rms(out) * gfin_ref[...]
    o_ref[...] = out


def _ffn_out(x, attn, dn, w_out, gain, wg, wu, wd, gfin, *, final):
    t, d = x.shape
    d_ff = wg.shape[1]
    tm = FFN_TILE
    row = lambda i: (i, 0)
    assert w_out.shape == (attn.shape[1] + dn.shape[1], d)
    return pl.pallas_call(
        functools.partial(_ffn_out_kernel, final=final),
        grid=(t // tm,),
        in_specs=[pl.BlockSpec((tm, d), row), pl.BlockSpec((tm, attn.shape[1]), row),
                  pl.BlockSpec((tm, dn.shape[1]), row), _resident(w_out.shape), _resident((1, d)),
                  _resident(wg.shape), _resident(wu.shape), _resident(wd.shape),
                  _resident((1, d))],
        out_specs=pl.BlockSpec((tm, d), row),
        out_shape=jax.ShapeDtypeStruct((t, d), F32),
        scratch_shapes=[pltpu.VMEM((tm, d_ff), BF16)],
        compiler_params=_params("arbitrary"),
        name="ffn_out",
    )(x, attn, dn, w_out, gain, wg, wu, wd, gfin)


def _ffn_in_kernel(x_ref, gain_ref, wg_hbm, wu_hbm, wd_hbm, win_ref, o_ref, wmain_ref, wsmall_ref,
                   a_ref, wg_ref, wu_ref, wd_ref, wide_ref, narrow_ref, sem_ref,
                   *, small_lo, small_hi):
    @pl.when(pl.program_id(0) == 0)
    def _():
        _load_swiglu_weights(wg_hbm, wu_hbm, wd_hbm, wg_ref, wu_ref, wd_ref, wide_ref, narrow_ref,
                             sem_ref)

    w = win_ref[...]
    wmain_ref[...] = jnp.concatenate([w[:, :small_lo], w[:, small_hi:]], axis=1).astype(BF16)
    lane = lax.broadcasted_iota(jnp.int32, (w.shape[0], LANES), 1)
    wsmall_ref[...] = jnp.where(lane < small_hi - small_lo, w[:, small_lo:small_lo + LANES],
                                0.0).astype(BF16)

    o_ref[...] = _swiglu_step(x_ref[...], gain_ref, wg_ref, wu_ref, wd_ref, a_ref)


def _ffn_in(x, gain, wg, wu, wd, w_in):
    t, d = x.shape
    d_ff = wg.shape[1]
    tm = FFN_TILE
    steps = t // tm
    row = lambda i: (i, 0)
    small_lo = 6 * ATTN_WIDTH
    small_hi = small_lo + 2 * DN_HEADS
    rows, cols = w_in.shape
    main_cols = cols - (small_hi - small_lo)
    assert small_lo % LANES == 0 and cols - small_hi == DN_WIDTH and rows % (16 * steps) == 0
    return pl.pallas_call(
        functools.partial(_ffn_in_kernel, small_lo=small_lo, small_hi=small_hi),
        grid=(steps,),
        in_specs=[pl.BlockSpec((tm, d), row), _resident((1, d)), _any(), _any(), _any(),
                  pl.BlockSpec((rows // steps, cols), row)],
        out_specs=[pl.BlockSpec((tm, d), row), pl.BlockSpec((rows // steps, main_cols), row),
                   pl.BlockSpec((rows // steps, LANES), row)],
        out_shape=[jax.ShapeDtypeStruct((t, d), F32), jax.ShapeDtypeStruct((rows, main_cols), BF16),
                   jax.ShapeDtypeStruct((rows, LANES), BF16)],
        scratch_shapes=_swiglu_scratch(tm, d, d_ff),
        compiler_params=_params("arbitrary"),
        name="ffn_in",
    )(x, gain, wg, wu, wd, w_in)


def _inproj_kernel(x_ref, gain_ref, wm_ref, ws_ref, cw_ref,
                   aq_ref, ak_ref, av_ref, dq_ref, dk_ref, dv_ref, gate_ref, small_ref,
                   xc_ref, *, tiles_per_seq):
    i = pl.program_id(0)
    tm = x_ref.shape[0]
    halo = SUBLANES

    @pl.when(i % tiles_per_seq == 0)
    def _():
        xc_ref[0:halo, :] = jnp.zeros((halo, xc_ref.shape[1]), F32)

    @pl.when(i % tiles_per_seq != 0)
    def _():
        xc_ref[0:halo, :] = xc_ref[tm:tm + halo, :]

    h = (_rms(x_ref[...]) * gain_ref[...]).astype(BF16)

    def section(j, width=ATTN_WIDTH):
        return jnp.dot(h, wm_ref[:, j * width:(j + 1) * width], preferred_element_type=F32)

    for j in range(3):
        xc_ref[halo:halo + tm, j * DN_WIDTH:(j + 1) * DN_WIDTH] = section(3 + j)

    outs = (dq_ref, dk_ref, dv_ref)
    sub = lax.broadcasted_iota(jnp.int32, (halo, DN_HEAD_DIM), 0)
    for c in range(3 * DN_HEADS):
        sl = slice(c * DN_HEAD_DIM, (c + 1) * DN_HEAD_DIM)
        cur = xc_ref[halo:halo + tm, sl]
        tail = xc_ref[0:halo, sl]
        conv = None
        for j in range(CONV_WIDTH):
            back = CONV_WIDTH - 1 - j
            if back:
                shifted = pltpu.roll(cur, back, 0)
                head = jnp.where(sub < back, pltpu.roll(tail, back, 0), shifted[0:halo])
                shifted = jnp.concatenate([head, shifted[halo:]], axis=0)
            else:
                shifted = cur
            term = shifted * cw_ref[j:j + 1, sl]
            conv = term if conv is None else conv + term
        a = _silu(conv)
        if c < 2 * DN_HEADS:
            a = a * lax.rsqrt(jnp.sum(a * a, axis=-1, keepdims=True) + L2_EPS)
        if c < DN_HEADS:
            a = a * (DN_HEAD_DIM ** -0.5)
        hs = slice((c % DN_HEADS) * DN_HEAD_DIM, (c % DN_HEADS + 1) * DN_HEAD_DIM)
        outs[c // DN_HEADS][:, hs] = a

    aq_ref[...] = section(0) * (ATTN_HEAD_DIM ** -0.5 * LOG2_E)
    ak_ref[...] = section(1)
    av_ref[...] = section(2)
    gate_ref[...] = _silu(section(6))
    small_ref[...] = jnp.dot(h, ws_ref[...], preferred_element_type=F32)


def _inproj(x, gain, w_main, w_small, conv_w, *, seq):
    t, d = x.shape
    tm = TOKEN_TILE
    row = lambda i: (i, 0)
    wide = jax.ShapeDtypeStruct((t, ATTN_WIDTH), F32)
    return pl.pallas_call(
        functools.partial(_inproj_kernel, tiles_per_seq=seq // tm),
        grid=(t // tm,),
        in_specs=[pl.BlockSpec((tm, d), row), _resident((1, d)), _resident(w_main.shape),
                  _resident(w_small.shape), _resident(conv_w.shape)],
        out_specs=[pl.BlockSpec((tm, ATTN_WIDTH), row)] * 7 + [pl.BlockSpec((tm, LANES), row)],
        out_shape=[wide] * 7 + [jax.ShapeDtypeStruct((t, LANES), F32)],
        scratch_shapes=[pltpu.VMEM((tm + 2 * SUBLANES, 3 * DN_WIDTH), F32)],
        compiler_params=_params("arbitrary"),
        name="inproj",
    )(x, gain, w_main, w_small, conv_w)


FINE = 8
COARSE = 16


def _chunks(dilation):
    return FINE // dilation


def _tile_order(dilation, size):
    c = _chunks(dilation) if dilation < FINE else 1
    per = size // c
    x = np.arange(size)
    return c * (x % per) + x // per


def _attn_group(refs, stage, bias_ref, dilation, blocks, *, init):
    fine, coarse, acc_ref, m_ref, l_ref = refs
    s_ref, p_ref, alpha_ref = stage
    d = dilation
    blk = ATTN_BLOCK
    seq = acc_ref.shape[0]
    lane = lax.broadcasted_iota(jnp.int32, (blk, LANES), 1)
    head_lanes = [(lane >= h * ATTN_HEAD_DIM) & (lane < (h + 1) * ATTN_HEAD_DIM)
                  for h in range(HEADS_PER_SLAB)]

    def gather(ref, starts, rows):
        parts = [ref[pl.ds(pl.multiple_of(st, 16), rows), :] for st in starts]
        return parts[0] if len(parts) == 1 else jnp.concatenate(parts, axis=0)

    plans = []
    for r, n in blocks:
        n_key = jnp.maximum(n - 1, 0)
        if d < FINE:
            c = _chunks(d)
            per = seq // FINE
            bases = [(r + d * b) * per for b in range(c)]
            q_starts = [base + (blk // c) * n for base in bases]
            k_starts = [base + (blk // c) * n_key for base in bases]
            q_refs, k_ref, v_ref = fine
            state = [(st, blk // c, 1) for st in q_starts]
        else:
            c = 1
            per = seq // COARSE
            q_starts = [r * per + blk * n]
            k_starts = [r * per + blk * n_key]
            q_refs, k_ref, v_ref = coarse
            ratio = d // FINE
            state = [((r % FINE) * (seq // FINE) + ratio * blk * n + r // FINE, blk, ratio)]
        plans.append((q_refs, k_ref, v_ref, q_starts, k_starts, c, state, jnp.minimum(n, 1)))

    def state_rows(ref, state):
        parts = []
        for start, rows, stride in state:
            if stride == 1:
                parts.append(ref[pl.ds(pl.multiple_of(start, 16), rows), :])
            else:
                parts.append(ref[pl.ds(start, rows, stride=stride), :])
        return parts[0] if len(parts) == 1 else jnp.concatenate(parts, axis=0)

    for i, (q_refs, k_ref, _, q_starts, k_starts, c, _, _) in enumerate(plans):
        kb = gather(k_ref, k_starts, 2 * blk // c)
        for h in range(HEADS_PER_SLAB):
            qh = gather(q_refs[h], q_starts, blk // c)
            s_ref[i * HEADS_PER_SLAB + h] = lax.dot_general(
                qh, kb, (((1,), (1,)), ((), ())), preferred_element_type=F32)

    pending = []
    for i, plan in enumerate(plans):
        state, table = plan[6], plan[7]
        m_prev = None if init else state_rows(m_ref, state)
        m_heads, l_heads = [], []
        for h in range(HEADS_PER_SLAB):
            t = i * HEADS_PER_SLAB + h
            s = s_ref[t] + bias_ref[table, h]
            top = jnp.maximum(s[:, :LANES], s[:, LANES:])
            if not init:
                top = jnp.maximum(top, jnp.where(head_lanes[h], m_prev, MASK_VALUE))
            m_new = jnp.broadcast_to(jnp.max(top, axis=-1, keepdims=True), (blk, LANES))
            p = jnp.exp2(s - jnp.concatenate([m_new, m_new], axis=1))
            l_heads.append(jnp.broadcast_to(jnp.sum(p, axis=-1, keepdims=True), (blk, LANES)))
            m_heads.append(m_new)
            p_ref[t] = p.astype(BF16)
        m_new = jnp.where(head_lanes[0], m_heads[0], m_heads[1])
        l_new = jnp.where(head_lanes[0], l_heads[0], l_heads[1])
        if not init:
            alpha = jnp.exp2(m_prev - m_new)
            alpha_ref[i] = alpha
            l_new = state_rows(l_ref, state) * alpha + l_new
        pending.append((m_ref, state, m_new))
        pending.append((l_ref, state, l_new))

    for i, plan in enumerate(plans):
        v_ref, k_starts, c, state = plan[2], plan[4], plan[5], plan[6]
        vb = gather(v_ref, k_starts, 2 * blk // c)
        pv = [jnp.dot(p_ref[i * HEADS_PER_SLAB + h], vb, preferred_element_type=F32)
              for h in range(HEADS_PER_SLAB)]
        acc_new = jnp.where(head_lanes[0], pv[0], pv[1])
        if not init:
            acc_new = state_rows(acc_ref, state) * alpha_ref[i] + acc_new
        pending.append((acc_ref, state, acc_new))

    for ref, state, value in pending:
        offset = 0
        for start, rows, stride in state:
            piece = value[offset:offset + rows]
            if stride == 1:
                ref[pl.ds(pl.multiple_of(start, 16), rows), :] = piece
            else:
                ref[pl.ds(start, rows, stride=stride), :] = piece
            offset += rows


def _deinterleave(src_ref, tmp_ref, dst_fine, dst_coarse, *, masks):
    seq = src_ref.shape[0]
    quarter = seq // 4
    piece = DEINTERLEAVE_ROWS

    for r4 in range(4):
        for off in range(0, quarter, piece):
            tmp_ref[r4 * quarter + off:r4 * quarter + off + piece, :] = \
                src_ref[pl.ds(r4 + 4 * off, piece, stride=4), :]

    for layout, dsts in ((FINE, dst_fine), (COARSE, dst_coarse)):
        per = seq // layout
        sub = layout // 4
        for res in range(layout):
            for off in range(0, per, piece):
                value = tmp_ref[pl.ds((res % 4) * quarter + res // 4 + sub * off, piece,
                                      stride=sub), :]
                rows = slice(res * per + off, res * per + off + piece)
                if masks is None:
                    dsts[0][rows, :] = value.astype(BF16)
                else:
                    for dst, mask in zip(dsts, masks):
                        dst[rows, :] = jnp.where(mask, value, 0.0).astype(BF16)


def _attn_kernel(q_ref, k_ref, v_ref, b1_ref, b4_ref, b16_ref, *rest, riders):
    for src_ref, dst_ref in zip(rest[:riders], rest[riders + 1:2 * riders + 1]):
        dst_ref[...] = src_ref[...].astype(dst_ref.dtype)
    o_ref = rest[riders]
    (qf0, qf1, kf, vf, qc0, qc1, kc, vc, tmp_ref, acc_ref, m_ref, l_ref,
     s_ref, p_ref, alpha_ref) = rest[2 * riders + 1:]
    seq = q_ref.shape[0]
    blk = ATTN_BLOCK
    grp = ATTN_GROUP
    lane = lax.broadcasted_iota(jnp.int32, (DEINTERLEAVE_ROWS, LANES), 1)
    masks = [(lane >= h * ATTN_HEAD_DIM) & (lane < (h + 1) * ATTN_HEAD_DIM)
             for h in range(HEADS_PER_SLAB)]
    _deinterleave(q_ref, tmp_ref, (qf0, qf1), (qc0, qc1), masks=masks)
    _deinterleave(k_ref, tmp_ref, (kf,), (kc,), masks=None)
    _deinterleave(v_ref, tmp_ref, (vf,), (vc,), masks=None)

    refs = (((qf0, qf1), kf, vf), ((qc0, qc1), kc, vc), acc_ref, m_ref, l_ref)
    stage = (s_ref, p_ref, alpha_ref)
    patterns = sorted(zip(DILATED_PATTERNS, (b1_ref, b4_ref, b16_ref)), key=lambda e: -e[0][1])
    for idx, ((_, d), bias_ref) in enumerate(patterns):
        nb = seq // (d * blk)
        run = functools.partial(_attn_group, refs, stage, bias_ref, d, init=(idx == 0))
        if d < grp:
            per_res = nb // grp

            def body(i, carry, run=run, per_res=per_res):
                r = i // per_res
                n0 = (i % per_res) * grp
                run([(r, n0 + j) for j in range(grp)])
                return carry

            lax.fori_loop(0, d * per_res, body, 0)
        else:
            per_blk = d // grp

            def body(i, carry, run=run, per_blk=per_blk):
                n = i // per_blk
                r0 = (i % per_blk) * grp
                run([(r0 + j, n) for j in range(grp)])
                return carry

            lax.fori_loop(0, nb * per_blk, body, 0)

    per = seq // FINE

    def out_body(i, carry):
        res = i // (per // OUT_ROWS)
        off = (i % (per // OUT_ROWS)) * OUT_ROWS
        rows = pl.ds(pl.multiple_of(res * per + off, OUT_ROWS), OUT_ROWS)
        tmp_ref[pl.ds(res + FINE * off, OUT_ROWS, stride=FINE), :] = acc_ref[rows, :] / l_ref[rows, :]
        return carry

    lax.fori_loop(0, FINE * (per // OUT_ROWS), out_body, 0)

    def cast_body(i, carry):
        rows = pl.ds(pl.multiple_of(i * OUT_ROWS, OUT_ROWS), OUT_ROWS)
        o_ref[rows, :] = tmp_ref[rows, :].astype(o_ref.dtype)
        return carry

    lax.fori_loop(0, seq // OUT_ROWS, cast_body, 0)


def _attn_bias(window, dilation):
    slopes = np.array([2.0 ** (-8.0 * (i + 1) / ATTN_HEADS) for i in range(ATTN_HEADS)],
                      dtype=np.float32)
    qi = _tile_order(dilation, ATTN_BLOCK)[:, None]
    kj = _tile_order(dilation, 2 * ATTN_BLOCK)[None, :]
    tables = []
    for steps in (qi - kj, qi + ATTN_BLOCK - kj):
        valid = (steps >= 0) & (steps <= window // dilation)
        bias = -slopes[:, None, None] * (steps * dilation).astype(np.float32) * np.float32(LOG2_E)
        tables.append(np.where(valid[None], bias, np.float32(MASK_VALUE)))
    table = np.stack(tables).astype(np.float32)
    table = table.reshape(2, ATTN_HEADS // HEADS_PER_SLAB, HEADS_PER_SLAB, *table.shape[2:])
    return jnp.asarray(table.transpose(1, 0, 2, 3, 4))


def _attention(q, k, v, weights, *, batch, seq):
    slabs = ATTN_WIDTH // LANES
    steps = batch * slabs
    rider = lambda w: pl.BlockSpec((w.shape[0] // steps, w.shape[1]),
                                   lambda bi, j: (bi * slabs + j, 0))
    assert all(w.shape[0] % (16 * steps) == 0 for w in weights)
    per_seq = pl.BlockSpec((seq, LANES), lambda bi, j: (bi, j))
    bias = [_attn_bias(w, d) for w, d in DILATED_PATTERNS]
    bias_spec = pl.BlockSpec((None,) + bias[0].shape[1:], lambda bi, j: (j, 0, 0, 0, 0))
    rows16 = pltpu.VMEM((seq, LANES), BF16)
    rows32 = pltpu.VMEM((seq, LANES), F32)
    tiles = ATTN_GROUP * HEADS_PER_SLAB
    return pl.pallas_call(
        functools.partial(_attn_kernel, riders=len(weights)),
        grid=(batch, slabs),
        in_specs=[per_seq, per_seq, per_seq, bias_spec, bias_spec, bias_spec]
        + [rider(w) for w in weights],
        out_specs=[per_seq] + [rider(w) for w in weights],
        out_shape=[jax.ShapeDtypeStruct((batch * seq, ATTN_WIDTH), BF16)]
        + [jax.ShapeDtypeStruct(w.shape, BF16) for w in weights],
        scratch_shapes=[rows16] * 8 + [rows32] * 4 + [
            pltpu.VMEM((tiles, ATTN_BLOCK, 2 * ATTN_BLOCK), F32),
            pltpu.VMEM((tiles, ATTN_BLOCK, 2 * ATTN_BLOCK), BF16),
            pltpu.VMEM((ATTN_GROUP, ATTN_BLOCK, LANES), F32),
        ],
        compiler_params=_params("arbitrary", "arbitrary"),
        name="attention",
    )(q, k, v, *bias, *weights)


def _bmm(a, b):
    return lax.dot_general(a.astype(BF16), b.astype(BF16), (((2,), (1,)), ((0,), (0,))),
                           preferred_element_type=F32)


def _bmm_nt(a, b):
    return lax.dot_general(a.astype(BF16), b.astype(BF16), (((2,), (2,)), ((0,), (0,))),
                           preferred_element_type=F32)


MASK_EYE, MASK_CAUSAL, MASK_STRICT, MASK_UPPER, MASK_JOIN = 0, 1, 2, 3, 4


def _dn_masks():
    c = DN_CHUNK
    row, col = np.indices((c, c))
    tables = [row == col, row >= col, row > col, row <= col]
    size = 1
    while size < c:
        shift = size.bit_length() - 1
        tables.append(((row >> (shift + 1)) == (col >> (shift + 1)))
                      & (((row >> shift) & 1) == 1) & (((col >> shift) & 1) == 0))
        size *= 2
    return jnp.asarray(np.stack(tables).astype(np.float32))


def _dn_spread(offset):
    e = np.zeros((LANES, DN_WIDTH), np.float32)
    for hd in range(DN_HEADS):
        e[offset + hd, hd * DN_HEAD_DIM:(hd + 1) * DN_HEAD_DIM] = 1.0
    return jnp.asarray(e, dtype=BF16)


def _unit_lower_inverse(a_strict, mask_ref):
    levels = mask_ref.shape[0] - MASK_JOIN
    t = mask_ref[MASK_EYE] - a_strict * mask_ref[MASK_JOIN]
    for level in range(1, levels):
        lk = a_strict * mask_ref[MASK_JOIN + level]
        t = t - _bmm(t, _bmm(lk, t))
    return t


def _dn_kernel(q_ref, k_ref, v_ref, gate_ref, small_ref, arow_ref, dtrow_ref, nrm_ref, mask_ref,
               ebeta_ref, eg_ref, o_ref, s_ref, wq_ref, u_ref, at_ref, kd_ref):
    c = DN_CHUNK
    dh = DN_HEAD_DIM
    batch, tc = q_ref.shape[:2]
    g_chunks = tc // c
    lanes = batch * DN_HEADS
    nb = g_chunks * lanes

    @pl.when(pl.program_id(0) == 0)
    def _():
        s_ref[...] = jnp.zeros(s_ref.shape, F32)

    sm = small_ref[...]
    beta_all = jax.nn.sigmoid(sm)
    z = sm + dtrow_ref[...]
    softplus = jnp.maximum(z, 0.0) + jnp.log1p(jnp.exp(-jnp.abs(z)))
    g_all = -jnp.exp(arow_ref[...]) * softplus

    def per_head(ref):
        parts = [ref[b, :, hd * dh:(hd + 1) * dh].reshape(g_chunks, 1, c, dh)
                 for b in range(batch) for hd in range(DN_HEADS)]
        return jnp.concatenate(parts, axis=1).reshape(nb, c, dh)

    def split3(x):
        hi = x.astype(BF16)
        rest = x - hi.astype(F32)
        mid = rest.astype(BF16)
        return hi, mid, (rest - mid.astype(F32)).astype(BF16)

    def spread(x, e_ref):
        flat = x.reshape(batch * tc, LANES)
        wide = sum(jnp.dot(part, e_ref[...], preferred_element_type=F32) for part in split3(flat))
        parts = [wide[b * tc:(b + 1) * tc, hd * dh:(hd + 1) * dh].reshape(g_chunks, 1, c, dh)
                 for b in range(batch) for hd in range(DN_HEADS)]
        return jnp.concatenate(parts, axis=1).reshape(nb, c, dh)

    q = per_head(q_ref)
    k = per_head(k_ref)
    v = per_head(v_ref)

    causal = mask_ref[MASK_CAUSAL]

    chunks = batch * g_chunks
    tri = jnp.broadcast_to(causal.astype(BF16), (chunks, c, c))
    gc_all = sum(_bmm(tri, part) for part in split3(g_all.reshape(chunks, c, LANES)))
    beta = spread(beta_all, ebeta_ref)
    gc = spread(gc_all.reshape(batch, tc, LANES), eg_ref)
    gc_mat = gc[..., :c]
    gc_row = jnp.sum(gc_mat * mask_ref[MASK_EYE], axis=1, keepdims=True)
    decay = jnp.exp((gc_mat - gc_row) * causal) * causal

    k_beta = k * beta
    e_gc = jnp.exp(gc)
    g_last = gc[:, c - 1:c, :]
    both = _bmm_nt(jnp.concatenate([k_beta, q], axis=1), k)
    a_mat = both[:, :c] * (decay * mask_ref[MASK_STRICT])
    at_ref[...] = (both[:, c:] * decay).astype(BF16)
    rhs = jnp.concatenate([v * beta, k_beta * e_gc], axis=-1)
    sol = _bmm(_unit_lower_inverse(a_mat, mask_ref), rhs)
    u_ref[...] = sol[..., :dh]
    wq_ref[:, :c, :] = sol[..., dh:].astype(BF16)
    wq_ref[:, c:, :] = (q * e_gc).astype(BF16)
    kd_ref[...] = (k * jnp.exp(g_last - gc)).astype(BF16)
    e_last = jnp.exp(g_last)

    state = s_ref[...]
    outs = []
    for n in range(g_chunks):
        sl = slice(n * lanes, (n + 1) * lanes)
        ws = _bmm(wq_ref[sl], state)
        v_new = u_ref[sl] - ws[:, :c]
        outs.append(ws[:, c:] + _bmm(at_ref[sl], v_new))
        state = state * e_last[sl] + lax.dot_general(
            kd_ref[sl], v_new.astype(BF16), (((1,), (1,)), ((0,), (0,))),
            preferred_element_type=F32)
    s_ref[...] = state

    for b in range(batch):
        for hd in range(DN_HEADS):
            sl = slice(hd * dh, (hd + 1) * dh)
            o = jnp.concatenate([outs[n][b * DN_HEADS + hd] for n in range(g_chunks)], axis=0)
            o = o * lax.rsqrt(jnp.mean(o * o, axis=-1, keepdims=True) + NORM_EPS)
            o_ref[b, :, sl] = (o * nrm_ref[...] * gate_ref[b, :, sl]).astype(o_ref.dtype)


def _deltanet(dq, dk, dv, gate, small, arow, dtrow, nrm, *, batch, seq):
    tc = DN_TOKENS
    nb = (tc // DN_CHUNK) * batch * DN_HEADS
    masks = _dn_masks()
    spread_beta = _dn_spread(0)
    spread_g = _dn_spread(DN_HEADS)
    view = lambda t: t.reshape(batch, seq, t.shape[-1])
    step = lambda j: (0, j, 0)
    wide = pl.BlockSpec((batch, tc, DN_WIDTH), step)
    out = pl.pallas_call(
        _dn_kernel,
        grid=(seq // tc,),
        in_specs=[wide, wide, wide, wide, pl.BlockSpec((batch, tc, LANES), step),
                  _resident((1, LANES)), _resident((1, LANES)), _resident((1, DN_HEAD_DIM)),
                  _resident(masks.shape), _resident(spread_beta.shape),
                  _resident(spread_g.shape)],
        out_specs=wide,
        out_shape=jax.ShapeDtypeStruct((batch, seq, DN_WIDTH), BF16),
        scratch_shapes=[
            pltpu.VMEM((batch * DN_HEADS, DN_HEAD_DIM, DN_HEAD_DIM), F32),
            pltpu.VMEM((nb, 2 * DN_CHUNK, DN_HEAD_DIM), BF16),
            pltpu.VMEM((nb, DN_CHUNK, DN_HEAD_DIM), F32),
            pltpu.VMEM((nb, DN_CHUNK, DN_CHUNK), BF16),
            pltpu.VMEM((nb, DN_CHUNK, DN_HEAD_DIM), BF16),
        ],
        compiler_params=_params("arbitrary"),
        name="deltanet",
    )(view(dq), view(dk), view(dv), view(gate), view(small), arow, dtrow, nrm, masks,
      spread_beta, spread_g)
    return out.reshape(batch * seq, DN_WIDTH)


def _lane_row(values, offset):
    return jnp.zeros((1, LANES), F32).at[0, offset:offset + values.shape[0]].set(values.astype(F32))


def kernel(x, norm_ffn1, ffn1_gate, ffn1_up, ffn1_down, norm_mix, w_in, conv_w, a_log, dt_bias,
           dn_norm, w_out, norm_ffn2, ffn2_gate, ffn2_up, ffn2_down, norm_final):
    batch, seq, d = x.shape
    depth = norm_ffn1.shape[0]
    assert seq % (max(dl for _, dl in DILATED_PATTERNS) * ATTN_BLOCK) == 0
    assert (batch * seq) % FFN_TILE == 0
    assert seq % TOKEN_TILE == 0 and seq % DN_TOKENS == 0 and seq % OUT_ROWS == 0
    xt = x.reshape(batch * seq, d)
    gfin = norm_final.reshape(1, d).astype(F32)

    for i in range(depth):
        row = lambda g: g.reshape(1, -1).astype(F32)
        xt, w_main, w_small = _ffn_in(xt, row(norm_ffn1[i]), ffn1_gate[i], ffn1_up[i],
                                      ffn1_down[i], w_in[i])
        aq, ak, av, dq, dk, dv, gate, small = _inproj(
            xt, row(norm_mix[i]), w_main, w_small, conv_w[i].astype(F32), seq=seq)

        attn, wo, wg2, wu2, wd2 = _attention(
            aq, ak, av, [w_out[i], ffn2_gate[i], ffn2_up[i], ffn2_down[i]], batch=batch, seq=seq)
        dn = _deltanet(dq, dk, dv, gate, small, _lane_row(a_log[i], DN_HEADS),
                       _lane_row(dt_bias[i], DN_HEADS), row(dn_norm[i]), batch=batch, seq=seq)

        xt = _ffn_out(xt, attn, dn, wo, row(norm_ffn2[i]), wg2, wu2, wd2, gfin,
                      final=(i == depth - 1))

    return xt.reshape(batch, seq, d)
```

```python
import functools

import numpy as np
import jax
import jax.numpy as jnp
from jax import lax
from jax.experimental import pallas as pl
from jax.experimental.pallas import tpu as pltpu

F32 = jnp.float32
BF16 = jnp.bfloat16

ATTN_HEADS = 8
ATTN_HEAD_DIM = 64
ATTN_WIDTH = ATTN_HEADS * ATTN_HEAD_DIM
DILATED_PATTERNS = ((128, 1), (512, 4), (2048, 16))
ATTN_BLOCK = 128
DN_HEADS = 4
DN_HEAD_DIM = 128
DN_WIDTH = DN_HEADS * DN_HEAD_DIM
DN_CHUNK = 64
CONV_WIDTH = 4
NORM_EPS = 1e-6
L2_EPS = 1e-6

LANES = 128
SUBLANES = 8
VMEM_LIMIT_BYTES = 56 * 1024 * 1024

MASK_VALUE = -1e30
LOG2_E = 1.4426950408889634

TOKEN_TILE = 512
FFN_TILE = 1024
FF_TILE = 256
DN_TOKENS = 128
HEADS_PER_SLAB = LANES // ATTN_HEAD_DIM
OUT_ROWS = 512
ATTN_GROUP = 8
DEINTERLEAVE_ROWS = 256
WEIGHT_ROWS = 128
WEIGHT_SLOTS = 6


def _rms(x):
    return x * lax.rsqrt(jnp.mean(x * x, axis=-1, keepdims=True) + NORM_EPS)


def _silu(x):
    half = 0.5 * x
    return half + half * jnp.tanh(half)


def _mm(a, b):
    return jnp.dot(a.astype(BF16), b.astype(BF16), preferred_element_type=F32)


def _resident(shape):
    nd = len(shape)
    return pl.BlockSpec(shape, lambda *_: (0,) * nd, pipeline_mode=pl.Buffered(1))


def _params(*sem):
    return pltpu.CompilerParams(dimension_semantics=sem, vmem_limit_bytes=VMEM_LIMIT_BYTES)


def _load_rows(src_ref, stage_ref, sem_ref, emit):
    slots, rows = stage_ref.shape[:2]
    chunks = src_ref.shape[0] // rows
    ahead = slots - 1
    assert src_ref.shape[0] % rows == 0 and src_ref.shape[1] == stage_ref.shape[2]

    def copy(c):
        slot = c % slots
        return pltpu.make_async_copy(src_ref.at[pl.ds(c * rows, rows), :], stage_ref.at[slot],
                                     sem_ref.at[slot])

    for c in range(min(ahead, chunks)):
        copy(c).start()

    def step(c, carry):
        @pl.when(c + ahead < chunks)
        def _():
            copy(c + ahead).start()

        copy(c).wait()
        emit(pl.ds(pl.multiple_of(c * rows, rows), rows), stage_ref[c % slots])
        return carry

    lax.fori_loop(0, chunks, step, 0)


def _load_bf16(src_ref, dst_ref, stage_ref, sem_ref):
    def emit(rows, chunk):
        dst_ref[rows, :] = chunk.astype(BF16)

    _load_rows(src_ref, stage_ref, sem_ref, emit)


def _any():
    return pl.BlockSpec(memory_space=pl.ANY)


def _swiglu_step(x, gain_ref, wg_ref, wu_ref, wd_ref, a_ref):
    h = (_rms(x) * gain_ref[...]).astype(BF16)
    d_ff = wg_ref.shape[1]
    for f0 in range(0, d_ff, FF_TILE):
        g = jnp.dot(h, wg_ref[:, f0:f0 + FF_TILE], preferred_element_type=F32)
        u = jnp.dot(h, wu_ref[:, f0:f0 + FF_TILE], preferred_element_type=F32)
        a_ref[:, f0:f0 + FF_TILE] = (_silu(g) * u).astype(BF16)
    return x + 0.5 * jnp.dot(a_ref[...], wd_ref[...], preferred_element_type=F32)


def _swiglu_scratch(tm, d, d_ff):
    return [pltpu.VMEM((tm, d_ff), BF16),
            pltpu.VMEM((d, d_ff), BF16), pltpu.VMEM((d, d_ff), BF16), pltpu.VMEM((d_ff, d), BF16),
            pltpu.VMEM((WEIGHT_SLOTS, WEIGHT_ROWS, d_ff), F32),
            pltpu.VMEM((WEIGHT_SLOTS, WEIGHT_ROWS, d), F32),
            pltpu.SemaphoreType.DMA((WEIGHT_SLOTS,))]


def _load_swiglu_weights(wg_hbm, wu_hbm, wd_hbm, wg_ref, wu_ref, wd_ref, wide_ref, narrow_ref, sem_ref):
    _load_bf16(wg_hbm, wg_ref, wide_ref, sem_ref)
    _load_bf16(wu_hbm, wu_ref, wide_ref, sem_ref)
    _load_bf16(wd_hbm, wd_ref, narrow_ref, sem_ref)


def _ffn_out_kernel(x_ref, attn_ref, dn_ref, wo_ref, gain_ref, wg_ref, wu_ref, wd_ref, gfin_ref,
                    o_ref, a_ref, *, final):
    width = attn_ref.shape[1]
    x = x_ref[...] + jnp.dot(attn_ref[...], wo_ref[:width, :], preferred_element_type=F32)
    x = x + jnp.dot(dn_ref[...], wo_ref[width:, :], preferred_element_type=F32)
    out = _swiglu_step(x, gain_ref, wg_ref, wu_ref, wd_ref, a_ref)
    if final:
        out = _rms(out) * gfin_ref[...]
    o_ref[...] = out


def _ffn_out(x, attn, dn, w_out, gain, wg, wu, wd, gfin, *, final):
    t, d = x.shape
    d_ff = wg.shape[1]
    tm = FFN_TILE
    row = lambda i: (i, 0)
    assert w_out.shape == (attn.shape[1] + dn.shape[1], d)
    return pl.pallas_call(
        functools.partial(_ffn_out_kernel, final=final),
        grid=(t // tm,),
        in_specs=[pl.BlockSpec((tm, d), row), pl.BlockSpec((tm, attn.shape[1]), row),
                  pl.BlockSpec((tm, dn.shape[1]), row), _resident(w_out.shape), _resident((1, d)),
                  _resident(wg.shape), _resident(wu.shape), _resident(wd.shape),
                  _resident((1, d))],
        out_specs=pl.BlockSpec((tm, d), row),
        out_shape=jax.ShapeDtypeStruct((t, d), F32),
        scratch_shapes=[pltpu.VMEM((tm, d_ff), BF16)],
        compiler_params=_params("arbitrary"),
        name="ffn_out",
    )(x, attn, dn, w_out, gain, wg, wu, wd, gfin)


def _ffn_in_kernel(x_ref, gain_ref, wg_hbm, wu_hbm, wd_hbm, win_ref, o_ref, wmain_ref, wsmall_ref,
                   a_ref, wg_ref, wu_ref, wd_ref, wide_ref, narrow_ref, sem_ref,
                   *, small_lo, small_hi):
    @pl.when(pl.program_id(0) == 0)
    def _():
        _load_swiglu_weights(wg_hbm, wu_hbm, wd_hbm, wg_ref, wu_ref, wd_ref, wide_ref, narrow_ref,
                             sem_ref)

    w = win_ref[...]
    wmain_ref[...] = jnp.concatenate([w[:, :small_lo], w[:, small_hi:]], axis=1).astype(BF16)
    lane = lax.broadcasted_iota(jnp.int32, (w.shape[0], LANES), 1)
    wsmall_ref[...] = jnp.where(lane < small_hi - small_lo, w[:, small_lo:small_lo + LANES],
                                0.0).astype(BF16)

    o_ref[...] = _swiglu_step(x_ref[...], gain_ref, wg_ref, wu_ref, wd_ref, a_ref)


def _ffn_in(x, gain, wg, wu, wd, w_in):
    t, d = x.shape
    d_ff = wg.shape[1]
    tm = FFN_TILE
    steps = t // tm
    row = lambda i: (i, 0)
    small_lo = 6 * ATTN_WIDTH
    small_hi = small_lo + 2 * DN_HEADS
    rows, cols = w_in.shape
    main_cols = cols - (small_hi - small_lo)
    assert small_lo % LANES == 0 and cols - small_hi == DN_WIDTH and rows % (16 * steps) == 0
    return pl.pallas_call(
        functools.partial(_ffn_in_kernel, small_lo=small_lo, small_hi=small_hi),
        grid=(steps,),
        in_specs=[pl.BlockSpec((tm, d), row), _resident((1, d)), _any(), _any(), _any(),
                  pl.BlockSpec((rows // steps, cols), row)],
        out_specs=[pl.BlockSpec((tm, d), row), pl.BlockSpec((rows // steps, main_cols), row),
                   pl.BlockSpec((rows // steps, LANES), row)],
        out_shape=[jax.ShapeDtypeStruct((t, d), F32), jax.ShapeDtypeStruct((rows, main_cols), BF16),
                   jax.ShapeDtypeStruct((rows, LANES), BF16)],
        scratch_shapes=_swiglu_scratch(tm, d, d_ff),
        compiler_params=_params("arbitrary"),
        name="ffn_in",
    )(x, gain, wg, wu, wd, w_in)


def _inproj_kernel(x_ref, gain_ref, wm_ref, ws_ref, cw_ref,
                   aq_ref, ak_ref, av_ref, dq_ref, dk_ref, dv_ref, gate_ref, small_ref,
                   xc_ref, *, tiles_per_seq):
    i = pl.program_id(0)
    tm = x_ref.shape[0]
    halo = SUBLANES

    @pl.when(i % tiles_per_seq == 0)
    def _():
        xc_ref[0:halo, :] = jnp.zeros((halo, xc_ref.shape[1]), F32)

    @pl.when(i % tiles_per_seq != 0)
    def _():
        xc_ref[0:halo, :] = xc_ref[tm:tm + halo, :]

    h = (_rms(x_ref[...]) * gain_ref[...]).astype(BF16)

    def section(j, width=ATTN_WIDTH):
        return jnp.dot(h, wm_ref[:, j * width:(j + 1) * width], preferred_element_type=F32)

    for j in range(3):
        xc_ref[halo:halo + tm, j * DN_WIDTH:(j + 1) * DN_WIDTH] = section(3 + j)

    outs = (dq_ref, dk_ref, dv_ref)
    sub = lax.broadcasted_iota(jnp.int32, (halo, DN_HEAD_DIM), 0)
    for c in range(3 * DN_HEADS):
        sl = slice(c * DN_HEAD_DIM, (c + 1) * DN_HEAD_DIM)
        cur = xc_ref[halo:halo + tm, sl]
        tail = xc_ref[0:halo, sl]
        conv = None
        for j in range(CONV_WIDTH):
            back = CONV_WIDTH - 1 - j
            if back:
                shifted = pltpu.roll(cur, back, 0)
                head = jnp.where(sub < back, pltpu.roll(tail, back, 0), shifted[0:halo])
                shifted = jnp.concatenate([head, shifted[halo:]], axis=0)
            else:
                shifted = cur
            term = shifted * cw_ref[j:j + 1, sl]
            conv = term if conv is None else conv + term
        a = _silu(conv)
        if c < 2 * DN_HEADS:
            a = a * lax.rsqrt(jnp.sum(a * a, axis=-1, keepdims=True) + L2_EPS)
        if c < DN_HEADS:
            a = a * (DN_HEAD_DIM ** -0.5)
        hs = slice((c % DN_HEADS) * DN_HEAD_DIM, (c % DN_HEADS + 1) * DN_HEAD_DIM)
        outs[c // DN_HEADS][:, hs] = a.astype(outs[c // DN_HEADS].dtype)

    aq_ref[...] = section(0) * (ATTN_HEAD_DIM ** -0.5 * LOG2_E)
    ak_ref[...] = section(1)
    av_ref[...] = section(2)
    gate_ref[...] = _silu(section(6)).astype(gate_ref.dtype)
    small_ref[...] = jnp.dot(h, ws_ref[...], preferred_element_type=F32)


def _inproj(x, gain, w_main, w_small, conv_w, *, seq):
    t, d = x.shape
    tm = TOKEN_TILE
    row = lambda i: (i, 0)
    wide = jax.ShapeDtypeStruct((t, ATTN_WIDTH), F32)
    return pl.pallas_call(
        functools.partial(_inproj_kernel, tiles_per_seq=seq // tm),
        grid=(t // tm,),
        in_specs=[pl.BlockSpec((tm, d), row), _resident((1, d)), _resident(w_main.shape),
                  _resident(w_small.shape), _resident(conv_w.shape)],
        out_specs=[pl.BlockSpec((tm, ATTN_WIDTH), row)] * 7 + [pl.BlockSpec((tm, LANES), row)],
        out_shape=[wide] * 3 + [jax.ShapeDtypeStruct((t, ATTN_WIDTH), BF16)] * 4
        + [jax.ShapeDtypeStruct((t, LANES), F32)],
        scratch_shapes=[pltpu.VMEM((tm + 2 * SUBLANES, 3 * DN_WIDTH), F32)],
        compiler_params=_params("arbitrary"),
        name="inproj",
    )(x, gain, w_main, w_small, conv_w)


FINE = 8
COARSE = 16


def _chunks(dilation):
    return FINE // dilation


def _tile_order(dilation, size):
    c = _chunks(dilation) if dilation < FINE else 1
    per = size // c
    x = np.arange(size)
    return c * (x % per) + x // per


def _attn_group(refs, stage, bias_ref, dilation, blocks, *, init):
    fine, coarse, acc_ref, m_ref, l_ref = refs
    s_ref, p_ref, alpha_ref = stage
    d = dilation
    blk = ATTN_BLOCK
    seq = acc_ref.shape[0]
    lane = lax.broadcasted_iota(jnp.int32, (blk, LANES), 1)
    head_lanes = [(lane >= h * ATTN_HEAD_DIM) & (lane < (h + 1) * ATTN_HEAD_DIM)
                  for h in range(HEADS_PER_SLAB)]

    def gather(ref, starts, rows):
        parts = [ref[pl.ds(pl.multiple_of(st, 16), rows), :] for st in starts]
        return parts[0] if len(parts) == 1 else jnp.concatenate(parts, axis=0)

    plans = []
    for r, n in blocks:
        n_key = jnp.maximum(n - 1, 0)
        if d < FINE:
            c = _chunks(d)
            per = seq // FINE
            bases = [(r + d * b) * per for b in range(c)]
            q_starts = [base + (blk // c) * n for base in bases]
            k_starts = [base + (blk // c) * n_key for base in bases]
            q_refs, k_ref, v_ref = fine
            state = [(st, blk // c, 1) for st in q_starts]
        else:
            c = 1
            per = seq // COARSE
            q_starts = [r * per + blk * n]
            k_starts = [r * per + blk * n_key]
            q_refs, k_ref, v_ref = coarse
            ratio = d // FINE
            state = [((r % FINE) * (seq // FINE) + ratio * blk * n + r // FINE, blk, ratio)]
        plans.append((q_refs, k_ref, v_ref, q_starts, k_starts, c, state, jnp.minimum(n, 1)))

    def state_rows(ref, state):
        parts = []
        for start, rows, stride in state:
            if stride == 1:
                parts.append(ref[pl.ds(pl.multiple_of(start, 16), rows), :])
            else:
                parts.append(ref[pl.ds(start, rows, stride=stride), :])
        return parts[0] if len(parts) == 1 else jnp.concatenate(parts, axis=0)

    for i, (q_refs, k_ref, _, q_starts, k_starts, c, _, _) in enumerate(plans):
        kb = gather(k_ref, k_starts, 2 * blk // c)
        for h in range(HEADS_PER_SLAB):
            qh = gather(q_refs[h], q_starts, blk // c)
            s_ref[i * HEADS_PER_SLAB + h] = lax.dot_general(
                qh, kb, (((1,), (1,)), ((), ())), preferred_element_type=F32)

    pending = []
    for i, plan in enumerate(plans):
        state, table = plan[6], plan[7]
        m_prev = None if init else state_rows(m_ref, state)
        m_heads, l_heads = [], []
        for h in range(HEADS_PER_SLAB):
            t = i * HEADS_PER_SLAB + h
            s = s_ref[t] + bias_ref[table, h]
            top = jnp.maximum(s[:, :LANES], s[:, LANES:])
            if not init:
                top = jnp.maximum(top, jnp.where(head_lanes[h], m_prev, MASK_VALUE))
            m_new = jnp.broadcast_to(jnp.max(top, axis=-1, keepdims=True), (blk, LANES))
            p = jnp.exp2(s - jnp.concatenate([m_new, m_new], axis=1))
            l_heads.append(jnp.broadcast_to(jnp.sum(p, axis=-1, keepdims=True), (blk, LANES)))
            m_heads.append(m_new)
            p_ref[t] = p.astype(BF16)
        m_new = jnp.where(head_lanes[0], m_heads[0], m_heads[1])
        l_new = jnp.where(head_lanes[0], l_heads[0], l_heads[1])
        if not init:
            alpha = jnp.exp2(m_prev - m_new)
            alpha_ref[i] = alpha
            l_new = state_rows(l_ref, state) * alpha + l_new
        pending.append((m_ref, state, m_new))
        pending.append((l_ref, state, l_new))

    for i, plan in enumerate(plans):
        v_ref, k_starts, c, state = plan[2], plan[4], plan[5], plan[6]
        vb = gather(v_ref, k_starts, 2 * blk // c)
        pv = [jnp.dot(p_ref[i * HEADS_PER_SLAB + h], vb, preferred_element_type=F32)
              for h in range(HEADS_PER_SLAB)]
        acc_new = jnp.where(head_lanes[0], pv[0], pv[1])
        if not init:
            acc_new = state_rows(acc_ref, state) * alpha_ref[i] + acc_new
        pending.append((acc_ref, state, acc_new))

    for ref, state, value in pending:
        offset = 0
        for start, rows, stride in state:
            piece = value[offset:offset + rows]
            if stride == 1:
                ref[pl.ds(pl.multiple_of(start, 16), rows), :] = piece
            else:
                ref[pl.ds(start, rows, stride=stride), :] = piece
            offset += rows


def _deinterleave(src_ref, tmp_ref, dst_fine, dst_coarse, *, masks):
    seq = src_ref.shape[0]
    quarter = seq // 4
    piece = DEINTERLEAVE_ROWS

    for r4 in range(4):
        for off in range(0, quarter, piece):
            tmp_ref[r4 * quarter + off:r4 * quarter + off + piece, :] = \
                src_ref[pl.ds(r4 + 4 * off, piece, stride=4), :]

    for layout, dsts in ((FINE, dst_fine), (COARSE, dst_coarse)):
        per = seq // layout
        sub = layout // 4
        for res in range(layout):
            for off in range(0, per, piece):
                value = tmp_ref[pl.ds((res % 4) * quarter + res // 4 + sub * off, piece,
                                      stride=sub), :]
                rows = slice(res * per + off, res * per + off + piece)
                if masks is None:
                    dsts[0][rows, :] = value.astype(BF16)
                else:
                    for dst, mask in zip(dsts, masks):
                        dst[rows, :] = jnp.where(mask, value, 0.0).astype(BF16)


def _attn_kernel(q_ref, k_ref, v_ref, b1_ref, b4_ref, b16_ref, *rest, riders):
    for src_ref, dst_ref in zip(rest[:riders], rest[riders + 1:2 * riders + 1]):
        dst_ref[...] = src_ref[...].astype(dst_ref.dtype)
    o_ref = rest[riders]
    (qf0, qf1, kf, vf, qc0, qc1, kc, vc, tmp_ref, acc_ref, m_ref, l_ref,
     s_ref, p_ref, alpha_ref) = rest[2 * riders + 1:]
    seq = q_ref.shape[0]
    blk = ATTN_BLOCK
    grp = ATTN_GROUP
    lane = lax.broadcasted_iota(jnp.int32, (DEINTERLEAVE_ROWS, LANES), 1)
    masks = [(lane >= h * ATTN_HEAD_DIM) & (lane < (h + 1) * ATTN_HEAD_DIM)
             for h in range(HEADS_PER_SLAB)]
    _deinterleave(q_ref, tmp_ref, (qf0, qf1), (qc0, qc1), masks=masks)
    _deinterleave(k_ref, tmp_ref, (kf,), (kc,), masks=None)
    _deinterleave(v_ref, tmp_ref, (vf,), (vc,), masks=None)

    refs = (((qf0, qf1), kf, vf), ((qc0, qc1), kc, vc), acc_ref, m_ref, l_ref)
    stage = (s_ref, p_ref, alpha_ref)
    patterns = sorted(zip(DILATED_PATTERNS, (b1_ref, b4_ref, b16_ref)), key=lambda e: -e[0][1])
    for idx, ((_, d), bias_ref) in enumerate(patterns):
        nb = seq // (d * blk)
        run = functools.partial(_attn_group, refs, stage, bias_ref, d, init=(idx == 0))
        if d < grp:
            per_res = nb // grp

            def body(i, carry, run=run, per_res=per_res):
                r = i // per_res
                n0 = (i % per_res) * grp
                run([(r, n0 + j) for j in range(grp)])
                return carry

            lax.fori_loop(0, d * per_res, body, 0)
        else:
            per_blk = d // grp

            def body(i, carry, run=run, per_blk=per_blk):
                n = i // per_blk
                r0 = (i % per_blk) * grp
                run([(r0 + j, n) for j in range(grp)])
                return carry

            lax.fori_loop(0, nb * per_blk, body, 0)

    per = seq // FINE

    def out_body(i, carry):
        res = i // (per // OUT_ROWS)
        off = (i % (per // OUT_ROWS)) * OUT_ROWS
        rows = pl.ds(pl.multiple_of(res * per + off, OUT_ROWS), OUT_ROWS)
        tmp_ref[pl.ds(res + FINE * off, OUT_ROWS, stride=FINE), :] = acc_ref[rows, :] / l_ref[rows, :]
        return carry

    lax.fori_loop(0, FINE * (per // OUT_ROWS), out_body, 0)

    def cast_body(i, carry):
        rows = pl.ds(pl.multiple_of(i * OUT_ROWS, OUT_ROWS), OUT_ROWS)
        o_ref[rows, :] = tmp_ref[rows, :].astype(o_ref.dtype)
        return carry

    lax.fori_loop(0, seq // OUT_ROWS, cast_body, 0)


def _attn_bias(window, dilation):
    slopes = np.array([2.0 ** (-8.0 * (i + 1) / ATTN_HEADS) for i in range(ATTN_HEADS)],
                      dtype=np.float32)
    qi = _tile_order(dilation, ATTN_BLOCK)[:, None]
    kj = _tile_order(dilation, 2 * ATTN_BLOCK)[None, :]
    tables = []
    for steps in (qi - kj, qi + ATTN_BLOCK - kj):
        valid = (steps >= 0) & (steps <= window // dilation)
        bias = -slopes[:, None, None] * (steps * dilation).astype(np.float32) * np.float32(LOG2_E)
        tables.append(np.where(valid[None], bias, np.float32(MASK_VALUE)))
    table = np.stack(tables).astype(np.float32)
    table = table.reshape(2, ATTN_HEADS // HEADS_PER_SLAB, HEADS_PER_SLAB, *table.shape[2:])
    return jnp.asarray(table.transpose(1, 0, 2, 3, 4))


def _attention(q, k, v, weights, *, batch, seq):
    slabs = ATTN_WIDTH // LANES
    steps = batch * slabs
    rider = lambda w: pl.BlockSpec((w.shape[0] // steps, w.shape[1]),
                                   lambda bi, j: (bi * slabs + j, 0))
    assert all(w.shape[0] % (16 * steps) == 0 for w in weights)
    per_seq = pl.BlockSpec((seq, LANES), lambda bi, j: (bi, j))
    bias = [_attn_bias(w, d) for w, d in DILATED_PATTERNS]
    bias_spec = pl.BlockSpec((None,) + bias[0].shape[1:], lambda bi, j: (j, 0, 0, 0, 0))
    rows16 = pltpu.VMEM((seq, LANES), BF16)
    rows32 = pltpu.VMEM((seq, LANES), F32)
    tiles = ATTN_GROUP * HEADS_PER_SLAB
    return pl.pallas_call(
        functools.partial(_attn_kernel, riders=len(weights)),
        grid=(batch, slabs),
        in_specs=[per_seq, per_seq, per_seq, bias_spec, bias_spec, bias_spec]
        + [rider(w) for w in weights],
        out_specs=[per_seq] + [rider(w) for w in weights],
        out_shape=[jax.ShapeDtypeStruct((batch * seq, ATTN_WIDTH), BF16)]
        + [jax.ShapeDtypeStruct(w.shape, BF16) for w in weights],
        scratch_shapes=[rows16] * 8 + [rows32] * 4 + [
            pltpu.VMEM((tiles, ATTN_BLOCK, 2 * ATTN_BLOCK), F32),
            pltpu.VMEM((tiles, ATTN_BLOCK, 2 * ATTN_BLOCK), BF16),
            pltpu.VMEM((ATTN_GROUP, ATTN_BLOCK, LANES), F32),
        ],
        compiler_params=_params("arbitrary", "arbitrary"),
        name="attention",
    )(q, k, v, *bias, *weights)


def _bmm(a, b):
    return lax.dot_general(a.astype(BF16), b.astype(BF16), (((2,), (1,)), ((0,), (0,))),
                           preferred_element_type=F32)


def _bmm_nt(a, b):
    return lax.dot_general(a.astype(BF16), b.astype(BF16), (((2,), (2,)), ((0,), (0,))),
                           preferred_element_type=F32)


MASK_EYE, MASK_CAUSAL, MASK_STRICT, MASK_UPPER, MASK_JOIN = 0, 1, 2, 3, 4


def _dn_masks():
    c = DN_CHUNK
    row, col = np.indices((c, c))
    tables = [row == col, row >= col, row > col, row <= col]
    size = 1
    while size < c:
        shift = size.bit_length() - 1
        tables.append(((row >> (shift + 1)) == (col >> (shift + 1)))
                      & (((row >> shift) & 1) == 1) & (((col >> shift) & 1) == 0))
        size *= 2
    return jnp.asarray(np.stack(tables).astype(np.float32))


def _dn_spread(offset):
    e = np.zeros((LANES, DN_WIDTH), np.float32)
    for hd in range(DN_HEADS):
        e[offset + hd, hd * DN_HEAD_DIM:(hd + 1) * DN_HEAD_DIM] = 1.0
    return jnp.asarray(e, dtype=BF16)


def _unit_lower_inverse(a_strict, mask_ref):
    levels = mask_ref.shape[0] - MASK_JOIN
    t = mask_ref[MASK_EYE] - a_strict * mask_ref[MASK_JOIN]
    for level in range(1, levels):
        lk = a_strict * mask_ref[MASK_JOIN + level]
        t = t - _bmm(t, _bmm(lk, t))
    return t


def _dn_kernel(q_ref, k_ref, v_ref, gate_ref, small_ref, arow_ref, dtrow_ref, nrm_ref, mask_ref,
               ebeta_ref, eg_ref, o_ref, s_ref, wq_ref, u_ref, at_ref, kd_ref):
    c = DN_CHUNK
    dh = DN_HEAD_DIM
    batch, tc = q_ref.shape[:2]
    g_chunks = tc // c
    lanes = batch * DN_HEADS
    nb = g_chunks * lanes

    @pl.when(pl.program_id(0) == 0)
    def _():
        s_ref[...] = jnp.zeros(s_ref.shape, F32)

    sm = small_ref[...]
    beta_all = jax.nn.sigmoid(sm)
    z = sm + dtrow_ref[...]
    softplus = jnp.maximum(z, 0.0) + jnp.log1p(jnp.exp(-jnp.abs(z)))
    g_all = -jnp.exp(arow_ref[...]) * softplus

    def per_head(ref):
        parts = [ref[b, :, hd * dh:(hd + 1) * dh].astype(F32).reshape(g_chunks, 1, c, dh)
                 for b in range(batch) for hd in range(DN_HEADS)]
        return jnp.concatenate(parts, axis=1).reshape(nb, c, dh)

    def split3(x):
        hi = x.astype(BF16)
        rest = x - hi.astype(F32)
        mid = rest.astype(BF16)
        return hi, mid, (rest - mid.astype(F32)).astype(BF16)

    def spread(x, e_ref):
        flat = x.reshape(batch * tc, LANES)
        wide = sum(jnp.dot(part, e_ref[...], preferred_element_type=F32) for part in split3(flat))
        parts = [wide[b * tc:(b + 1) * tc, hd * dh:(hd + 1) * dh].reshape(g_chunks, 1, c, dh)
                 for b in range(batch) for hd in range(DN_HEADS)]
        return jnp.concatenate(parts, axis=1).reshape(nb, c, dh)

    q = per_head(q_ref)
    k = per_head(k_ref)
    v = per_head(v_ref)

    causal = mask_ref[MASK_CAUSAL]

    chunks = batch * g_chunks
    tri = jnp.broadcast_to(causal.astype(BF16), (chunks, c, c))
    gc_all = sum(_bmm(tri, part) for part in split3(g_all.reshape(chunks, c, LANES)))
    beta = spread(beta_all, ebeta_ref)
    gc = spread(gc_all.reshape(batch, tc, LANES), eg_ref)
    gc_mat = gc[..., :c]
    gc_row = jnp.sum(gc_mat * mask_ref[MASK_EYE], axis=1, keepdims=True)
    decay = jnp.exp((gc_mat - gc_row) * causal) * causal

    k_beta = k * beta
    e_gc = jnp.exp(gc)
    g_last = gc[:, c - 1:c, :]
    both = _bmm_nt(jnp.concatenate([k_beta, q], axis=1), k)
    a_mat = both[:, :c] * (decay * mask_ref[MASK_STRICT])
    at_ref[...] = (both[:, c:] * decay).astype(BF16)
    rhs = jnp.concatenate([v * beta, k_beta * e_gc], axis=-1)
    sol = _bmm(_unit_lower_inverse(a_mat, mask_ref), rhs)
    u_ref[...] = sol[..., :dh]
    wq_ref[:, :c, :] = sol[..., dh:].astype(BF16)
    wq_ref[:, c:, :] = (q * e_gc).astype(BF16)
    kd_ref[...] = (k * jnp.exp(g_last - gc)).astype(BF16)
    e_last = jnp.exp(g_last)

    state = s_ref[...]
    outs = []
    for n in range(g_chunks):
        sl = slice(n * lanes, (n + 1) * lanes)
        ws = _bmm(wq_ref[sl], state)
        v_new = u_ref[sl] - ws[:, :c]
        outs.append(ws[:, c:] + _bmm(at_ref[sl], v_new))
        state = state * e_last[sl] + lax.dot_general(
            kd_ref[sl], v_new.astype(BF16), (((1,), (1,)), ((0,), (0,))),
            preferred_element_type=F32)
    s_ref[...] = state

    for b in range(batch):
        for hd in range(DN_HEADS):
            sl = slice(hd * dh, (hd + 1) * dh)
            o = jnp.concatenate([outs[n][b * DN_HEADS + hd] for n in range(g_chunks)], axis=0)
            o = o * lax.rsqrt(jnp.mean(o * o, axis=-1, keepdims=True) + NORM_EPS)
            o_ref[b, :, sl] = (o * nrm_ref[...] * gate_ref[b, :, sl]).astype(o_ref.dtype)


def _deltanet(dq, dk, dv, gate, small, arow, dtrow, nrm, *, batch, seq):
    tc = DN_TOKENS
    nb = (tc // DN_CHUNK) * batch * DN_HEADS
    masks = _dn_masks()
    spread_beta = _dn_spread(0)
    spread_g = _dn_spread(DN_HEADS)
    view = lambda t: t.reshape(batch, seq, t.shape[-1])
    step = lambda j: (0, j, 0)
    wide = pl.BlockSpec((batch, tc, DN_WIDTH), step)
    out = pl.pallas_call(
        _dn_kernel,
        grid=(seq // tc,),
        in_specs=[wide, wide, wide, wide, pl.BlockSpec((batch, tc, LANES), step),
                  _resident((1, LANES)), _resident((1, LANES)), _resident((1, DN_HEAD_DIM)),
                  _resident(masks.shape), _resident(spread_beta.shape),
                  _resident(spread_g.shape)],
        out_specs=wide,
        out_shape=jax.ShapeDtypeStruct((batch, seq, DN_WIDTH), BF16),
        scratch_shapes=[
            pltpu.VMEM((batch * DN_HEADS, DN_HEAD_DIM, DN_HEAD_DIM), F32),
            pltpu.VMEM((nb, 2 * DN_CHUNK, DN_HEAD_DIM), BF16),
            pltpu.VMEM((nb, DN_CHUNK, DN_HEAD_DIM), F32),
            pltpu.VMEM((nb, DN_CHUNK, DN_CHUNK), BF16),
            pltpu.VMEM((nb, DN_CHUNK, DN_HEAD_DIM), BF16),
        ],
        compiler_params=_params("arbitrary"),
        name="deltanet",
    )(view(dq), view(dk), view(dv), view(gate), view(small), arow, dtrow, nrm, masks,
      spread_beta, spread_g)
    return out.reshape(batch * seq, DN_WIDTH)


def _lane_row(values, offset):
    return jnp.zeros((1, LANES), F32).at[0, offset:offset + values.shape[0]].set(values.astype(F32))


def kernel(x, norm_ffn1, ffn1_gate, ffn1_up, ffn1_down, norm_mix, w_in, conv_w, a_log, dt_bias,
           dn_norm, w_out, norm_ffn2, ffn2_gate, ffn2_up, ffn2_down, norm_final):
    batch, seq, d = x.shape
    depth = norm_ffn1.shape[0]
    assert seq % (max(dl for _, dl in DILATED_PATTERNS) * ATTN_BLOCK) == 0
    assert (batch * seq) % FFN_TILE == 0
    assert seq % TOKEN_TILE == 0 and seq % DN_TOKENS == 0 and seq % OUT_ROWS == 0
    xt = x.reshape(batch * seq, d)
    gfin = norm_final.reshape(1, d).astype(F32)

    for i in range(depth):
        row = lambda g: g.reshape(1, -1).astype(F32)
        xt, w_main, w_small = _ffn_in(xt, row(norm_ffn1[i]), ffn1_gate[i], ffn1_up[i],
                                      ffn1_down[i], w_in[i])
        aq, ak, av, dq, dk, dv, gate, small = _inproj(
            xt, row(norm_mix[i]), w_main, w_small, conv_w[i].astype(F32), seq=seq)

        attn, wo, wg2, wu2, wd2 = _attention(
            aq, ak, av, [w_out[i], ffn2_gate[i], ffn2_up[i], ffn2_down[i]], batch=batch, seq=seq)
        dn = _deltanet(dq, dk, dv, gate, small, _lane_row(a_log[i], DN_HEADS),
                       _lane_row(dt_bias[i], DN_HEADS), row(dn_norm[i]), batch=batch, seq=seq)

        xt = _ffn_out(xt, attn, dn, wo, row(norm_ffn2[i]), wg2, wu2, wd2, gfin,
                      final=(i == depth - 1))

    return xt.reshape(batch, seq, d)
```
